```python
import math
import jax, jax.numpy as jnp
from jax import lax
import numpy as np

D_MODEL = 1024
BATCH = 16
SEQ = 256
DEPTH = 4
DEC_BATCH = 4
DEC_SEQ = 1024
PAST_LEN = 256

GRID_W = 64
EPS = 1e-6
D_HY = 512
HY_SHORT = 3
N_BANDS = 8
FILT_EMB = 1 + 2 * N_BANDS
FILT_HID = 64
DECAY_SLOW = -math.log(1e-2) / 1.5
DECAY_FAST = -math.log(1e-2) / 0.3
D_FN = 512
N_FN_GROUPS = 4
FN_GROUP = D_FN // N_FN_GROUPS
D_RG = 1024
N_RG_BLOCKS = 8
RG_BLOCK = D_RG // N_RG_BLOCKS
RG_CONV = 4
RG_C = 8.0
D_FF = -(-8 * D_MODEL // (3 * 256)) * 256
D_IN = 3 * D_HY + D_FN + 2 * D_RG + 3 * D_MODEL
SPLITS = (3 * D_HY, 3 * D_HY + D_FN, 3 * D_HY + D_FN + D_RG, 3 * D_HY + D_FN + 2 * D_RG)

kernel_name = 'hybrid_hyena_fnet_rglru_prefix_step'


def rmsnorm(x, g):
    xf = x.astype(jnp.float32)
    y = xf * lax.rsqrt(jnp.mean(xf * xf, axis=-1, keepdims=True) + EPS)
    return (y * g.astype(jnp.float32)).astype(x.dtype)


def depthwise_conv(x, w, b, pad_l, pad_r):
    L = x.shape[1]
    xp = jnp.pad(x, ((0, 0), (pad_l, pad_r), (0, 0)))
    out = b
    for k in range(w.shape[0]):
        out = out + xp[:, k:k + L] * w[k]
    return out


def hyena_filters(L, fw1, fb1, fw2, fb2, fw3, ffreq):
    f32 = jnp.float32
    t = jnp.arange(L, dtype=f32) / L
    ang = 2.0 * math.pi * t[:, None] * jnp.arange(1, N_BANDS + 1, dtype=f32)
    feats = jnp.concatenate([t[:, None], jnp.sin(ang), jnp.cos(ang)], axis=-1)
    freq = ffreq.astype(f32)
    h = jnp.sin(freq * (feats @ fw1.astype(f32) + fb1.astype(f32)))
    h = jnp.sin(freq * (h @ fw2.astype(f32) + fb2.astype(f32)))
    h = (h @ fw3.astype(f32)).reshape(L, 2, 2, D_HY)
    deltas = jnp.linspace(DECAY_SLOW, DECAY_FAST, D_HY, dtype=f32)
    h = h * jnp.exp(-t[:, None] * deltas)[:, None, None, :]
    fwd = h[:, :, 0]
    bwd = h[1:, :, 1][::-1]
    k = jnp.concatenate([fwd, jnp.zeros((1, 2, D_HY), f32), bwd], axis=0)
    k = k * lax.rsqrt(jnp.sum(k * k, axis=0, keepdims=True) + EPS)
    return jnp.fft.rfft(k, axis=0)


def fft_conv(z, kf):
    L = z.shape[1]
    zf = jnp.fft.rfft(z, n=2 * L, axis=1)
    return jnp.fft.irfft(zf * kf, n=2 * L, axis=1)[:, :L]


def hyena_branch(u, conv_w, conv_b, kf, bias):
    u = depthwise_conv(u, conv_w, conv_b, HY_SHORT // 2, HY_SHORT - 1 - HY_SHORT // 2).astype(jnp.float32)
    v, x1, x2 = jnp.split(u, 3, axis=-1)
    bias = bias.astype(jnp.float32)
    z = x1 * (fft_conv(v, kf[:, 0]) + bias[0] * v)
    z = x2 * (fft_conv(z, kf[:, 1]) + bias[1] * z)
    return z


def fnet_branch(u):
    B, L, _ = u.shape
    g = u.astype(jnp.float32).reshape(B, L, N_FN_GROUPS, FN_GROUP)
    return jnp.real(jnp.fft.fft2(g, axes=(1, 3), norm='ortho')).reshape(B, L, D_FN)


def _lin_combine(e1, e2):
    a1, b1 = e1
    a2, b2 = e2
    return a1 * a2, a2 * b1 + b2


def rglru_scan(x, w_r, b_r, w_i, b_i, lam, h0, reverse, reset):
    f32 = jnp.float32
    B, L, _ = x.shape
    xb = x.reshape(B, L, N_RG_BLOCKS, RG_BLOCK)
    r = jax.nn.sigmoid(jnp.einsum('blnc,ncd->blnd', xb, w_r.astype(f32)).reshape(B, L, D_RG) + b_r.astype(f32))
    i = jax.nn.sigmoid(jnp.einsum('blnc,ncd->blnd', xb, w_i.astype(f32)).reshape(B, L, D_RG) + b_i.astype(f32))
    log_a = -RG_C * r * jax.nn.softplus(-lam.astype(f32))
    a = jnp.exp(log_a)
    mult = jnp.sqrt(-jnp.expm1(2.0 * log_a))
    if reset:
        mult = mult.at[:, L - 1 if reverse else 0].set(1.0)
    A, Bc = lax.associative_scan(_lin_combine, (a, mult * (i * x)), axis=1, reverse=reverse)
    h = Bc + A * h0[:, None, :]
    final = h[:, 0] if reverse else h[:, -1]
    return h, final


def trunk_layer(x, mod, rg_h0, reset, lp):
    (n1, n2, w_in, hcw, hcb, fw1, fb1, fw2, fb2, fw3, ffreq, hbias, w_a, w_b,
     rcw, rcb, wr, br, wi, bi, lam, w_c, w_o, w_gu, w_down) = lp
    dt = x.dtype
    L = x.shape[1]
    sh1, sc1, g1, sh2, sc2, g2 = jnp.split(mod[:, None, :].astype(dt), 6, axis=-1)
    h = rmsnorm(x, n1) * (1 + sc1) + sh1
    u = h @ w_in
    u_hy, u_fn, u_rx, u_ry, u_g = jnp.split(u, SPLITS, axis=-1)
    kf = hyena_filters(L, fw1, fb1, fw2, fb2, fw3, ffreq)
    y_a = hyena_branch(u_hy, hcw, hcb, kf, hbias).astype(dt) @ w_a
    y_b = fnet_branch(u_fn).astype(dt) @ w_b
    xr = depthwise_conv(u_rx, rcw, rcb, RG_CONV // 2, RG_CONV - 1 - RG_CONV // 2).astype(jnp.float32)
    h_f, s_f = rglru_scan(xr, wr[0], br[0], wi[0], bi[0], lam[0], rg_h0[:, 0], False, reset)
    h_b, s_b = rglru_scan(xr, wr[1], br[1], wi[1], bi[1], lam[1], rg_h0[:, 1], True, reset)
    y_c = ((h_f + h_b).astype(dt) * jax.nn.gelu(u_ry)) @ w_c
    ga, gb, gc = jnp.split(jax.nn.sigmoid(u_g), 3, axis=-1)
    x = x + g1 * ((ga * y_a + gb * y_b + gc * y_c) @ w_o)
    h = rmsnorm(x, n2) * (1 + sc2) + sh2
    gt, up = jnp.split(h @ w_gu, 2, axis=-1)
    x = x + g2 * ((jax.nn.silu(gt) * up) @ w_down)
    return x, jnp.stack([s_f, s_b], axis=1)


def setup_inputs(seed: int = 0) -> dict:
    key = jax.random.key(seed)
    ks = iter(jax.random.split(key, 48))

    def nrm(shape, scale):
        return jax.random.normal(next(ks), shape, jnp.float32) * scale

    D = D_MODEL
    lam_u = jax.random.uniform(next(ks), (DEPTH, 2, D_RG), jnp.float32, 0.9, 0.999)
    return {
        'x_prompt': nrm((BATCH, SEQ, D), 1.0),
        'x_sample': nrm((DEC_BATCH, DEC_SEQ, D), 1.0),
        'c': nrm((DEC_BATCH, D), 1.0),
        'state_rglru': nrm((DEC_BATCH, DEPTH, 2, D_RG), 0.5),
        'c_ctx': nrm((D,), 1.0),
        'norm1_g': 1.0 + nrm((DEPTH, D), 0.02),
        'norm2_g': 1.0 + nrm((DEPTH, D), 0.02),
        'w_ada': nrm((DEPTH, D, 6 * D), 0.5 * D ** -0.5),
        'b_ada': nrm((DEPTH, 6 * D), 0.01),
        'w_in': nrm((DEPTH, D, D_IN), D ** -0.5),
        'hy_conv_w': nrm((DEPTH, HY_SHORT, 3 * D_HY), HY_SHORT ** -0.5),
        'hy_conv_b': nrm((DEPTH, 3 * D_HY), 0.01),
        'hy_f_w1': nrm((DEPTH, FILT_EMB, FILT_HID), FILT_EMB ** -0.5),
        'hy_f_b1': nrm((DEPTH, FILT_HID), 0.1),
        'hy_f_w2': nrm((DEPTH, FILT_HID, FILT_HID), FILT_HID ** -0.5),
        'hy_f_b2': nrm((DEPTH, FILT_HID), 0.1),
        'hy_f_w3': nrm((DEPTH, FILT_HID, 4 * D_HY), FILT_HID ** -0.5),
        'hy_f_freq': 1.0 + nrm((DEPTH, FILT_HID), 0.1),
        'hy_bias': nrm((DEPTH, 2, D_HY), 0.1),
        'w_a': nrm((DEPTH, D_HY, D), D_HY ** -0.5),
        'w_b': nrm((DEPTH, D_FN, D), D_FN ** -0.5),
        'rg_conv_w': nrm((DEPTH, RG_CONV, D_RG), RG_CONV ** -0.5),
        'rg_conv_b': nrm((DEPTH, D_RG), 0.01),
        'rg_wr': nrm((DEPTH, 2, N_RG_BLOCKS, RG_BLOCK, RG_BLOCK), RG_BLOCK ** -0.5),
        'rg_br': nrm((DEPTH, 2, D_RG), 0.01),
        'rg_wi': nrm((DEPTH, 2, N_RG_BLOCKS, RG_BLOCK, RG_BLOCK), RG_BLOCK ** -0.5),
        'rg_bi': nrm((DEPTH, 2, D_RG), 0.01),
        'rg_lam': jnp.log(lam_u) - jnp.log1p(-lam_u),
        'w_c': nrm((DEPTH, D_RG, D), D_RG ** -0.5),
        'w_o': nrm((DEPTH, D, D), D ** -0.5),
        'w_gu': nrm((DEPTH, D, 2 * D_FF), D ** -0.5),
        'w_down': nrm((DEPTH, D_FF, D), D_FF ** -0.5),
        'final_g': 1.0 + nrm((D,), 0.02),
    }


def reference(x_prompt, x_sample, c, state_rglru, c_ctx, norm1_g, norm2_g, w_ada, b_ada, w_in,
              hy_conv_w, hy_conv_b, hy_f_w1, hy_f_b1, hy_f_w2, hy_f_b2, hy_f_w3, hy_f_freq, hy_bias,
              w_a, w_b, rg_conv_w, rg_conv_b, rg_wr, rg_br, rg_wi, rg_bi, rg_lam, w_c, w_o,
              w_gu, w_down, final_g):
    xp = x_prompt
    xs = x_sample
    ctx_states = []
    for l in range(DEPTH):
        lp = (norm1_g[l], norm2_g[l], w_in[l], hy_conv_w[l], hy_conv_b[l], hy_f_w1[l], hy_f_b1[l],
              hy_f_w2[l], hy_f_b2[l], hy_f_w3[l], hy_f_freq[l], hy_bias[l], w_a[l], w_b[l],
              rg_conv_w[l], rg_conv_b[l], rg_wr[l], rg_br[l], rg_wi[l], rg_bi[l], rg_lam[l],
              w_c[l], w_o[l], w_gu[l], w_down[l])
        mod_ctx = jax.nn.silu(c_ctx[None, :]) @ w_ada[l] + b_ada[l]
        mod_lat = jax.nn.silu(c) @ w_ada[l] + b_ada[l]
        xp, st = trunk_layer(xp, mod_ctx, jnp.zeros((xp.shape[0], 2, D_RG), jnp.float32), True, lp)
        ctx_states.append(st)
        xs, _ = trunk_layer(xs, mod_lat, state_rglru[:, l].astype(jnp.float32), False, lp)
    y_prompt = rmsnorm(xp, final_g)
    y_sample = rmsnorm(xs, final_g)
    new_state_rglru = jnp.stack(ctx_states, axis=1).astype(x_prompt.dtype)
    return (y_prompt, y_sample, new_state_rglru)
```

```python
import functools
import math

import numpy as np
import jax
import jax.numpy as jnp
from jax import lax
from jax.experimental import pallas as pl
from jax.experimental.pallas import tpu as pltpu

F32 = jnp.float32
BF16 = jnp.bfloat16
HIGHEST = lax.Precision.HIGHEST

D_MODEL = 1024
BATCH = 16
SEQ = 256
DEPTH = 4
DEC_BATCH = 4
DEC_SEQ = 1024
EPS = 1e-6
D_HY = 512
N_BANDS = 8
FILT_EMB = 1 + 2 * N_BANDS
FILT_HID = 64
DECAY_SLOW = -math.log(1e-2) / 1.5
DECAY_FAST = -math.log(1e-2) / 0.3
D_FN = 512
FN_GROUP = 128
N_FN_GROUPS = D_FN // FN_GROUP
D_RG = 1024
RG_BLOCK = 128
RG_C = 8.0
D_FF = -(-8 * D_MODEL // (3 * 256)) * 256
D_IN = 3 * D_HY + D_FN + 2 * D_RG + 3 * D_MODEL
COL_FN = 3 * D_HY
COL_RX = COL_FN + D_FN
COL_RY = COL_RX + D_RG
COL_G = COL_RY + D_RG

CHUNK = 1024
N_CTX_TOK = BATCH * SEQ
N_LAT_TOK = DEC_BATCH * DEC_SEQ
N_TOK = N_CTX_TOK + N_LAT_TOK
NQ_CTX = N_CTX_TOK // CHUNK
NQ = N_TOK // CHUNK
CTX_PER_CHUNK = CHUNK // SEQ
MOD_ROWS = 8
FEAT_PAD = 128
HY_TC = 256
RG_TC = 512
VMEM_LIMIT = 56 * 1024 * 1024

assert DEC_SEQ == CHUNK and CHUNK % SEQ == 0 and N_CTX_TOK % CHUNK == 0
assert 1 + DEC_BATCH <= MOD_ROWS


def _params(*sem):
    return pltpu.CompilerParams(dimension_semantics=sem, vmem_limit_bytes=VMEM_LIMIT)


def _mod_row(q):
    return jnp.maximum(q - (NQ_CTX - 1), 0)


def _rmsnorm(x, g):
    return x * lax.rsqrt(jnp.mean(x * x, axis=-1, keepdims=True) + EPS) * g


def _sigmoid(x):
    return 0.5 * jnp.tanh(0.5 * x) + 0.5


def _gelu_tanh(x):
    return x * (0.5 * (1.0 + jnp.tanh(math.sqrt(2.0 / math.pi) * (x + 0.044715 * (x * x * x)))))


def _shift_rows(x, k, seq_len):
    n = x.shape[0]
    rolled = pltpu.roll(x, k % n, axis=0)
    t = lax.broadcasted_iota(jnp.int32, (n, 1), 0) & (seq_len - 1)
    valid = (t >= k) if k > 0 else (t < seq_len + k)
    return jnp.where(valid, rolled, 0.0)


def _as_bf16(*tables):
    return tuple(jnp.asarray(t, dtype=F32).astype(BF16) for t in tables)


def _angle_table(n_rows, n_cols, period):
    prod = np.outer(np.arange(n_rows, dtype=np.int64), np.arange(n_cols, dtype=np.int64)) % period
    return 2.0 * np.pi * prod.astype(np.float64) / period


@functools.lru_cache(maxsize=None)
def _hyena_dft_tables(L):
    n = 2 * L
    ang = _angle_table(L, L, n)
    alt = np.where(np.arange(L) % 2 == 0, 1.0, -1.0)
    cos_f, sin_f = np.cos(ang), np.sin(ang)
    sin_f[0, :] = alt
    fwd = np.concatenate([cos_f, sin_f], axis=0)
    cos_i, sin_i = 2.0 / n * np.cos(ang), 2.0 / n * np.sin(ang)
    cos_i[:, 0] = 1.0 / n
    sin_i[:, 0] = alt / n
    inv = np.concatenate([cos_i, sin_i], axis=1)
    return fwd.astype(np.float32), inv.astype(np.float32)


@functools.lru_cache(maxsize=None)
def _fnet_tables(L):
    ang = _angle_table(L, L, L)
    seq = np.concatenate([np.cos(ang), -np.sin(ang)], axis=1)
    ang_c = _angle_table(FN_GROUP, FN_GROUP, FN_GROUP)
    chan = np.concatenate([np.cos(ang_c), np.sin(ang_c)], axis=1)
    return seq.astype(np.float32), chan.astype(np.float32)


@functools.lru_cache(maxsize=None)
def _filter_tables(L):
    t = np.arange(L, dtype=np.float32) / np.float32(L)
    ang = 2.0 * np.pi * t[:, None].astype(np.float64) * np.arange(1, N_BANDS + 1, dtype=np.float64)
    feats = np.zeros((L, FEAT_PAD), np.float64)
    feats[:, 0] = t
    feats[:, 1:1 + N_BANDS] = np.sin(ang)
    feats[:, 1 + N_BANDS:FILT_EMB] = np.cos(ang)
    deltas = np.linspace(DECAY_SLOW, DECAY_FAST, D_HY, dtype=np.float32).astype(np.float64)
    decay = np.exp(-t[:, None].astype(np.float64) * deltas)
    return jnp.asarray(feats, dtype=F32), jnp.asarray(decay, dtype=F32)


def _mod_kernel(c_ref, w_ref, b_ref, o_ref):
    cv = c_ref[...]
    act = cv * _sigmoid(cv)
    o_ref[...] = jnp.dot(act, w_ref[...], precision=HIGHEST, preferred_element_type=F32) + b_ref[...]


def _modulation(cvec, w_ada, b_ada):
    tn = 1536
    return pl.pallas_call(
        _mod_kernel,
        grid=(DEPTH, 6 * D_MODEL // tn),
        in_specs=[
            pl.BlockSpec((MOD_ROWS, D_MODEL), lambda l, j: (0, 0)),
            pl.BlockSpec((None, D_MODEL, tn), lambda l, j: (l, 0, j)),
            pl.BlockSpec((None, 1, tn), lambda l, j: (l, 0, j)),
        ],
        out_specs=pl.BlockSpec((None, MOD_ROWS, tn), lambda l, j: (l, 0, j)),
        out_shape=jax.ShapeDtypeStruct((DEPTH, MOD_ROWS, 6 * D_MODEL), F32),
        compiler_params=_params("arbitrary", "arbitrary"),
        name="adaln_modulation",
    )(cvec, w_ada, b_ada.reshape(DEPTH, 1, 6 * D_MODEL))


def _filter_kernel(feats_ref, decay_ref, wf_ref, w1_ref, b1_ref, w2_ref, b2_ref, w3_ref, fq_ref,
                   p_ref, *, L):
    freq = fq_ref[...]
    h = jnp.sin(freq * (jnp.dot(feats_ref[...], w1_ref[...], precision=HIGHEST,
                                preferred_element_type=F32) + b1_ref[...]))
    h = jnp.sin(freq * (jnp.dot(h, w2_ref[...], precision=HIGHEST,
                                preferred_element_type=F32) + b2_ref[...]))
    h = jnp.dot(h, w3_ref[...], precision=HIGHEST, preferred_element_type=F32)
    decay = decay_ref[...]
    row = lax.broadcasted_iota(jnp.int32, (L, 1), 0)
    h_fwd = h[:, :D_HY] * decay
    h_bwd = jnp.where(row == 0, 0.0, h[:, D_HY:] * decay)
    ssq = jnp.sum(h_fwd * h_fwd, axis=0, keepdims=True) + jnp.sum(h_bwd * h_bwd, axis=0, keepdims=True)
    scale = lax.rsqrt(ssq + EPS)
    even = (h_fwd + h_bwd) * scale
    odd = (h_bwd - h_fwd) * scale
    k_re = jnp.dot(wf_ref[0:L, :], even.astype(BF16), preferred_element_type=F32)
    k_im = jnp.dot(wf_ref[L:2 * L, :], odd.astype(BF16), preferred_element_type=F32)
    alt = jnp.where((row & 1) == 0, 1.0, -1.0)
    k_nyq = jnp.sum(even * alt, axis=0, keepdims=True)
    p_ref[0] = k_re
    p_ref[1] = jnp.where(row == 0, 0.0, k_im)
    p_ref[2] = jnp.where(row == 0, k_nyq, k_re)


def _hyena_filters(L, wf, fw1, fb1, fw2, fb2, fw3, ffreq):
    feats, decay = _filter_tables(L)
    w1 =jnp.pad(fw1, ((0, 0), (0, FEAT_PAD - FILT_EMB), (0, 0)))
    const = lambda shape: pl.BlockSpec(shape, lambda l, o: (0,) * len(shape))
    per_layer = lambda *shape: pl.BlockSpec((None,) + shape, lambda l, o: (l,) + (0,) * len(shape))
    return pl.pallas_call(
        functools.partial(_filter_kernel, L=L),
        grid=(DEPTH, 2),
        in_specs=[
            const((L, FEAT_PAD)), const((L, D_HY)), const((2 * L, L)),
            per_layer(FEAT_PAD, FILT_HID), per_layer(1, FILT_HID),
            per_layer(FILT_HID, FILT_HID), per_layer(1, FILT_HID),
            pl.BlockSpec((None, FILT_HID, 2 * D_HY), lambda l, o: (l, 0, o)),
            per_layer(1, FILT_HID),
        ],
        out_specs=pl.BlockSpec((None, None, 3, L, D_HY), lambda l, o: (l, o, 0, 0, 0)),
        out_shape=jax.ShapeDtypeStruct((DEPTH, 2, 3, L, D_HY), F32),
        compiler_params=_params("arbitrary", "arbitrary"),
        name=f"hyena_filters_{L}",
    )(feats, decay, wf, w1, fb1.reshape(DEPTH, 1, FILT_HID), fw2, fb2.reshape(DEPTH, 1, FILT_HID),
      fw3, ffreq.reshape(DEPTH, 1, FILT_HID))


def _inproj_kernel(x_ref, mod_ref, g_ref, w_ref, u_ref, h_scr, *, tm):
    i = pl.program_id(0)

    @pl.when(pl.program_id(1) == 0)
    def _():
        for s in range(tm // CHUNK):
            r = _mod_row(i * (tm // CHUNK) + s)
            shift = mod_ref[pl.ds(r, 1), 0:D_MODEL]
            scale = mod_ref[pl.ds(r, 1), D_MODEL:2 * D_MODEL]
            rows = slice(s * CHUNK, (s + 1) * CHUNK)
            h = _rmsnorm(x_ref[rows, :], g_ref[...]) * (1.0 + scale) + shift
            h_scr[rows, :] = h.astype(BF16)

    u_ref[...] = jnp.dot(h_scr[...], w_ref[...].astype(BF16), preferred_element_type=F32)


def _input_projection(x, mod, norm_g, w_in, l, tm=1024, tn=512):
    return pl.pallas_call(
        functools.partial(_inproj_kernel, tm=tm),
        grid=(N_TOK // tm, D_IN // tn),
        in_specs=[
            pl.BlockSpec((tm, D_MODEL), lambda i, j: (i, 0)),
            pl.BlockSpec((None, MOD_ROWS, 6 * D_MODEL), lambda i, j: (l, 0, 0)),
            pl.BlockSpec((None, 1, D_MODEL), lambda i, j: (l, 0, 0)),
            pl.BlockSpec((None, D_MODEL, tn), lambda i, j: (l, 0, j)),
        ],
        out_specs=pl.BlockSpec((tm, tn), lambda i, j: (i, j)),
        out_shape=jax.ShapeDtypeStruct((N_TOK, D_IN), F32),
        scratch_shapes=[pltpu.VMEM((tm, D_MODEL), BF16)],
        compiler_params=_params("arbitrary", "arbitrary"),
        name="input_projection",
    )(x, mod, norm_g, w_in)


def _fftconv(z, p_ref, order, wf_ref, wi_ref, L):
    spec = jnp.dot(wf_ref[...], z.astype(BF16), preferred_element_type=F32)
    z_re, z_sn = spec[:L], spec[L:]
    p_a, p_b, p_c = p_ref[order, 0], p_ref[order, 1], p_ref[order, 2]
    y_re = z_re * p_a + z_sn * p_b
    y_sn = z_sn * p_c - z_re * p_b
    y = jnp.concatenate([y_re, y_sn], axis=0).astype(BF16)
    return jnp.dot(wi_ref[...], y, preferred_element_type=F32)


def _hyena_kernel(v_ref, x1_ref, x2_ref, wv_ref, w1_ref, w2_ref, bv_ref, b1_ref, b2_ref, hb_ref,
                  ps_ref, pl_ref, wfs_ref, wis_ref, wfl_ref, wil_ref, o_ref):
    q = pl.program_id(1)

    def run(L, p_ref, wf_ref, wi_ref):
        def short_conv(u_ref, w_ref, b_ref):
            u = u_ref[...]
            w = w_ref[...]
            return (b_ref[...] + _shift_rows(u, 1, L) * w[0:1] + u * w[1:2]
                    + _shift_rows(u, -1, L) * w[2:3])

        v = short_conv(v_ref, wv_ref, bv_ref)
        x1 = short_conv(x1_ref, w1_ref, b1_ref)
        x2 = short_conv(x2_ref, w2_ref, b2_ref)
        hb = hb_ref[...]
        for s in range(CHUNK // L):
            rows = slice(s * L, (s + 1) * L)
            vs = v[rows]
            z = x1[rows] * (_fftconv(vs, p_ref, 0, wf_ref, wi_ref, L) + hb[0:1] * vs)
            y = x2[rows] * (_fftconv(z, p_ref, 1, wf_ref, wi_ref, L) + hb[1:2] * z)
            o_ref[rows, :] = y.astype(BF16)

    @pl.when(q < NQ_CTX)
    def _():
        run(SEQ, ps_ref, wfs_ref, wis_ref)

    @pl.when(q >= NQ_CTX)
    def _():
        run(DEC_SEQ, pl_ref, wfl_ref, wil_ref)


def _hyena(u, conv_w, conv_b, hbias, p_short, p_long, dft_short, dft_long, l):
    tc = HY_TC
    nc = D_HY // tc
    wfs, wis = dft_short
    wfl, wil = dft_long
    part = lambda k: pl.BlockSpec((CHUNK, tc), lambda c, q: (q, k * nc + c))
    wpart = lambda k: pl.BlockSpec((None, 3, tc), lambda c, q: (l, 0, k * nc + c))
    bpart = lambda k: pl.BlockSpec((None, 1, tc), lambda c, q: (l, 0, k * nc + c))
    const = lambda shape: pl.BlockSpec(shape, lambda c, q: (0, 0))
    planes = lambda L: pl.BlockSpec((None, 2, 3, L, tc), lambda c, q: (l, 0, 0, 0, c))
    conv_b = conv_b.reshape(DEPTH, 1, 3 * D_HY)
    return pl.pallas_call(
        _hyena_kernel,
        grid=(nc, NQ),
        in_specs=[
            part(0), part(1), part(2), wpart(0), wpart(1), wpart(2), bpart(0), bpart(1), bpart(2),
            pl.BlockSpec((None, 2, tc), lambda c, q: (l, 0, c)),
            planes(SEQ), planes(DEC_SEQ),
            const((2 * SEQ, SEQ)), const((SEQ, 2 * SEQ)),
            const((2 * DEC_SEQ, DEC_SEQ)), const((DEC_SEQ, 2 * DEC_SEQ)),
        ],
        out_specs=pl.BlockSpec((CHUNK, tc), lambda c, q: (q, c)),
        out_shape=jax.ShapeDtypeStruct((N_TOK, D_HY), BF16),
        compiler_params=_params("arbitrary", "arbitrary"),
        name="hyena_mixer",
    )(u, u, u, conv_w, conv_w, conv_w, conv_b, conv_b, conv_b, hbias, p_short, p_long,
      wfs, wis, wfl, wil)


def _fnet_kernel(u_ref, cs_ref, fs_ref, fl_ref, o_ref):
    q = pl.program_id(0)

    def run(L, f_ref):
        x = u_ref[...].astype(BF16)
        parts_c, parts_s = [], []
        for g in range(N_FN_GROUPS):
            r = jnp.dot(x[:, g * FN_GROUP:(g + 1) * FN_GROUP], cs_ref[...], preferred_element_type=F32)
            parts_c.append(r[:, :FN_GROUP])
            parts_s.append(r[:, FN_GROUP:])
        xc = jnp.concatenate(parts_c, axis=1).astype(BF16)
        xs = jnp.concatenate(parts_s, axis=1).astype(BF16)
        norm = 1.0 / math.sqrt(L * FN_GROUP)
        for s in range(CHUNK // L):
            rows = slice(s * L, (s + 1) * L)
            stacked = jnp.concatenate([xc[rows], xs[rows]], axis=0)
            y = jnp.dot(f_ref[...], stacked, preferred_element_type=F32) * norm
            o_ref[rows, :] = y.astype(BF16)

    @pl.when(q < NQ_CTX)
    def _():
        run(SEQ, fs_ref)

    @pl.when(q >= NQ_CTX)
    def _():
        run(DEC_SEQ, fl_ref)


def _fnet(u, cs, fs, fl):
    const = lambda shape: pl.BlockSpec(shape, lambda q: (0, 0))
    return pl.pallas_call(
        _fnet_kernel,
        grid=(NQ,),
        in_specs=[
            pl.BlockSpec((CHUNK, D_FN), lambda q: (q, COL_FN // D_FN)),
            const((FN_GROUP, 2 * FN_GROUP)), const((SEQ, 2 * SEQ)), const((DEC_SEQ, 2 * DEC_SEQ)),
        ],
        out_specs=pl.BlockSpec((CHUNK, D_FN), lambda q: (q, 0)),
        out_shape=jax.ShapeDtypeStruct((N_TOK, D_FN), BF16),
        compiler_params=_params("arbitrary"),
        name="fnet_mixer",
    )(u, cs, fs, fl)


def _rglru_kernel(ux_ref, uy_ref, h0_ref, cw_ref, cb_ref, wg_ref, br_ref, bi_ref, lam_ref,
                  y_ref, st_ref, a_f, b_f, a_b, b_b):
    q = pl.program_id(0)

    def run(L, reset):
        nseq = CHUNK // L
        u = ux_ref[...]
        w = cw_ref[...]
        xr = (cb_ref[...] + _shift_rows(u, 2, L) * w[0:1] + _shift_rows(u, 1, L) * w[1:2]
              + u * w[2:3] + _shift_rows(u, -1, L) * w[3:4])
        xb = xr.astype(BF16)
        tpos = lax.broadcasted_iota(jnp.int32, (CHUNK, 1), 0) & (L - 1)
        neg_lam = -lam_ref[...]
        softplus = jnp.maximum(neg_lam, 0.0) + jnp.log1p(jnp.exp(-jnp.abs(neg_lam)))
        neg_c_softplus = -RG_C * softplus
        for n in range(RG_TC // RG_BLOCK):
            cs = slice(n * RG_BLOCK, (n + 1) * RG_BLOCK)
            gates = jnp.dot(xb[:, cs], wg_ref[n].astype(BF16), preferred_element_type=F32)
            for d, (a_scr, b_scr) in enumerate(((a_f, b_f), (a_b, b_b))):
                c0 = 2 * d * RG_BLOCK
                r = _sigmoid(gates[:, c0:c0 + RG_BLOCK] + br_ref[d:d + 1, cs])
                ig = _sigmoid(gates[:, c0 + RG_BLOCK:c0 + 2 * RG_BLOCK] + bi_ref[d:d + 1, cs])
                log_a = r * neg_c_softplus[d:d + 1, cs]
                th = jnp.tanh(log_a)
                mult = jnp.sqrt(-2.0 * th / (1.0 - th))
                if reset:
                    mult = jnp.where(tpos == (0 if d == 0 else L - 1), 1.0, mult)
                a_scr[:, cs] = jnp.exp(log_a)
                b_scr[:, cs] = mult * (ig * xr[:, cs])

        def step(t, carry):
            h_f, h_b = carry
            new_f, new_b = [], []
            for s in range(nseq):
                rf = s * L + t
                rb = s * L + (L - 1) - t
                hf = a_f[pl.ds(rf, 1), :] * h_f[s] + b_f[pl.ds(rf, 1), :]
                b_f[pl.ds(rf, 1), :] = hf
                hb = a_b[pl.ds(rb, 1), :] * h_b[s] + b_b[pl.ds(rb, 1), :]
                b_b[pl.ds(rb, 1), :] = hb
                new_f.append(hf)
                new_b.append(hb)
            return tuple(new_f), tuple(new_b)

        init = (tuple(h0_ref[s, 0:1, :] for s in range(nseq)),
                tuple(h0_ref[s, 1:2, :] for s in range(nseq)))
        fin_f, fin_b = lax.fori_loop(0, L, step, init, unroll=8)
        y_ref[...] = ((b_f[...] + b_b[...]) * _gelu_tanh(uy_ref[...])).astype(BF16)
        st_ref[...] = jnp.zeros(st_ref.shape, F32)
        for s in range(nseq):
            st_ref[s, 0:1, :] = fin_f[s]
            st_ref[s, 1:2, :] = fin_b[s]

    @pl.when(q < NQ_CTX)
    def _():
        run(SEQ, True)

    @pl.when(q >= NQ_CTX)
    def _():
        run(DEC_SEQ, False)


def _rglru(u, h0, conv_w, conv_b, w_gates, b_r, b_i, lam, l):
    tc = RG_TC
    per_layer = lambda rows: pl.BlockSpec((None, rows, tc), lambda q, c: (l, 0, c))
    state_spec = pl.BlockSpec((None, CTX_PER_CHUNK, 2, tc), lambda q, c: (q, 0, 0, c))
    return pl.pallas_call(
        _rglru_kernel,
        grid=(NQ, D_RG // tc),
        in_specs=[
            pl.BlockSpec((CHUNK, tc), lambda q, c: (q, COL_RX // tc + c)),
            pl.BlockSpec((CHUNK, tc), lambda q, c: (q, COL_RY // tc + c)),
            state_spec,
            per_layer(4), per_layer(1),
            pl.BlockSpec((None, tc // RG_BLOCK, RG_BLOCK, 4 * RG_BLOCK), lambda q, c: (l, c, 0, 0)),
            per_layer(2), per_layer(2), per_layer(2),
        ],
        out_specs=[pl.BlockSpec((CHUNK, tc), lambda q, c: (q, c)), state_spec],
        out_shape=[jax.ShapeDtypeStruct((N_TOK, D_RG), BF16),
                   jax.ShapeDtypeStruct((NQ, CTX_PER_CHUNK, 2, D_RG), F32)],
        scratch_shapes=[pltpu.VMEM((CHUNK, tc), F32)] * 4,
        compiler_params=_params("arbitrary", "arbitrary"),
        name="rglru_mixer",
    )(u, u, h0, conv_w, conv_b.reshape(DEPTH, 1, D_RG), w_gates, b_r, b_i, lam)


def _merge_kernel(x_ref, ya_ref, yb_ref, yc_ref, ga_ref, gb_ref, gc_ref, mod_ref,
                  wa_ref, wb_ref, wc_ref, wo_ref, o_ref, *, tm):
    r = _mod_row(pl.program_id(0) // (CHUNK // tm))
    gate = mod_ref[pl.ds(r, 1), 2 * D_MODEL:3 * D_MODEL]
    ya = jnp.dot(ya_ref[...], wa_ref[...], preferred_element_type=F32)
    yb = jnp.dot(yb_ref[...], wb_ref[...], preferred_element_type=F32)
    yc = jnp.dot(yc_ref[...], wc_ref[...], preferred_element_type=F32)
    mix = _sigmoid(ga_ref[...]) * ya + _sigmoid(gb_ref[...]) * yb + _sigmoid(gc_ref[...]) * yc
    out = jnp.dot(mix.astype(BF16), wo_ref[...], preferred_element_type=F32)
    o_ref[...] = x_ref[...] + gate * out


def _merge(x, y_hy, y_fn, y_rg, u, mod, w_a, w_b, w_c, w_o, l, tm=512):
    rows = lambda width, col=0: pl.BlockSpec((tm, width), lambda i: (i, col))
    weight = lambda k: pl.BlockSpec((None, k, D_MODEL), lambda i: (l, 0, 0))
    g0 = COL_G // D_MODEL
    return pl.pallas_call(
        functools.partial(_merge_kernel, tm=tm),
        grid=(N_TOK // tm,),
        in_specs=[
            rows(D_MODEL), rows(D_HY), rows(D_FN), rows(D_RG),
            rows(D_MODEL, g0), rows(D_MODEL, g0 + 1), rows(D_MODEL, g0 + 2),
            pl.BlockSpec((None, MOD_ROWS, 6 * D_MODEL), lambda i: (l, 0, 0)),
            weight(D_HY), weight(D_FN), weight(D_RG), weight(D_MODEL),
        ],
        out_specs=rows(D_MODEL),
        out_shape=jax.ShapeDtypeStruct((N_TOK, D_MODEL), F32),
        compiler_params=_params("arbitrary"),
        name="branch_merge",
    )(x, y_hy, y_fn, y_rg, u, u, u, mod, w_a, w_b, w_c, w_o)


def _ffn_kernel(x_ref, mod_ref, g_ref, wg_ref, wu_ref, wd_ref, fg_ref, o_ref, h_scr, acc, *,
                tm, final_norm):
    i = pl.program_id(0)
    f = pl.program_id(1)
    sub = tm // CHUNK

    @pl.when(f == 0)
    def _():
        for s in range(sub):
            r = _mod_row(i * sub + s)
            shift = mod_ref[pl.ds(r, 1), 3 * D_MODEL:4 * D_MODEL]
            scale = mod_ref[pl.ds(r, 1), 4 * D_MODEL:5 * D_MODEL]
            rows = slice(s * CHUNK, (s + 1) * CHUNK)
            h = _rmsnorm(x_ref[rows, :], g_ref[...]) * (1.0 + scale) + shift
            h_scr[rows, :] = h.astype(BF16)
        acc[...] = jnp.zeros(acc.shape, F32)

    h = h_scr[...]
    gt = jnp.dot(h, wg_ref[...].astype(BF16), preferred_element_type=F32)
    up = jnp.dot(h, wu_ref[...].astype(BF16), preferred_element_type=F32)
    act = (gt * _sigmoid(gt)) * up
    acc[...] += jnp.dot(act.astype(BF16), wd_ref[...].astype(BF16), preferred_element_type=F32)

    @pl.when(f == pl.num_programs(1) - 1)
    def _():
        for s in range(sub):
            r = _mod_row(i * sub + s)
            gate = mod_ref[pl.ds(r, 1), 5 * D_MODEL:6 * D_MODEL]
            rows = slice(s * CHUNK, (s + 1) * CHUNK)
            out = x_ref[rows, :] + gate * acc[rows, :]
            if final_norm:
                out = _rmsnorm(out, fg_ref[...])
            o_ref[rows, :] = out


def _ffn(x, mod, norm_g, w_gu, w_down, final_g, l, final_norm, tm=1024, tf=256):
    nf = D_FF // tf
    return pl.pallas_call(
        functools.partial(_ffn_kernel, tm=tm, final_norm=final_norm),
        grid=(N_TOK // tm, nf),
        in_specs=[
            pl.BlockSpec((tm, D_MODEL), lambda i, f: (i, 0)),
            pl.BlockSpec((None, MOD_ROWS, 6 * D_MODEL), lambda i, f: (l, 0, 0)),
            pl.BlockSpec((None, 1, D_MODEL), lambda i, f: (l, 0, 0)),
            pl.BlockSpec((None, D_MODEL, tf), lambda i, f: (l, 0, f)),
            pl.BlockSpec((None, D_MODEL, tf), lambda i, f: (l, 0, nf + f)),
            pl.BlockSpec((None, tf, D_MODEL), lambda i, f: (l, f, 0)),
            pl.BlockSpec((1, D_MODEL), lambda i, f: (0, 0)),
        ],
        out_specs=pl.BlockSpec((tm, D_MODEL), lambda i, f: (i, 0)),
        out_shape=jax.ShapeDtypeStruct((N_TOK, D_MODEL), F32),
        scratch_shapes=[pltpu.VMEM((tm, D_MODEL), BF16), pltpu.VMEM((tm, D_MODEL), F32)],
        compiler_params=_params("arbitrary", "arbitrary"),
        name="swiglu_ffn",
    )(x, mod, norm_g, w_gu, w_gu, w_down, final_g)


def kernel(x_prompt, x_sample, c, state_rglru, c_ctx, norm1_g, norm2_g, w_ada, b_ada, w_in,
           hy_conv_w, hy_conv_b, hy_f_w1, hy_f_b1, hy_f_w2, hy_f_b2, hy_f_w3, hy_f_freq, hy_bias,
           w_a, w_b, rg_conv_w, rg_conv_b, rg_wr, rg_br, rg_wi, rg_bi, rg_lam, w_c, w_o,
           w_gu, w_down, final_g):
    x = jnp.concatenate([x_prompt.reshape(N_CTX_TOK, D_MODEL), x_sample.reshape(N_LAT_TOK, D_MODEL)])
    cvec = jnp.concatenate([c_ctx[None, :], c, jnp.zeros((MOD_ROWS - 1 - DEC_BATCH, D_MODEL), F32)])
    mod = _modulation(cvec, w_ada, b_ada)

    dft_short = _as_bf16(*_hyena_dft_tables(SEQ))
    dft_long = _as_bf16(*_hyena_dft_tables(DEC_SEQ))
    fnet_short, fnet_chan = _as_bf16(*_fnet_tables(SEQ))
    fnet_long, = _as_bf16(_fnet_tables(DEC_SEQ)[0])
    filt = (hy_f_w1, hy_f_b1, hy_f_w2, hy_f_b2, hy_f_w3, hy_f_freq)
    p_short = _hyena_filters(SEQ, dft_short[0], *filt)
    p_long = _hyena_filters(DEC_SEQ, dft_long[0], *filt)

    w_gates = jnp.concatenate([rg_wr[:, 0], rg_wi[:, 0], rg_wr[:, 1], rg_wi[:, 1]], axis=-1)
    lat_h0 = jnp.pad(state_rglru.astype(F32).transpose(1, 0, 2, 3)[:, :, None],
                     ((0, 0), (0, 0), (0, CTX_PER_CHUNK - 1), (0, 0), (0, 0)))
    h0_all = jnp.concatenate([jnp.zeros((DEPTH, NQ_CTX, CTX_PER_CHUNK, 2, D_RG), F32), lat_h0], axis=1)
    w_a16, w_b16, w_c16, w_o16 = (w.astype(BF16) for w in (w_a, w_b, w_c, w_o))
    norm1 = norm1_g.reshape(DEPTH, 1, D_MODEL)
    norm2 = norm2_g.reshape(DEPTH, 1, D_MODEL)
    final = final_g.reshape(1, D_MODEL)

    states = []
    for l in range(DEPTH):
        u = _input_projection(x, mod, norm1, w_in, l)
        y_hy = _hyena(u, hy_conv_w, hy_conv_b, hy_bias, p_short, p_long, dft_short, dft_long, l)
        y_fn = _fnet(u, fnet_chan, fnet_short, fnet_long)
        y_rg, st = _rglru(u, h0_all[l], rg_conv_w, rg_conv_b, w_gates, rg_br, rg_bi, rg_lam, l)
        states.append(st[:NQ_CTX].reshape(BATCH, 2, D_RG))
        x = _merge(x, y_hy, y_fn, y_rg, u, mod, w_a16, w_b16, w_c16, w_o16, l)
        x = _ffn(x, mod, norm2, w_gu, w_down, final, l, final_norm=(l == DEPTH - 1))

    y_prompt = x[:N_CTX_TOK].reshape(BATCH, SEQ, D_MODEL)
    y_sample = x[N_CTX_TOK:].reshape(DEC_BATCH, DEC_SEQ, D_MODEL)
    new_state = jnp.stack(states, axis=1).astype(x_prompt.dtype)
    return (y_prompt, y_sample, new_state)
```

```python
import functools
import math

import numpy as np
import jax
import jax.numpy as jnp
from jax import lax
from jax.experimental import pallas as pl
from jax.experimental.pallas import tpu as pltpu

F32 = jnp.float32
BF16 = jnp.bfloat16
HIGHEST = lax.Precision.HIGHEST

D_MODEL = 1024
BATCH = 16
SEQ = 256
DEPTH = 4
DEC_BATCH = 4
DEC_SEQ = 1024
EPS = 1e-6
D_HY = 512
N_BANDS = 8
FILT_EMB = 1 + 2 * N_BANDS
FILT_HID = 64
DECAY_SLOW = -math.log(1e-2) / 1.5
DECAY_FAST = -math.log(1e-2) / 0.3
D_FN = 512
FN_GROUP = 128
N_FN_GROUPS = D_FN // FN_GROUP
D_RG = 1024
RG_BLOCK = 128
RG_C = 8.0
D_FF = -(-8 * D_MODEL // (3 * 256)) * 256
D_IN = 3 * D_HY + D_FN + 2 * D_RG + 3 * D_MODEL
COL_FN = 3 * D_HY
COL_RX = COL_FN + D_FN
COL_RY = COL_RX + D_RG
COL_G = COL_RY + D_RG

CHUNK = 1024
N_CTX_TOK = BATCH * SEQ
N_LAT_TOK = DEC_BATCH * DEC_SEQ
N_TOK = N_CTX_TOK + N_LAT_TOK
NQ_CTX = N_CTX_TOK // CHUNK
NQ = N_TOK // CHUNK
CTX_PER_CHUNK = CHUNK // SEQ
MOD_ROWS = 8
FEAT_PAD = 128
HY_TC = 256
RG_TC = 512
VMEM_LIMIT = 56 * 1024 * 1024

assert DEC_SEQ == CHUNK and CHUNK % SEQ == 0 and N_CTX_TOK % CHUNK == 0
assert 1 + DEC_BATCH <= MOD_ROWS


def _params(*sem):
    return pltpu.CompilerParams(dimension_semantics=sem, vmem_limit_bytes=VMEM_LIMIT)


def _mod_row(q):
    return jnp.maximum(q - (NQ_CTX - 1), 0)


def _rmsnorm(x, g):
    return x * lax.rsqrt(jnp.mean(x * x, axis=-1, keepdims=True) + EPS) * g


def _sigmoid(x):
    return 0.5 * jnp.tanh(0.5 * x) + 0.5


def _gelu_tanh(x):
    return x * (0.5 * (1.0 + jnp.tanh(math.sqrt(2.0 / math.pi) * (x + 0.044715 * (x * x * x)))))


def _shift_rows(x, k, seq_len):
    n = x.shape[0]
    rolled = pltpu.roll(x, k % n, axis=0)
    t = lax.broadcasted_iota(jnp.int32, (n, 1), 0) & (seq_len - 1)
    valid = (t >= k) if k > 0 else (t < seq_len + k)
    return jnp.where(valid, rolled, 0.0)


def _as_bf16(*tables):
    return tuple(jnp.asarray(t, dtype=F32).astype(BF16) for t in tables)


def _angle_table(n_rows, n_cols, period):
    prod = np.outer(np.arange(n_rows, dtype=np.int64), np.arange(n_cols, dtype=np.int64)) % period
    return 2.0 * np.pi * prod.astype(np.float64) / period


@functools.lru_cache(maxsize=None)
def _hyena_dft_tables(L):
    n = 2 * L
    ang = _angle_table(L, L, n)
    alt = np.where(np.arange(L) % 2 == 0, 1.0, -1.0)
    cos_f, sin_f = np.cos(ang), np.sin(ang)
    sin_f[0, :] = alt
    fwd = np.concatenate([cos_f, sin_f], axis=0)
    cos_i, sin_i = 2.0 / n * np.cos(ang), 2.0 / n * np.sin(ang)
    cos_i[:, 0] = 1.0 / n
    sin_i[:, 0] = alt / n
    inv = np.concatenate([cos_i, sin_i], axis=1)
    return fwd.astype(np.float32), inv.astype(np.float32)


@functools.lru_cache(maxsize=None)
def _fnet_tables(L):
    ang = _angle_table(L, L, L)
    seq = np.concatenate([np.cos(ang), -np.sin(ang)], axis=1)
    ang_c = _angle_table(FN_GROUP, FN_GROUP, FN_GROUP)
    chan = np.concatenate([np.cos(ang_c), np.sin(ang_c)], axis=1)
    return seq.astype(np.float32), chan.astype(np.float32)


@functools.lru_cache(maxsize=None)
def _filter_tables(L):
    t = np.arange(L, dtype=np.float32) / np.float32(L)
    ang = 2.0 * np.pi * t[:, None].astype(np.float64) * np.arange(1, N_BANDS + 1, dtype=np.float64)
    feats = np.zeros((L, FEAT_PAD), np.float64)
    feats[:, 0] = t
    feats[:, 1:1 + N_BANDS] = np.sin(ang)
    feats[:, 1 + N_BANDS:FILT_EMB] = np.cos(ang)
    deltas = np.linspace(DECAY_SLOW, DECAY_FAST, D_HY, dtype=np.float32).astype(np.float64)
    decay = np.exp(-t[:, None].astype(np.float64) * deltas)
    return jnp.asarray(feats, dtype=F32), jnp.asarray(decay, dtype=F32)


def _mod_kernel(c_ref, w_ref, b_ref, o_ref):
    cv = c_ref[...]
    act = cv * _sigmoid(cv)
    o_ref[...] = jnp.dot(act, w_ref[...], precision=HIGHEST, preferred_element_type=F32) + b_ref[...]


def _modulation(cvec, w_ada, b_ada):
    tn = 1536
    return pl.pallas_call(
        _mod_kernel,
        grid=(DEPTH, 6 * D_MODEL // tn),
        in_specs=[
            pl.BlockSpec((MOD_ROWS, D_MODEL), lambda l, j: (0, 0)),
            pl.BlockSpec((None, D_MODEL, tn), lambda l, j: (l, 0, j)),
            pl.BlockSpec((None, 1, tn), lambda l, j: (l, 0, j)),
        ],
        out_specs=pl.BlockSpec((None, MOD_ROWS, tn), lambda l, j: (l, 0, j)),
        out_shape=jax.ShapeDtypeStruct((DEPTH, MOD_ROWS, 6 * D_MODEL), F32),
        compiler_params=_params("arbitrary", "arbitrary"),
        name="adaln_modulation",
    )(cvec, w_ada, b_ada.reshape(DEPTH, 1, 6 * D_MODEL))


def _filter_kernel(feats_ref, decay_ref, wf_ref, w1_ref, b1_ref, w2_ref, b2_ref, w3_ref, fq_ref,
                   p_ref, *, L):
    freq = fq_ref[...]
    h = jnp.sin(freq * (jnp.dot(feats_ref[...], w1_ref[...], precision=HIGHEST,
                                preferred_element_type=F32) + b1_ref[...]))
    h = jnp.sin(freq * (jnp.dot(h, w2_ref[...], precision=HIGHEST,
                                preferred_element_type=F32) + b2_ref[...]))
    h = jnp.dot(h, w3_ref[...], precision=HIGHEST, preferred_element_type=F32)
    decay = decay_ref[...]
    row = lax.broadcasted_iota(jnp.int32, (L, 1), 0)
    h_fwd = h[:, :D_HY] * decay
    h_bwd = jnp.where(row == 0, 0.0, h[:, D_HY:] * decay)
    ssq = jnp.sum(h_fwd * h_fwd, axis=0, keepdims=True) + jnp.sum(h_bwd * h_bwd, axis=0, keepdims=True)
    scale = lax.rsqrt(ssq + EPS)
    even = (h_fwd + h_bwd) * scale
    odd = (h_bwd - h_fwd) * scale
    k_re = jnp.dot(wf_ref[0:L, :], even.astype(BF16), preferred_element_type=F32)
    k_im = jnp.dot(wf_ref[L:2 * L, :], odd.astype(BF16), preferred_element_type=F32)
    alt = jnp.where((row & 1) == 0, 1.0, -1.0)
    k_nyq = jnp.sum(even * alt, axis=0, keepdims=True)
    p_ref[0] = k_re
    p_ref[1] = jnp.where(row == 0, 0.0, k_im)
    p_ref[2] = jnp.where(row == 0, k_nyq, k_re)


def _hyena_filters(L, wf, fw1, fb1, fw2, fb2, fw3, ffreq):
    feats, decay = _filter_tables(L)
    w1 =jnp.pad(fw1, ((0, 0), (0, FEAT_PAD - FILT_EMB), (0, 0)))
    const = lambda shape: pl.BlockSpec(shape, lambda l, o: (0,) * len(shape))
    per_layer = lambda *shape: pl.BlockSpec((None,) + shape, lambda l, o: (l,) + (0,) * len(shape))
    return pl.pallas_call(
        functools.partial(_filter_kernel, L=L),
        grid=(DEPTH, 2),
        in_specs=[
            const((L, FEAT_PAD)), const((L, D_HY)), const((2 * L, L)),
            per_layer(FEAT_PAD, FILT_HID), per_layer(1, FILT_HID),
            per_layer(FILT_HID, FILT_HID), per_layer(1, FILT_HID),
            pl.BlockSpec((None, FILT_HID, 2 * D_HY), lambda l, o: (l, 0, o)),
            per_layer(1, FILT_HID),
        ],
        out_specs=pl.BlockSpec((None, None, 3, L, D_HY), lambda l, o: (l, o, 0, 0, 0)),
        out_shape=jax.ShapeDtypeStruct((DEPTH, 2, 3, L, D_HY), F32),
        compiler_params=_params("arbitrary", "arbitrary"),
        name=f"hyena_filters_{L}",
    )(feats, decay, wf, w1, fb1.reshape(DEPTH, 1, FILT_HID), fw2, fb2.reshape(DEPTH, 1, FILT_HID),
      fw3, ffreq.reshape(DEPTH, 1, FILT_HID))


def _inproj_kernel(x_ref, mod_ref, g_ref, w_ref, u_ref, h_scr, *, tm):
    i = pl.program_id(0)

    @pl.when(pl.program_id(1) == 0)
    def _():
        for s in range(tm // CHUNK):
            r = _mod_row(i * (tm // CHUNK) + s)
            shift = mod_ref[pl.ds(r, 1), 0:D_MODEL]
            scale = mod_ref[pl.ds(r, 1), D_MODEL:2 * D_MODEL]
            rows = slice(s * CHUNK, (s + 1) * CHUNK)
            h = _rmsnorm(x_ref[rows, :], g_ref[...]) * (1.0 + scale) + shift
            h_scr[rows, :] = h.astype(BF16)

    u_ref[...] = jnp.dot(h_scr[...], w_ref[...].astype(BF16), preferred_element_type=F32).astype(BF16)


def _input_projection(x, mod, norm_g, w_in, l, tm=2048, tn=512):
    return pl.pallas_call(
        functools.partial(_inproj_kernel, tm=tm),
        grid=(N_TOK // tm, D_IN // tn),
        in_specs=[
            pl.BlockSpec((tm, D_MODEL), lambda i, j: (i, 0)),
            pl.BlockSpec((None, MOD_ROWS, 6 * D_MODEL), lambda i, j: (l, 0, 0)),
            pl.BlockSpec((None, 1, D_MODEL), lambda i, j: (l, 0, 0)),
            pl.BlockSpec((None, D_MODEL, tn), lambda i, j: (l, 0, j)),
        ],
        out_specs=pl.BlockSpec((tm, tn), lambda i, j: (i, j)),
        out_shape=jax.ShapeDtypeStruct((N_TOK, D_IN), BF16),
        scratch_shapes=[pltpu.VMEM((tm, D_MODEL), BF16)],
        compiler_params=_params("arbitrary", "arbitrary"),
        name="input_projection",
    )(x, mod, norm_g, w_in)


def _fftconv(z, p_ref, order, wf_ref, wi_ref, L):
    spec = jnp.dot(wf_ref[...], z.astype(BF16), preferred_element_type=F32)
    z_re, z_sn = spec[:L], spec[L:]
    p_a, p_b, p_c = p_ref[order, 0], p_ref[order, 1], p_ref[order, 2]
    y_re = z_re * p_a + z_sn * p_b
    y_sn = z_sn * p_c - z_re * p_b
    y = jnp.concatenate([y_re, y_sn], axis=0).astype(BF16)
    return jnp.dot(wi_ref[...], y, preferred_element_type=F32)


def _hyena_kernel(v_ref, x1_ref, x2_ref, wv_ref, w1_ref, w2_ref, bv_ref, b1_ref, b2_ref, hb_ref,
                  ps_ref, pl_ref, wfs_ref, wis_ref, wfl_ref, wil_ref, o_ref):
    q = pl.program_id(1)

    def run(L, p_ref, wf_ref, wi_ref):
        def short_conv(u_ref, w_ref, b_ref):
            u = u_ref[...].astype(F32)
            w = w_ref[...]
            return (b_ref[...] + _shift_rows(u, 1, L) * w[0:1] + u * w[1:2]
                    + _shift_rows(u, -1, L) * w[2:3])

        v = short_conv(v_ref, wv_ref, bv_ref)
        x1 = short_conv(x1_ref, w1_ref, b1_ref)
        x2 = short_conv(x2_ref, w2_ref, b2_ref)
        hb = hb_ref[...]
        for s in range(CHUNK // L):
            rows = slice(s * L, (s + 1) * L)
            vs = v[rows]
            z = x1[rows] * (_fftconv(vs, p_ref, 0, wf_ref, wi_ref, L) + hb[0:1] * vs)
            y = x2[rows] * (_fftconv(z, p_ref, 1, wf_ref, wi_ref, L) + hb[1:2] * z)
            o_ref[rows, :] = y.astype(BF16)

    @pl.when(q < NQ_CTX)
    def _():
        run(SEQ, ps_ref, wfs_ref, wis_ref)

    @pl.when(q >= NQ_CTX)
    def _():
        run(DEC_SEQ, pl_ref, wfl_ref, wil_ref)


def _hyena(u, conv_w, conv_b, hbias, p_short, p_long, dft_short, dft_long, l):
    tc = HY_TC
    nc = D_HY // tc
    wfs, wis = dft_short
    wfl, wil = dft_long
    part = lambda k: pl.BlockSpec((CHUNK, tc), lambda c, q: (q, k * nc + c))
    wpart = lambda k: pl.BlockSpec((None, 3, tc), lambda c, q: (l, 0, k * nc + c))
    bpart = lambda k: pl.BlockSpec((None, 1, tc), lambda c, q: (l, 0, k * nc + c))
    const = lambda shape: pl.BlockSpec(shape, lambda c, q: (0, 0))
    planes = lambda L: pl.BlockSpec((None, 2, 3, L, tc), lambda c, q: (l, 0, 0, 0, c))
    conv_b = conv_b.reshape(DEPTH, 1, 3 * D_HY)
    return pl.pallas_call(
        _hyena_kernel,
        grid=(nc, NQ),
        in_specs=[
            part(0), part(1), part(2), wpart(0), wpart(1), wpart(2), bpart(0), bpart(1), bpart(2),
            pl.BlockSpec((None, 2, tc), lambda c, q: (l, 0, c)),
            planes(SEQ), planes(DEC_SEQ),
            const((2 * SEQ, SEQ)), const((SEQ, 2 * SEQ)),
            const((2 * DEC_SEQ, DEC_SEQ)), const((DEC_SEQ, 2 * DEC_SEQ)),
        ],
        out_specs=pl.BlockSpec((CHUNK, tc), lambda c, q: (q, c)),
        out_shape=jax.ShapeDtypeStruct((N_TOK, D_HY), BF16),
        compiler_params=_params("arbitrary", "arbitrary"),
        name="hyena_mixer",
    )(u, u, u, conv_w, conv_w, conv_w, conv_b, conv_b, conv_b, hbias, p_short, p_long,
      wfs, wis, wfl, wil)


def _fnet_kernel(u_ref, cs_ref, fs_ref, fl_ref, o_ref):
    q = pl.program_id(0)

    def run(L, f_ref):
        x = u_ref[...].astype(BF16)
        parts_c, parts_s = [], []
        for g in range(N_FN_GROUPS):
            r = jnp.dot(x[:, g * FN_GROUP:(g + 1) * FN_GROUP], cs_ref[...], preferred_element_type=F32)
            parts_c.append(r[:, :FN_GROUP])
            parts_s.append(r[:, FN_GROUP:])
        xc = jnp.concatenate(parts_c, axis=1).astype(BF16)
        xs = jnp.concatenate(parts_s, axis=1).astype(BF16)
        norm = 1.0 / math.sqrt(L * FN_GROUP)
        for s in range(CHUNK // L):
            rows = slice(s * L, (s + 1) * L)
            stacked = jnp.concatenate([xc[rows], xs[rows]], axis=0)
            y = jnp.dot(f_ref[...], stacked, preferred_element_type=F32) * norm
            o_ref[rows, :] = y.astype(BF16)

    @pl.when(q < NQ_CTX)
    def _():
        run(SEQ, fs_ref)

    @pl.when(q >= NQ_CTX)
    def _():
        run(DEC_SEQ, fl_ref)


def _fnet(u, cs, fs, fl):
    const = lambda shape: pl.BlockSpec(shape, lambda q: (0, 0))
    return pl.pallas_call(
        _fnet_kernel,
        grid=(NQ,),
        in_specs=[
            pl.BlockSpec((CHUNK, D_FN), lambda q: (q, COL_FN // D_FN)),
            const((FN_GROUP, 2 * FN_GROUP)), const((SEQ, 2 * SEQ)), const((DEC_SEQ, 2 * DEC_SEQ)),
        ],
        out_specs=pl.BlockSpec((CHUNK, D_FN), lambda q: (q, 0)),
        out_shape=jax.ShapeDtypeStruct((N_TOK, D_FN), BF16),
        compiler_params=_params("arbitrary"),
        name="fnet_mixer",
    )(u, cs, fs, fl)


def _rglru_kernel(ux_ref, uy_ref, h0_ref, cw_ref, cb_ref, wg_ref, br_ref, bi_ref, lam_ref,
                  y_ref, st_ref, a_f, b_f, a_b, b_b):
    q = pl.program_id(0)

    def run(L, reset):
        nseq = CHUNK // L
        u = ux_ref[...].astype(F32)
        w = cw_ref[...]
        xr = (cb_ref[...] + _shift_rows(u, 2, L) * w[0:1] + _shift_rows(u, 1, L) * w[1:2]
              + u * w[2:3] + _shift_rows(u, -1, L) * w[3:4])
        xb = xr.astype(BF16)
        tpos = lax.broadcasted_iota(jnp.int32, (CHUNK, 1), 0) & (L - 1)
        neg_lam = -lam_ref[...]
        softplus = jnp.maximum(neg_lam, 0.0) + jnp.log1p(jnp.exp(-jnp.abs(neg_lam)))
        neg_c_softplus = -RG_C * softplus
        for n in range(RG_TC // RG_BLOCK):
            cs = slice(n * RG_BLOCK, (n + 1) * RG_BLOCK)
            gates = jnp.dot(xb[:, cs], wg_ref[n].astype(BF16), preferred_element_type=F32)
            for d, (a_scr, b_scr) in enumerate(((a_f, b_f), (a_b, b_b))):
                c0 = 2 * d * RG_BLOCK
                r = _sigmoid(gates[:, c0:c0 + RG_BLOCK] + br_ref[d:d + 1, cs])
                ig = _sigmoid(gates[:, c0 + RG_BLOCK:c0 + 2 * RG_BLOCK] + bi_ref[d:d + 1, cs])
                log_a = r * neg_c_softplus[d:d + 1, cs]
                th = jnp.tanh(log_a)
                mult = jnp.sqrt(-2.0 * th / (1.0 - th))
                if reset:
                    mult = jnp.where(tpos == (0 if d == 0 else L - 1), 1.0, mult)
                a_scr[:, cs] = jnp.exp(log_a)
                b_scr[:, cs] = mult * (ig * xr[:, cs])

        def step(t, carry):
            h_f, h_b = carry
            new_f, new_b = [], []
            for s in range(nseq):
                rf = s * L + t
                rb = s * L + (L - 1) - t
                hf = a_f[pl.ds(rf, 1), :] * h_f[s] + b_f[pl.ds(rf, 1), :]
                b_f[pl.ds(rf, 1), :] = hf
                hb = a_b[pl.ds(rb, 1), :] * h_b[s] + b_b[pl.ds(rb, 1), :]
                b_b[pl.ds(rb, 1), :] = hb
                new_f.append(hf)
                new_b.append(hb)
            return tuple(new_f), tuple(new_b)

        init = (tuple(h0_ref[s, 0:1, :] for s in range(nseq)),
                tuple(h0_ref[s, 1:2, :] for s in range(nseq)))
        fin_f, fin_b = lax.fori_loop(0, L, step, init, unroll=8)
        y_ref[...] = ((b_f[...] + b_b[...]) * _gelu_tanh(uy_ref[...].astype(F32))).astype(BF16)
        st_ref[...] = jnp.zeros(st_ref.shape, F32)
        for s in range(nseq):
            st_ref[s, 0:1, :] = fin_f[s]
            st_ref[s, 1:2, :] = fin_b[s]

    @pl.when(q < NQ_CTX)
    def _():
        run(SEQ, True)

    @pl.when(q >= NQ_CTX)
    def _():
        run(DEC_SEQ, False)


def _rglru(u, h0, conv_w, conv_b, w_gates, b_r, b_i, lam, l):
    tc = RG_TC
    per_layer = lambda rows: pl.BlockSpec((None, rows, tc), lambda q, c: (l, 0, c))
    state_spec = pl.BlockSpec((None, CTX_PER_CHUNK, 2, tc), lambda q, c: (q, 0, 0, c))
    return pl.pallas_call(
        _rglru_kernel,
        grid=(NQ, D_RG // tc),
        in_specs=[
            pl.BlockSpec((CHUNK, tc), lambda q, c: (q, COL_RX // tc + c)),
            pl.BlockSpec((CHUNK, tc), lambda q, c: (q, COL_RY // tc + c)),
            state_spec,
            per_layer(4), per_layer(1),
            pl.BlockSpec((None, tc // RG_BLOCK, RG_BLOCK, 4 * RG_BLOCK), lambda q, c: (l, c, 0, 0)),
            per_layer(2), per_layer(2), per_layer(2),
        ],
        out_specs=[pl.BlockSpec((CHUNK, tc), lambda q, c: (q, c)), state_spec],
        out_shape=[jax.ShapeDtypeStruct((N_TOK, D_RG), BF16),
                   jax.ShapeDtypeStruct((NQ, CTX_PER_CHUNK, 2, D_RG), F32)],
        scratch_shapes=[pltpu.VMEM((CHUNK, tc), F32)] * 4,
        compiler_params=_params("arbitrary", "arbitrary"),
        name="rglru_mixer",
    )(u, u, h0, conv_w, conv_b.reshape(DEPTH, 1, D_RG), w_gates, b_r, b_i, lam)


def _merge_kernel(x_ref, ya_ref, yb_ref, yc_ref, ga_ref, gb_ref, gc_ref, mod_ref,
                  wa_ref, wb_ref, wc_ref, wo_ref, o_ref, *, tm):
    r = _mod_row(pl.program_id(0) // (CHUNK // tm))
    gate = mod_ref[pl.ds(r, 1), 2 * D_MODEL:3 * D_MODEL]
    ya = jnp.dot(ya_ref[...], wa_ref[...], preferred_element_type=F32)
    yb = jnp.dot(yb_ref[...], wb_ref[...], preferred_element_type=F32)
    yc = jnp.dot(yc_ref[...], wc_ref[...], preferred_element_type=F32)
    ga, gb, gc = (_sigmoid(g_ref[...].astype(F32)) for g_ref in (ga_ref, gb_ref, gc_ref))
    mix = ga * ya + gb * yb + gc * yc
    out = jnp.dot(mix.astype(BF16), wo_ref[...], preferred_element_type=F32)
    o_ref[...] = x_ref[...] + gate * out


def _merge(x, y_hy, y_fn, y_rg, u, mod, w_a, w_b, w_c, w_o, l, tm=512):
    rows = lambda width, col=0: pl.BlockSpec((tm, width), lambda i: (i, col))
    weight = lambda k: pl.BlockSpec((None, k, D_MODEL), lambda i: (l, 0, 0))
    g0 = COL_G // D_MODEL
    return pl.pallas_call(
        functools.partial(_merge_kernel, tm=tm),
        grid=(N_TOK // tm,),
        in_specs=[
            rows(D_MODEL), rows(D_HY), rows(D_FN), rows(D_RG),
            rows(D_MODEL, g0), rows(D_MODEL, g0 + 1), rows(D_MODEL, g0 + 2),
            pl.BlockSpec((None, MOD_ROWS, 6 * D_MODEL), lambda i: (l, 0, 0)),
            weight(D_HY), weight(D_FN), weight(D_RG), weight(D_MODEL),
        ],
        out_specs=rows(D_MODEL),
        out_shape=jax.ShapeDtypeStruct((N_TOK, D_MODEL), F32),
        compiler_params=_params("arbitrary"),
        name="branch_merge",
    )(x, y_hy, y_fn, y_rg, u, u, u, mod, w_a, w_b, w_c, w_o)


def _ffn_kernel(x_ref, mod_ref, g_ref, wg_ref, wu_ref, wd_ref, fg_ref, o_ref, h_scr, *, tm, final_norm):
    i = pl.program_id(0)
    f = pl.program_id(1)
    sub = tm // CHUNK

    @pl.when(f == 0)
    def _():
        for s in range(sub):
            r = _mod_row(i * sub + s)
            shift = mod_ref[pl.ds(r, 1), 3 * D_MODEL:4 * D_MODEL]
            scale = mod_ref[pl.ds(r, 1), 4 * D_MODEL:5 * D_MODEL]
            rows = slice(s * CHUNK, (s + 1) * CHUNK)
            h = _rmsnorm(x_ref[rows, :], g_ref[...]) * (1.0 + scale) + shift
            h_scr[rows, :] = h.astype(BF16)
        o_ref[...] = jnp.zeros(o_ref.shape, F32)

    h = h_scr[...]
    gt = jnp.dot(h, wg_ref[...].astype(BF16), preferred_element_type=F32)
    up = jnp.dot(h, wu_ref[...].astype(BF16), preferred_element_type=F32)
    act = (gt * _sigmoid(gt)) * up
    o_ref[...] += jnp.dot(act.astype(BF16), wd_ref[...].astype(BF16), preferred_element_type=F32)

    @pl.when(f == pl.num_programs(1) - 1)
    def _():
        for s in range(sub):
            r = _mod_row(i * sub + s)
            gate = mod_ref[pl.ds(r, 1), 5 * D_MODEL:6 * D_MODEL]
            rows = slice(s * CHUNK, (s + 1) * CHUNK)
            out = x_ref[rows, :] + gate * o_ref[rows, :]
            if final_norm:
                out = _rmsnorm(out, fg_ref[...])
            o_ref[rows, :] = out


def _ffn(x, mod, norm_g, w_gu, w_down, final_g, l, final_norm, tm=2048, tf=256):
    nf = D_FF // tf
    return pl.pallas_call(
        functools.partial(_ffn_kernel, tm=tm, final_norm=final_norm),
        grid=(N_TOK // tm, nf),
        in_specs=[
            pl.BlockSpec((tm, D_MODEL), lambda i, f: (i, 0)),
            pl.BlockSpec((None, MOD_ROWS, 6 * D_MODEL), lambda i, f: (l, 0, 0)),
            pl.BlockSpec((None, 1, D_MODEL), lambda i, f: (l, 0, 0)),
            pl.BlockSpec((None, D_MODEL, tf), lambda i, f: (l, 0, f)),
            pl.BlockSpec((None, D_MODEL, tf), lambda i, f: (l, 0, nf + f)),
            pl.BlockSpec((None, tf, D_MODEL), lambda i, f: (l, f, 0)),
            pl.BlockSpec((1, D_MODEL), lambda i, f: (0, 0)),
        ],
        out_specs=pl.BlockSpec((tm, D_MODEL), lambda i, f: (i, 0)),
        out_shape=jax.ShapeDtypeStruct((N_TOK, D_MODEL), F32),
        scratch_shapes=[pltpu.VMEM((tm, D_MODEL), BF16)],
        compiler_params=_params("arbitrary", "arbitrary"),
        name="swiglu_ffn",
    )(x, mod, norm_g, w_gu, w_gu, w_down, final_g)


def kernel(x_prompt, x_sample, c, state_rglru, c_ctx, norm1_g, norm2_g, w_ada, b_ada, w_in,
           hy_conv_w, hy_conv_b, hy_f_w1, hy_f_b1, hy_f_w2, hy_f_b2, hy_f_w3, hy_f_freq, hy_bias,
           w_a, w_b, rg_conv_w, rg_conv_b, rg_wr, rg_br, rg_wi, rg_bi, rg_lam, w_c, w_o,
           w_gu, w_down, final_g):
    x = jnp.concatenate([x_prompt.reshape(N_CTX_TOK, D_MODEL), x_sample.reshape(N_LAT_TOK, D_MODEL)])
    cvec = jnp.concatenate([c_ctx[None, :], c, jnp.zeros((MOD_ROWS - 1 - DEC_BATCH, D_MODEL), F32)])
    mod = _modulation(cvec, w_ada, b_ada)

    dft_short = _as_bf16(*_hyena_dft_tables(SEQ))
    dft_long = _as_bf16(*_hyena_dft_tables(DEC_SEQ))
    fnet_short, fnet_chan = _as_bf16(*_fnet_tables(SEQ))
    fnet_long, = _as_bf16(_fnet_tables(DEC_SEQ)[0])
    filt = (hy_f_w1, hy_f_b1, hy_f_w2, hy_f_b2, hy_f_w3, hy_f_freq)
    p_short = _hyena_filters(SEQ, dft_short[0], *filt)
    p_long = _hyena_filters(DEC_SEQ, dft_long[0], *filt)

    w_gates = jnp.concatenate([rg_wr[:, 0], rg_wi[:, 0], rg_wr[:, 1], rg_wi[:, 1]], axis=-1)
    lat_h0 = jnp.pad(state_rglru.astype(F32).transpose(1, 0, 2, 3)[:, :, None],
                     ((0, 0), (0, 0), (0, CTX_PER_CHUNK - 1), (0, 0), (0, 0)))
    h0_all = jnp.concatenate([jnp.zeros((DEPTH, NQ_CTX, CTX_PER_CHUNK, 2, D_RG), F32), lat_h0], axis=1)
    w_a16, w_b16, w_c16, w_o16 = (w.astype(BF16) for w in (w_a, w_b, w_c, w_o))
    norm1 = norm1_g.reshape(DEPTH, 1, D_MODEL)
    norm2 = norm2_g.reshape(DEPTH, 1, D_MODEL)
    final = final_g.reshape(1, D_MODEL)

    states = []
    for l in range(DEPTH):
        u = _input_projection(x, mod, norm1, w_in, l)
        y_hy = _hyena(u, hy_conv_w, hy_conv_b, hy_bias, p_short, p_long, dft_short, dft_long, l)
        y_fn = _fnet(u, fnet_chan, fnet_short, fnet_long)
        y_rg, st = _rglru(u, h0_all[l], rg_conv_w, rg_conv_b, w_gates, rg_br, rg_bi, rg_lam, l)
        states.append(st[:NQ_CTX].reshape(BATCH, 2, D_RG))
        x = _merge(x, y_hy, y_fn, y_rg, u, mod, w_a16, w_b16, w_c16, w_o16, l)
        x = _ffn(x, mod, norm2, w_gu, w_down, final, l, final_norm=(l == DEPTH - 1))

    y_prompt = x[:N_CTX_TOK].reshape(BATCH, SEQ, D_MODEL)
    y_sample = x[N_CTX_TOK:].reshape(DEC_BATCH, DEC_SEQ, D_MODEL)
    new_state = jnp.stack(states, axis=1).astype(x_prompt.dtype)
    return (y_prompt, y_sample, new_state)
```

```python
import functools
import math

import numpy as np
import jax
import jax.numpy as jnp
from jax import lax
from jax.experimental import pallas as pl
from jax.experimental.pallas import tpu as pltpu

F32 = jnp.float32
BF16 = jnp.bfloat16
HIGHEST = lax.Precision.HIGHEST

D_MODEL = 1024
BATCH = 16
SEQ = 256
DEPTH = 4
DEC_BATCH = 4
DEC_SEQ = 1024
EPS = 1e-6
D_HY = 512
N_BANDS = 8
FILT_EMB = 1 + 2 * N_BANDS
FILT_HID = 64
DECAY_SLOW = -math.log(1e-2) / 1.5
DECAY_FAST = -math.log(1e-2) / 0.3
D_FN = 512
FN_GROUP = 128
N_FN_GROUPS = D_FN // FN_GROUP
D_RG = 1024
RG_BLOCK = 128
RG_C = 8.0
D_FF = -(-8 * D_MODEL // (3 * 256)) * 256
D_IN = 3 * D_HY + D_FN + 2 * D_RG + 3 * D_MODEL
COL_FN = 3 * D_HY
COL_RX = COL_FN + D_FN
COL_RY = COL_RX + D_RG
COL_G = COL_RY + D_RG

CHUNK = 1024
N_CTX_TOK = BATCH * SEQ
N_LAT_TOK = DEC_BATCH * DEC_SEQ
N_TOK = N_CTX_TOK + N_LAT_TOK
NQ_CTX = N_CTX_TOK // CHUNK
NQ = N_TOK // CHUNK
CTX_PER_CHUNK = CHUNK // SEQ
MOD_ROWS = 8
FEAT_PAD = 128
HY_TC = 256
HY_ROWS = 2048
RG_TC = 512
VMEM_LIMIT = 56 * 1024 * 1024

assert DEC_SEQ == CHUNK and CHUNK % SEQ == 0 and N_CTX_TOK % CHUNK == 0
assert N_CTX_TOK % HY_ROWS == 0 and N_LAT_TOK % HY_ROWS == 0 and HY_ROWS % DEC_SEQ == 0
assert SEQ % 16 == 0 and DEC_SEQ % 16 == 0
assert 1 + DEC_BATCH <= MOD_ROWS


def _params(*sem):
    return pltpu.CompilerParams(dimension_semantics=sem, vmem_limit_bytes=VMEM_LIMIT)


def _mod_row(q):
    return jnp.maximum(q - (NQ_CTX - 1), 0)


def _rmsnorm(x, g):
    return x * lax.rsqrt(jnp.mean(x * x, axis=-1, keepdims=True) + EPS) * g


def _sigmoid(x):
    return 0.5 * jnp.tanh(0.5 * x) + 0.5


def _gelu_tanh(x):
    inner = x * (math.sqrt(2.0 / math.pi) * 0.044715 * (x * x) + math.sqrt(2.0 / math.pi))
    half = 0.5 * x
    return half * jnp.tanh(inner) + half


def _time_of_row(L):
    p = np.arange(L)
    return (p % 8) * (L // 8) + p // 8


def _to_time_permuted(x, L):
    b = x.shape[0]
    return x.reshape(b, 8, L // 8, x.shape[-1]).transpose(0, 2, 1, 3).reshape(b * L, x.shape[-1])


def _from_time_permuted(x, b, L):
    return x.reshape(b, L // 8, 8, x.shape[-1]).transpose(0, 2, 1, 3).reshape(b, L, x.shape[-1])


def _shift_time(x, k, L):
    sub = lax.broadcasted_iota(jnp.int32, (8, 1), 0)
    out = []
    for s0 in range(0, x.shape[0], L):
        xs = x[s0:s0 + L]
        if k > 0:
            wrap = [jnp.where(sub == 0, 0.0, pltpu.roll(xs[L - 8 * (k - i):L - 8 * (k - i - 1)], 1, axis=0))
                    for i in range(k)]
            out += wrap + [xs[:L - 8 * k]]
        else:
            wrap = [jnp.where(sub == 7, 0.0, pltpu.roll(xs[8 * i:8 * (i + 1)], 7, axis=0))
                    for i in range(-k)]
            out += [xs[-8 * k:]] + wrap
    return jnp.concatenate(out, axis=0)


def _as_bf16(*tables):
    return tuple(jnp.asarray(t, dtype=F32).astype(BF16) for t in tables)


def _angle_table(n_rows, n_cols, period):
    prod = np.outer(np.arange(n_rows, dtype=np.int64), np.arange(n_cols, dtype=np.int64)) % period
    return 2.0 * np.pi * prod.astype(np.float64) / period


@functools.lru_cache(maxsize=None)
def _hyena_dft_tables(L):
    n = 2 * L
    ang = _angle_table(L, L, n)
    alt = np.where(np.arange(L) % 2 == 0, 1.0, -1.0)
    cos_f, sin_f = np.cos(ang), np.sin(ang)
    sin_f[0, :] = alt
    fwd = np.concatenate([cos_f, sin_f], axis=0)
    cos_i, sin_i = 2.0 / n * np.cos(ang), 2.0 / n * np.sin(ang)
    cos_i[:, 0] = 1.0 / n
    sin_i[:, 0] = alt / n
    inv = np.concatenate([cos_i, sin_i], axis=1)
    perm = _time_of_row(L)
    return fwd[:, perm].astype(np.float32), inv[perm, :].astype(np.float32)


@functools.lru_cache(maxsize=None)
def _fnet_tables(L):
    perm = _time_of_row(L)
    ang = _angle_table(L, L, L)[perm][:, perm]
    seq = np.concatenate([np.cos(ang), -np.sin(ang)], axis=1)
    ang_c = _angle_table(FN_GROUP, FN_GROUP, FN_GROUP)
    chan = np.concatenate([np.cos(ang_c), np.sin(ang_c)], axis=1)
    return seq.astype(np.float32), chan.astype(np.float32)


@functools.lru_cache(maxsize=None)
def _filter_tables(L):
    t = np.arange(L, dtype=np.float32) / np.float32(L)
    ang = 2.0 * np.pi * t[:, None].astype(np.float64) * np.arange(1, N_BANDS + 1, dtype=np.float64)
    feats = np.zeros((L, FEAT_PAD), np.float64)
    feats[:, 0] = t
    feats[:, 1:1 + N_BANDS] = np.sin(ang)
    feats[:, 1 + N_BANDS:FILT_EMB] = np.cos(ang)
    deltas = np.linspace(DECAY_SLOW, DECAY_FAST, D_HY, dtype=np.float32).astype(np.float64)
    decay = np.exp(-t[:, None].astype(np.float64) * deltas)
    perm = _time_of_row(L)
    return jnp.asarray(feats[perm], dtype=F32), jnp.asarray(decay[perm], dtype=F32)


def _mod_kernel(c_ref, w_ref, b_ref, o_ref):
    cv = c_ref[...]
    act = cv * _sigmoid(cv)
    o_ref[...] = jnp.dot(act, w_ref[...], precision=HIGHEST, preferred_element_type=F32) + b_ref[...]


def _modulation(cvec, w_ada, b_ada):
    tn = 1536
    return pl.pallas_call(
        _mod_kernel,
        grid=(DEPTH, 6 * D_MODEL // tn),
        in_specs=[
            pl.BlockSpec((MOD_ROWS, D_MODEL), lambda l, j: (0, 0)),
            pl.BlockSpec((None, D_MODEL, tn), lambda l, j: (l, 0, j)),
            pl.BlockSpec((None, 1, tn), lambda l, j: (l, 0, j)),
        ],
        out_specs=pl.BlockSpec((None, MOD_ROWS, tn), lambda l, j: (l, 0, j)),
        out_shape=jax.ShapeDtypeStruct((DEPTH, MOD_ROWS, 6 * D_MODEL), F32),
        compiler_params=_params("arbitrary", "arbitrary"),
        name="adaln_modulation",
    )(cvec, w_ada, b_ada.reshape(DEPTH, 1, 6 * D_MODEL))


def _filter_kernel(feats_ref, decay_ref, wf_ref, w1_ref, b1_ref, w2_ref, b2_ref, w3_ref, fq_ref,
                   p_ref, *, L):
    freq = fq_ref[...]
    h = jnp.sin(freq * (jnp.dot(feats_ref[...], w1_ref[...], precision=HIGHEST,
                                preferred_element_type=F32) + b1_ref[...]))
    h = jnp.sin(freq * (jnp.dot(h, w2_ref[...], precision=HIGHEST,
                                preferred_element_type=F32) + b2_ref[...]))
    h = jnp.dot(h, w3_ref[...], precision=HIGHEST, preferred_element_type=F32)
    decay = decay_ref[...]
    row = lax.broadcasted_iota(jnp.int32, (L, 1), 0)
    h_fwd = h[:, :D_HY] * decay
    h_bwd = jnp.where(row == 0, 0.0, h[:, D_HY:] * decay)
    ssq = jnp.sum(h_fwd * h_fwd, axis=0, keepdims=True) + jnp.sum(h_bwd * h_bwd, axis=0, keepdims=True)
    scale = lax.rsqrt(ssq + EPS)
    even = (h_fwd + h_bwd) * scale
    odd = (h_bwd - h_fwd) * scale
    k_re = jnp.dot(wf_ref[0:L, :], even.astype(BF16), preferred_element_type=F32)
    k_im = jnp.dot(wf_ref[L:2 * L, :], odd.astype(BF16), preferred_element_type=F32)
    alt = jnp.where(((row >> 3) & 1) == 0, 1.0, -1.0)
    k_nyq = jnp.sum(even * alt, axis=0, keepdims=True)
    p_ref[0] = k_re
    p_ref[1] = jnp.where(row == 0, 0.0, k_im)
    p_ref[2] = jnp.where(row == 0, k_nyq, k_re)


def _hyena_filters(L, wf, fw1, fb1, fw2, fb2, fw3, ffreq):
    feats, decay = _filter_tables(L)
    w1 =jnp.pad(fw1, ((0, 0), (0, FEAT_PAD - FILT_EMB), (0, 0)))
    const = lambda shape: pl.BlockSpec(shape, lambda l, o: (0,) * len(shape))
    per_layer = lambda *shape: pl.BlockSpec((None,) + shape, lambda l, o: (l,) + (0,) * len(shape))
    return pl.pallas_call(
        functools.partial(_filter_kernel, L=L),
        grid=(DEPTH, 2),
        in_specs=[
            const((L, FEAT_PAD)), const((L, D_HY)), const((2 * L, L)),
            per_layer(FEAT_PAD, FILT_HID), per_layer(1, FILT_HID),
            per_layer(FILT_HID, FILT_HID), per_layer(1, FILT_HID),
            pl.BlockSpec((None, FILT_HID, 2 * D_HY), lambda l, o: (l, 0, o)),
            per_layer(1, FILT_HID),
        ],
        out_specs=pl.BlockSpec((None, None, 3, L, D_HY), lambda l, o: (l, o, 0, 0, 0)),
        out_shape=jax.ShapeDtypeStruct((DEPTH, 2, 3, L, D_HY), F32),
        compiler_params=_params("arbitrary", "arbitrary"),
        name=f"hyena_filters_{L}",
    )(feats, decay, wf, w1, fb1.reshape(DEPTH, 1, FILT_HID), fw2, fb2.reshape(DEPTH, 1, FILT_HID),
      fw3, ffreq.reshape(DEPTH, 1, FILT_HID))


def _inproj_kernel(x_ref, mod_ref, g_ref, w_ref, u_ref, h_scr, *, tm):
    i = pl.program_id(0)

    @pl.when(pl.program_id(1) == 0)
    def _():
        for s in range(tm // CHUNK):
            r = _mod_row(i * (tm // CHUNK) + s)
            shift = mod_ref[pl.ds(r, 1), 0:D_MODEL]
            scale = mod_ref[pl.ds(r, 1), D_MODEL:2 * D_MODEL]
            rows = slice(s * CHUNK, (s + 1) * CHUNK)
            h = _rmsnorm(x_ref[rows, :], g_ref[...]) * (1.0 + scale) + shift
            h_scr[rows, :] = h.astype(BF16)

    u_ref[...] = jnp.dot(h_scr[...], w_ref[...].astype(BF16), preferred_element_type=F32).astype(BF16)


def _input_projection(x, mod, norm_g, w_in, l, tm=2048, tn=512):
    return pl.pallas_call(
        functools.partial(_inproj_kernel, tm=tm),
        grid=(N_TOK // tm, D_IN // tn),
        in_specs=[
            pl.BlockSpec((tm, D_MODEL), lambda i, j: (i, 0)),
            pl.BlockSpec((None, MOD_ROWS, 6 * D_MODEL), lambda i, j: (l, 0, 0)),
            pl.BlockSpec((None, 1, D_MODEL), lambda i, j: (l, 0, 0)),
            pl.BlockSpec((None, D_MODEL, tn), lambda i, j: (l, 0, j)),
        ],
        out_specs=pl.BlockSpec((tm, tn), lambda i, j: (i, j)),
        out_shape=jax.ShapeDtypeStruct((N_TOK, D_IN), BF16),
        scratch_shapes=[pltpu.VMEM((tm, D_MODEL), BF16)],
        compiler_params=_params("arbitrary", "arbitrary"),
        name="input_projection",
    )(x, mod, norm_g, w_in)


def _fftconv(z, p_ref, order, wf_ref, wi_ref, L):
    spec = jnp.dot(wf_ref[...], z.astype(BF16), preferred_element_type=F32)
    z_re, z_sn = spec[:L], spec[L:]
    p_a, p_b, p_c = p_ref[order, 0], p_ref[order, 1], p_ref[order, 2]
    y_re = z_re * p_a + z_sn * p_b
    y_sn = z_sn * p_c - z_re * p_b
    y = jnp.concatenate([y_re, y_sn], axis=0).astype(BF16)
    return jnp.dot(wi_ref[...], y, preferred_element_type=F32)


def _hyena_kernel(v_ref, x1_ref, x2_ref, wv_ref, w1_ref, w2_ref, bv_ref, b1_ref, b2_ref, hb_ref,
                  ps_ref, pl_ref, wfs_ref, wis_ref, wfl_ref, wil_ref, o_ref):
    q = pl.program_id(1)

    def run(L, p_ref, wf_ref, wi_ref):
        def short_conv(u_ref, w_ref, b_ref):
            u = u_ref[...].astype(F32)
            w = w_ref[...]
            return (b_ref[...] + _shift_time(u, 1, L) * w[0:1] + u * w[1:2]
                    + _shift_time(u, -1, L) * w[2:3])

        v = short_conv(v_ref, wv_ref, bv_ref)
        x1 = short_conv(x1_ref, w1_ref, b1_ref)
        x2 = short_conv(x2_ref, w2_ref, b2_ref)
        hb = hb_ref[...]
        for s in range(HY_ROWS // L):
            rows = slice(s * L, (s + 1) * L)
            vs = v[rows]
            z = x1[rows] * (_fftconv(vs, p_ref, 0, wf_ref, wi_ref, L) + hb[0:1] * vs)
            y = x2[rows] * (_fftconv(z, p_ref, 1, wf_ref, wi_ref, L) + hb[1:2] * z)
            o_ref[rows, :] = y.astype(BF16)

    @pl.when(q < N_CTX_TOK // HY_ROWS)
    def _():
        run(SEQ, ps_ref, wfs_ref, wis_ref)

    @pl.when(q >= N_CTX_TOK // HY_ROWS)
    def _():
        run(DEC_SEQ, pl_ref, wfl_ref, wil_ref)


def _hyena(u, conv_w, conv_b, hbias, p_short, p_long, dft_short, dft_long, l):
    tc = HY_TC
    nc = D_HY // tc
    wfs, wis = dft_short
    wfl, wil = dft_long
    part = lambda k: pl.BlockSpec((HY_ROWS, tc), lambda c, q: (q, k * nc + c))
    wpart = lambda k: pl.BlockSpec((None, 3, tc), lambda c, q: (l, 0, k * nc + c))
    bpart = lambda k: pl.BlockSpec((None, 1, tc), lambda c, q: (l, 0, k * nc + c))
    const = lambda shape: pl.BlockSpec(shape, lambda c, q: (0, 0))
    planes = lambda L: pl.BlockSpec((None, 2, 3, L, tc), lambda c, q: (l, 0, 0, 0, c))
    conv_b = conv_b.reshape(DEPTH, 1, 3 * D_HY)
    return pl.pallas_call(
        _hyena_kernel,
        grid=(nc, N_TOK // HY_ROWS),
        in_specs=[
            part(0), part(1), part(2), wpart(0), wpart(1), wpart(2), bpart(0), bpart(1), bpart(2),
            pl.BlockSpec((None, 2, tc), lambda c, q: (l, 0, c)),
            planes(SEQ), planes(DEC_SEQ),
            const((2 * SEQ, SEQ)), const((SEQ, 2 * SEQ)),
            const((2 * DEC_SEQ, DEC_SEQ)), const((DEC_SEQ, 2 * DEC_SEQ)),
        ],
        out_specs=pl.BlockSpec((HY_ROWS, tc), lambda c, q: (q, c)),
        out_shape=jax.ShapeDtypeStruct((N_TOK, D_HY), BF16),
        compiler_params=_params("arbitrary", "arbitrary"),
        name="hyena_mixer",
    )(u, u, u, conv_w, conv_w, conv_w, conv_b, conv_b, conv_b, hbias, p_short, p_long,
      wfs, wis, wfl, wil)


def _fnet_kernel(u_ref, cs_ref, fs_ref, fl_ref, o_ref):
    q = pl.program_id(0)

    def run(L, f_ref):
        x = u_ref[...].astype(BF16)
        parts_c, parts_s = [], []
        for g in range(N_FN_GROUPS):
            r = jnp.dot(x[:, g * FN_GROUP:(g + 1) * FN_GROUP], cs_ref[...], preferred_element_type=F32)
            parts_c.append(r[:, :FN_GROUP])
            parts_s.append(r[:, FN_GROUP:])
        xc = jnp.concatenate(parts_c, axis=1).astype(BF16)
        xs = jnp.concatenate(parts_s, axis=1).astype(BF16)
        norm = 1.0 / math.sqrt(L * FN_GROUP)
        for s in range(CHUNK // L):
            rows = slice(s * L, (s + 1) * L)
            stacked = jnp.concatenate([xc[rows], xs[rows]], axis=0)
            y = jnp.dot(f_ref[...], stacked, preferred_element_type=F32) * norm
            o_ref[rows, :] = y.astype(BF16)

    @pl.when(q < NQ_CTX)
    def _():
        run(SEQ, fs_ref)

    @pl.when(q >= NQ_CTX)
    def _():
        run(DEC_SEQ, fl_ref)


def _fnet(u, cs, fs, fl):
    const = lambda shape: pl.BlockSpec(shape, lambda q: (0, 0))
    return pl.pallas_call(
        _fnet_kernel,
        grid=(NQ,),
        in_specs=[
            pl.BlockSpec((CHUNK, D_FN), lambda q: (q, COL_FN // D_FN)),
            const((FN_GROUP, 2 * FN_GROUP)), const((SEQ, 2 * SEQ)), const((DEC_SEQ, 2 * DEC_SEQ)),
        ],
        out_specs=pl.BlockSpec((CHUNK, D_FN), lambda q: (q, 0)),
        out_shape=jax.ShapeDtypeStruct((N_TOK, D_FN), BF16),
        compiler_params=_params("arbitrary"),
        name="fnet_mixer",
    )(u, cs, fs, fl)


def _scan_sequence(s, L, a_f, b_f, a_b, b_b, h0_ref, uy_ref, y_ref, st_ref):
    nb = L // 8
    tc = a_f.shape[1]
    s0 = s * L

    def local(j, carry):
        hf, pf, hb, pb = carry
        rf = pl.multiple_of(s0 + 8 * j, 8)
        rb = pl.multiple_of(s0 + L - 8 - 8 * j, 8)
        af = a_f[pl.ds(rf, 8), :]
        hf = af * hf + b_f[pl.ds(rf, 8), :]
        pf = af * pf
        b_f[pl.ds(rf, 8), :] = hf
        a_f[pl.ds(rf, 8), :] = pf
        ab = a_b[pl.ds(rb, 8), :]
        hb = ab * hb + b_b[pl.ds(rb, 8), :]
        pb = ab * pb
        b_b[pl.ds(rb, 8), :] = hb
        a_b[pl.ds(rb, 8), :] = pb
        return hf, pf, hb, pb

    zero = jnp.zeros((8, tc), F32)
    one = jnp.ones((8, tc), F32)
    hf, pf, hb, pb = lax.fori_loop(0, nb, local, (zero, one, zero, one), unroll=4)

    carry = h0_ref[s, 0:1, :]
    rows = []
    for k in range(8):
        rows.append(carry)
        carry = hf[k:k + 1] + pf[k:k + 1] * carry
    carry_f = jnp.concatenate(rows, axis=0)
    st_ref[s, 0:1, :] = carry
    carry = h0_ref[s, 1:2, :]
    rows = [None] * 8
    for k in range(7, -1, -1):
        rows[k] = carry
        carry = hb[k:k + 1] + pb[k:k + 1] * carry
    carry_b = jnp.concatenate(rows, axis=0)
    st_ref[s, 1:2, :] = carry

    sl = slice(s0, s0 + L)
    blocked = lambda ref: ref[sl, :].reshape(nb, 8, tc)
    h_sum = ((blocked(b_f) + blocked(a_f) * carry_f[None]) + (blocked(b_b) + blocked(a_b) * carry_b[None]))
    y_ref[sl, :] = (h_sum.reshape(L, tc) * _gelu_tanh(uy_ref[sl, :].astype(F32))).astype(BF16)


def _rglru_kernel(ux_ref, uy_ref, h0_ref, cw_ref, cb_ref, wg_ref, br_ref, bi_ref, lam_ref,
                  y_ref, st_ref, a_f, b_f, a_b, b_b):
    q = pl.program_id(0)

    def run(L, reset):
        nseq = CHUNK // L
        u = ux_ref[...].astype(F32)
        w = cw_ref[...]
        xr = (cb_ref[...] + _shift_time(u, 2, L) * w[0:1] + _shift_time(u, 1, L) * w[1:2]
              + u * w[2:3] + _shift_time(u, -1, L) * w[3:4])
        xb = xr.astype(BF16)
        tpos = lax.broadcasted_iota(jnp.int32, (CHUNK, 1), 0) & (L - 1)
        neg_lam = -lam_ref[...]
        softplus = jnp.maximum(neg_lam, 0.0) + jnp.log1p(jnp.exp(-jnp.abs(neg_lam)))
        half_rate = (-0.5 * RG_C) * softplus
        half_br = 0.5 * br_ref[...]
        half_bi = 0.5 * bi_ref[...]
        for n in range(RG_TC // RG_BLOCK):
            cs = slice(n * RG_BLOCK, (n + 1) * RG_BLOCK)
            gates = jnp.dot(xb[:, cs], wg_ref[n].astype(BF16), preferred_element_type=F32)
            for d, (a_scr, b_scr) in enumerate(((a_f, b_f), (a_b, b_b))):
                c0 = 2 * d * RG_BLOCK
                t_r = jnp.tanh(gates[:, c0:c0 + RG_BLOCK] + half_br[d:d + 1, cs])
                t_i = jnp.tanh(gates[:, c0 + RG_BLOCK:c0 + 2 * RG_BLOCK] + half_bi[d:d + 1, cs])
                rate = half_rate[d:d + 1, cs]
                log_a = rate * t_r + rate
                th = jnp.tanh(log_a)
                half_mult = jnp.sqrt((-0.5 * th) / (1.0 - th))
                if reset:
                    half_mult = jnp.where(tpos == (0 if d == 0 else L - 1), 0.5, half_mult)
                a_scr[:, cs] = jnp.exp(log_a)
                b_scr[:, cs] = (half_mult * xr[:, cs]) * (t_i + 1.0)

        st_ref[...] = jnp.zeros(st_ref.shape, F32)
        for s in range(nseq):
            _scan_sequence(s, L, a_f, b_f, a_b, b_b, h0_ref, uy_ref, y_ref, st_ref)

    @pl.when(q < NQ_CTX)
    def _():
        run(SEQ, True)

    @pl.when(q >= NQ_CTX)
    def _():
        run(DEC_SEQ, False)


def _rglru(u, h0, conv_w, conv_b, w_gates, b_r, b_i, lam, l):
    tc = RG_TC
    per_layer = lambda rows: pl.BlockSpec((None, rows, tc), lambda q, c: (l, 0, c))
    state_spec = pl.BlockSpec((None, CTX_PER_CHUNK, 2, tc), lambda q, c: (q, 0, 0, c))
    return pl.pallas_call(
        _rglru_kernel,
        grid=(NQ, D_RG // tc),
        in_specs=[
            pl.BlockSpec((CHUNK, tc), lambda q, c: (q, COL_RX // tc + c)),
            pl.BlockSpec((CHUNK, tc), lambda q, c: (q, COL_RY // tc + c)),
            state_spec,
            per_layer(4), per_layer(1),
            pl.BlockSpec((None, tc // RG_BLOCK, RG_BLOCK, 4 * RG_BLOCK), lambda q, c: (l, c, 0, 0)),
            per_layer(2), per_layer(2), per_layer(2),
        ],
        out_specs=[pl.BlockSpec((CHUNK, tc), lambda q, c: (q, c)), state_spec],
        out_shape=[jax.ShapeDtypeStruct((N_TOK, D_RG), BF16),
                   jax.ShapeDtypeStruct((NQ, CTX_PER_CHUNK, 2, D_RG), F32)],
        scratch_shapes=[pltpu.VMEM((CHUNK, tc), F32)] * 4,
        compiler_params=_params("arbitrary", "arbitrary"),
        name="rglru_mixer",
    )(u, u, h0, conv_w, conv_b.reshape(DEPTH, 1, D_RG), w_gates, b_r, b_i, lam)


def _merge_kernel(x_ref, ya_ref, yb_ref, yc_ref, ga_ref, gb_ref, gc_ref, mod_ref,
                  wa_ref, wb_ref, wc_ref, wo_ref, o_ref, *, tm):
    r = _mod_row(pl.program_id(0) // (CHUNK // tm))
    gate = mod_ref[pl.ds(r, 1), 2 * D_MODEL:3 * D_MODEL]
    ya = jnp.dot(ya_ref[...], wa_ref[...], preferred_element_type=F32)
    yb = jnp.dot(yb_ref[...], wb_ref[...], preferred_element_type=F32)
    yc = jnp.dot(yc_ref[...], wc_ref[...], preferred_element_type=F32)
    ga, gb, gc = (_sigmoid(g_ref[...].astype(F32)) for g_ref in (ga_ref, gb_ref, gc_ref))
    mix = ga * ya + gb * yb + gc * yc
    out = jnp.dot(mix.astype(BF16), wo_ref[...], preferred_element_type=F32)
    o_ref[...] = x_ref[...] + gate * out


def _merge(x, y_hy, y_fn, y_rg, u, mod, w_a, w_b, w_c, w_o, l, tm=512):
    rows = lambda width, col=0: pl.BlockSpec((tm, width), lambda i: (i, col))
    weight = lambda k: pl.BlockSpec((None, k, D_MODEL), lambda i: (l, 0, 0))
    g0 = COL_G // D_MODEL
    return pl.pallas_call(
        functools.partial(_merge_kernel, tm=tm),
        grid=(N_TOK // tm,),
        in_specs=[
            rows(D_MODEL), rows(D_HY), rows(D_FN), rows(D_RG),
            rows(D_MODEL, g0), rows(D_MODEL, g0 + 1), rows(D_MODEL, g0 + 2),
            pl.BlockSpec((None, MOD_ROWS, 6 * D_MODEL), lambda i: (l, 0, 0)),
            weight(D_HY), weight(D_FN), weight(D_RG), weight(D_MODEL),
        ],
        out_specs=rows(D_MODEL),
        out_shape=jax.ShapeDtypeStruct((N_TOK, D_MODEL), F32),
        compiler_params=_params("arbitrary"),
        name="branch_merge",
    )(x, y_hy, y_fn, y_rg, u, u, u, mod, w_a, w_b, w_c, w_o)


def _ffn_kernel(x_ref, mod_ref, g_ref, wg_ref, wu_ref, wd_ref, fg_ref, o_ref, h_scr, *, tm, final_norm):
    i = pl.program_id(0)
    f = pl.program_id(1)
    sub = tm // CHUNK

    @pl.when(f == 0)
    def _():
        for s in range(sub):
            r = _mod_row(i * sub + s)
            shift = mod_ref[pl.ds(r, 1), 3 * D_MODEL:4 * D_MODEL]
            scale = mod_ref[pl.ds(r, 1), 4 * D_MODEL:5 * D_MODEL]
            rows = slice(s * CHUNK, (s + 1) * CHUNK)
            h = _rmsnorm(x_ref[rows, :], g_ref[...]) * (1.0 + scale) + shift
            h_scr[rows, :] = h.astype(BF16)
        o_ref[...] = jnp.zeros(o_ref.shape, F32)

    h = h_scr[...]
    gt = jnp.dot(h, wg_ref[...].astype(BF16), preferred_element_type=F32)
    up = jnp.dot(h, wu_ref[...].astype(BF16), preferred_element_type=F32)
    act = (gt * _sigmoid(gt)) * up
    o_ref[...] += jnp.dot(act.astype(BF16), wd_ref[...].astype(BF16), preferred_element_type=F32)

    @pl.when(f == pl.num_programs(1) - 1)
    def _():
        for s in range(sub):
            r = _mod_row(i * sub + s)
            gate = mod_ref[pl.ds(r, 1), 5 * D_MODEL:6 * D_MODEL]
            rows = slice(s * CHUNK, (s + 1) * CHUNK)
            out = x_ref[rows, :] + gate * o_ref[rows, :]
            if final_norm:
                out = _rmsnorm(out, fg_ref[...])
            o_ref[rows, :] = out


def _ffn(x, mod, norm_g, w_gu, w_down, final_g, l, final_norm, tm=2048, tf=256):
    nf = D_FF // tf
    return pl.pallas_call(
        functools.partial(_ffn_kernel, tm=tm, final_norm=final_norm),
        grid=(N_TOK // tm, nf),
        in_specs=[
            pl.BlockSpec((tm, D_MODEL), lambda i, f: (i, 0)),
            pl.BlockSpec((None, MOD_ROWS, 6 * D_MODEL), lambda i, f: (l, 0, 0)),
            pl.BlockSpec((None, 1, D_MODEL), lambda i, f: (l, 0, 0)),
            pl.BlockSpec((None, D_MODEL, tf), lambda i, f: (l, 0, f)),
            pl.BlockSpec((None, D_MODEL, tf), lambda i, f: (l, 0, nf + f)),
            pl.BlockSpec((None, tf, D_MODEL), lambda i, f: (l, f, 0)),
            pl.BlockSpec((1, D_MODEL), lambda i, f: (0, 0)),
        ],
        out_specs=pl.BlockSpec((tm, D_MODEL), lambda i, f: (i, 0)),
        out_shape=jax.ShapeDtypeStruct((N_TOK, D_MODEL), F32),
        scratch_shapes=[pltpu.VMEM((tm, D_MODEL), BF16)],
        compiler_params=_params("arbitrary", "arbitrary"),
        name="swiglu_ffn",
    )(x, mod, norm_g, w_gu, w_gu, w_down, final_g)


def kernel(x_prompt, x_sample, c, state_rglru, c_ctx, norm1_g, norm2_g, w_ada, b_ada, w_in,
           hy_conv_w, hy_conv_b, hy_f_w1, hy_f_b1, hy_f_w2, hy_f_b2, hy_f_w3, hy_f_freq, hy_bias,
           w_a, w_b, rg_conv_w, rg_conv_b, rg_wr, rg_br, rg_wi, rg_bi, rg_lam, w_c, w_o,
           w_gu, w_down, final_g):
    x = jnp.concatenate([_to_time_permuted(x_prompt, SEQ), _to_time_permuted(x_sample, DEC_SEQ)])
    cvec = jnp.concatenate([c_ctx[None, :], c, jnp.zeros((MOD_ROWS - 1 - DEC_BATCH, D_MODEL), F32)])
    mod = _modulation(cvec, w_ada, b_ada)

    dft_short = _as_bf16(*_hyena_dft_tables(SEQ))
    dft_long = _as_bf16(*_hyena_dft_tables(DEC_SEQ))
    fnet_short, fnet_chan = _as_bf16(*_fnet_tables(SEQ))
    fnet_long, = _as_bf16(_fnet_tables(DEC_SEQ)[0])
    filt = (hy_f_w1, hy_f_b1, hy_f_w2, hy_f_b2, hy_f_w3, hy_f_freq)
    p_short = _hyena_filters(SEQ, dft_short[0], *filt)
    p_long = _hyena_filters(DEC_SEQ, dft_long[0], *filt)

    w_gates = 0.5 * jnp.concatenate([rg_wr[:, 0], rg_wi[:, 0], rg_wr[:, 1], rg_wi[:, 1]], axis=-1)
    lat_h0 = jnp.pad(state_rglru.astype(F32).transpose(1, 0, 2, 3)[:, :, None],
                     ((0, 0), (0, 0), (0, CTX_PER_CHUNK - 1), (0, 0), (0, 0)))
    h0_all = jnp.concatenate([jnp.zeros((DEPTH, NQ_CTX, CTX_PER_CHUNK, 2, D_RG), F32), lat_h0], axis=1)
    w_a16, w_b16, w_c16, w_o16 = (w.astype(BF16) for w in (w_a, w_b, w_c, w_o))
    norm1 = norm1_g.reshape(DEPTH, 1, D_MODEL)
    norm2 = norm2_g.reshape(DEPTH, 1, D_MODEL)
    final = final_g.reshape(1, D_MODEL)

    states = []
    for l in range(DEPTH):
        u = _input_projection(x, mod, norm1, w_in, l)
        y_hy = _hyena(u, hy_conv_w, hy_conv_b, hy_bias, p_short, p_long, dft_short, dft_long, l)
        y_fn = _fnet(u, fnet_chan, fnet_short, fnet_long)
        y_rg, st = _rglru(u, h0_all[l], rg_conv_w, rg_conv_b, w_gates, rg_br, rg_bi, rg_lam, l)
        states.append(st[:NQ_CTX].reshape(BATCH, 2, D_RG))
        x = _merge(x, y_hy, y_fn, y_rg, u, mod, w_a16, w_b16, w_c16, w_o16, l)
        x = _ffn(x, mod, norm2, w_gu, w_down, final, l, final_norm=(l == DEPTH - 1))

    y_prompt = _from_time_permuted(x[:N_CTX_TOK], BATCH, SEQ)
    y_sample = _from_time_permuted(x[N_CTX_TOK:], DEC_BATCH, DEC_SEQ)
    new_state = jnp.stack(states, axis=1).astype(x_prompt.dtype)
    return (y_prompt, y_sample, new_state)
```

```python
import functools
import math

import numpy as np
import jax
import jax.numpy as jnp
from jax import lax
from jax.experimental import pallas as pl
from jax.experimental.pallas import tpu as pltpu

F32 = jnp.float32
BF16 = jnp.bfloat16
HIGHEST = lax.Precision.HIGHEST

D_MODEL = 1024
BATCH = 16
SEQ = 256
DEPTH = 4
DEC_BATCH = 4
DEC_SEQ = 1024
EPS = 1e-6
TINY_F32 = float(np.finfo(np.float32).tiny)
D_HY = 512
N_BANDS = 8
FILT_EMB = 1 + 2 * N_BANDS
FILT_HID = 64
DECAY_SLOW = -math.log(1e-2) / 1.5
DECAY_FAST = -math.log(1e-2) / 0.3
D_FN = 512
FN_GROUP = 128
N_FN_GROUPS = D_FN // FN_GROUP
D_RG = 1024
RG_BLOCK = 128
RG_C = 8.0
D_FF = -(-8 * D_MODEL // (3 * 256)) * 256
D_IN = 3 * D_HY + D_FN + 2 * D_RG + 3 * D_MODEL
COL_FN = 3 * D_HY
COL_RX = COL_FN + D_FN
COL_RY = COL_RX + D_RG
COL_G = COL_RY + D_RG

CHUNK = 1024
N_CTX_TOK = BATCH * SEQ
N_LAT_TOK = DEC_BATCH * DEC_SEQ
N_TOK = N_CTX_TOK + N_LAT_TOK
NQ_CTX = N_CTX_TOK // CHUNK
NQ = N_TOK // CHUNK
CTX_PER_CHUNK = CHUNK // SEQ
MOD_ROWS = 8
FEAT_PAD = 128
HY_TC = 256
HY_ROWS = 2048
RG_TC = 512
VMEM_LIMIT = 56 * 1024 * 1024

assert DEC_SEQ == CHUNK and CHUNK % SEQ == 0 and N_CTX_TOK % CHUNK == 0
assert N_CTX_TOK % HY_ROWS == 0 and N_LAT_TOK % HY_ROWS == 0 and HY_ROWS % DEC_SEQ == 0
assert SEQ % 16 == 0 and DEC_SEQ % 16 == 0
assert 1 + DEC_BATCH <= MOD_ROWS


def _params(*sem):
    return pltpu.CompilerParams(dimension_semantics=sem, vmem_limit_bytes=VMEM_LIMIT)


def _mod_row(q):
    return jnp.maximum(q - (NQ_CTX - 1), 0)


def _rmsnorm(x, g):
    return x * lax.rsqrt(jnp.mean(x * x, axis=-1, keepdims=True) + EPS) * g


def _sigmoid(x):
    return 0.5 * jnp.tanh(0.5 * x) + 0.5


def _gelu_tanh(x):
    inner = x * (math.sqrt(2.0 / math.pi) * 0.044715 * (x * x) + math.sqrt(2.0 / math.pi))
    half = 0.5 * x
    return half * jnp.tanh(inner) + half


def _time_of_row(L):
    p = np.arange(L)
    return (p % 8) * (L // 8) + p // 8


def _to_time_permuted(x, L):
    b = x.shape[0]
    return x.reshape(b, 8, L // 8, x.shape[-1]).transpose(0, 2, 1, 3).reshape(b * L, x.shape[-1])


def _from_time_permuted(x, b, L):
    return x.reshape(b, L // 8, 8, x.shape[-1]).transpose(0, 2, 1, 3).reshape(b, L, x.shape[-1])


def _shift_time(x, k, L):
    sub = lax.broadcasted_iota(jnp.int32, (8, 1), 0)
    out = []
    for s0 in range(0, x.shape[0], L):
        xs = x[s0:s0 + L]
        if k > 0:
            wrap = [jnp.where(sub == 0, 0.0, pltpu.roll(xs[L - 8 * (k - i):L - 8 * (k - i - 1)], 1, axis=0))
                    for i in range(k)]
            out += wrap + [xs[:L - 8 * k]]
        else:
            wrap = [jnp.where(sub == 7, 0.0, pltpu.roll(xs[8 * i:8 * (i + 1)], 7, axis=0))
                    for i in range(-k)]
            out += [xs[-8 * k:]] + wrap
    return jnp.concatenate(out, axis=0)


def _as_bf16(*tables):
    return tuple(jnp.asarray(t, dtype=F32).astype(BF16) for t in tables)


def _angle_table(n_rows, n_cols, period):
    prod = np.outer(np.arange(n_rows, dtype=np.int64), np.arange(n_cols, dtype=np.int64)) % period
    return 2.0 * np.pi * prod.astype(np.float64) / period


@functools.lru_cache(maxsize=None)
def _hyena_dft_tables(L):
    n = 2 * L
    ang = _angle_table(L, L, n)
    alt = np.where(np.arange(L) % 2 == 0, 1.0, -1.0)
    cos_f, sin_f = np.cos(ang), np.sin(ang)
    sin_f[0, :] = alt
    fwd = np.concatenate([cos_f, sin_f], axis=0)
    cos_i, sin_i = 2.0 / n * np.cos(ang), 2.0 / n * np.sin(ang)
    cos_i[:, 0] = 1.0 / n
    sin_i[:, 0] = alt / n
    inv = np.concatenate([cos_i, sin_i], axis=1)
    perm = _time_of_row(L)
    return fwd[:, perm].astype(np.float32), inv[perm, :].astype(np.float32)


@functools.lru_cache(maxsize=None)
def _fnet_tables(L):
    perm = _time_of_row(L)
    ang = _angle_table(L, L, L)[perm][:, perm]
    seq = np.concatenate([np.cos(ang), -np.sin(ang)], axis=1)
    ang_c = _angle_table(FN_GROUP, FN_GROUP, FN_GROUP)
    chan = np.concatenate([np.cos(ang_c), np.sin(ang_c)], axis=1)
    return seq.astype(np.float32), chan.astype(np.float32)


@functools.lru_cache(maxsize=None)
def _filter_tables(L):
    t = np.arange(L, dtype=np.float32) / np.float32(L)
    ang = 2.0 * np.pi * t[:, None].astype(np.float64) * np.arange(1, N_BANDS + 1, dtype=np.float64)
    feats = np.zeros((L, FEAT_PAD), np.float64)
    feats[:, 0] = t
    feats[:, 1:1 + N_BANDS] = np.sin(ang)
    feats[:, 1 + N_BANDS:FILT_EMB] = np.cos(ang)
    deltas = np.linspace(DECAY_SLOW, DECAY_FAST, D_HY, dtype=np.float32).astype(np.float64)
    decay = np.exp(-t[:, None].astype(np.float64) * deltas)
    perm = _time_of_row(L)
    return jnp.asarray(feats[perm], dtype=F32), jnp.asarray(decay[perm], dtype=F32)


def _mod_kernel(c_ref, w_ref, b_ref, o_ref):
    cv = c_ref[...]
    act = cv * _sigmoid(cv)
    o_ref[...] = jnp.dot(act, w_ref[...], precision=HIGHEST, preferred_element_type=F32) + b_ref[...]


def _modulation(cvec, w_ada, b_ada):
    tn = 1536
    return pl.pallas_call(
        _mod_kernel,
        grid=(DEPTH, 6 * D_MODEL // tn),
        in_specs=[
            pl.BlockSpec((MOD_ROWS, D_MODEL), lambda l, j: (0, 0)),
            pl.BlockSpec((None, D_MODEL, tn), lambda l, j: (l, 0, j)),
            pl.BlockSpec((None, 1, tn), lambda l, j: (l, 0, j)),
        ],
        out_specs=pl.BlockSpec((None, MOD_ROWS, tn), lambda l, j: (l, 0, j)),
        out_shape=jax.ShapeDtypeStruct((DEPTH, MOD_ROWS, 6 * D_MODEL), F32),
        compiler_params=_params("arbitrary", "arbitrary"),
        name="adaln_modulation",
    )(cvec, w_ada, b_ada.reshape(DEPTH, 1, 6 * D_MODEL))


def _filter_kernel(feats_ref, decay_ref, wf_ref, w1_ref, b1_ref, w2_ref, b2_ref, w3_ref, fq_ref,
                   p_ref, hid_scr, *, L):
    @pl.when(pl.program_id(1) == 0)
    def _():
        freq = fq_ref[...]
        h = jnp.sin(freq * (jnp.dot(feats_ref[...], w1_ref[...], precision=HIGHEST,
                                    preferred_element_type=F32) + b1_ref[...]))
        hid_scr[...] = jnp.sin(freq * (jnp.dot(h, w2_ref[...], precision=HIGHEST,
                                               preferred_element_type=F32) + b2_ref[...]))

    h = jnp.dot(hid_scr[...], w3_ref[...], precision=HIGHEST, preferred_element_type=F32)
    decay = decay_ref[...]
    row = lax.broadcasted_iota(jnp.int32, (L, 1), 0)
    h_fwd = h[:, :D_HY] * decay
    h_bwd = jnp.where(row == 0, 0.0, h[:, D_HY:] * decay)
    ssq = jnp.sum(h_fwd * h_fwd, axis=0, keepdims=True) + jnp.sum(h_bwd * h_bwd, axis=0, keepdims=True)
    scale = lax.rsqrt(ssq + EPS)
    even = (h_fwd + h_bwd) * scale
    odd = (h_bwd - h_fwd) * scale
    k_re = jnp.dot(wf_ref[0:L, :], even.astype(BF16), preferred_element_type=F32)
    k_im = jnp.dot(wf_ref[L:2 * L, :], odd.astype(BF16), preferred_element_type=F32)
    alt = jnp.where(((row >> 3) & 1) == 0, 1.0, -1.0)
    k_nyq = jnp.sum(even * alt, axis=0, keepdims=True)
    p_ref[0] = k_re
    p_ref[1] = jnp.where(row == 0, 0.0, k_im)
    p_ref[2] = jnp.where(row == 0, k_nyq, k_re)


def _hyena_filters(L, wf, fw1, fb1, fw2, fb2, fw3, ffreq):
    feats, decay = _filter_tables(L)
    w1 = jnp.pad(fw1, ((0, 0), (0, FEAT_PAD - FILT_EMB), (0, 0)))
    const = lambda shape: pl.BlockSpec(shape, lambda l, o: (0,) * len(shape))
    per_layer = lambda *shape: pl.BlockSpec((None,) + shape, lambda l, o: (l,) + (0,) * len(shape))
    return pl.pallas_call(
        functools.partial(_filter_kernel, L=L),
        grid=(DEPTH, 2),
        in_specs=[
            const((L, FEAT_PAD)), const((L, D_HY)), const((2 * L, L)),
            per_layer(FEAT_PAD, FILT_HID), per_layer(1, FILT_HID),
            per_layer(FILT_HID, FILT_HID), per_layer(1, FILT_HID),
            pl.BlockSpec((None, FILT_HID, 2 * D_HY), lambda l, o: (l, 0, o)),
            per_layer(1, FILT_HID),
        ],
        out_specs=pl.BlockSpec((None, None, 3, L, D_HY), lambda l, o: (l, o, 0, 0, 0)),
        out_shape=jax.ShapeDtypeStruct((DEPTH, 2, 3, L, D_HY), F32),
        scratch_shapes=[pltpu.VMEM((L, FILT_HID), F32)],
        compiler_params=_params("arbitrary", "arbitrary"),
        name=f"hyena_filters_{L}",
    )(feats, decay, wf, w1, fb1.reshape(DEPTH, 1, FILT_HID), fw2, fb2.reshape(DEPTH, 1, FILT_HID),
      fw3, ffreq.reshape(DEPTH, 1, FILT_HID))


def _inproj_kernel(x_ref, mod_ref, g_ref, w_ref, u_ref, h_scr, *, tm):
    i = pl.program_id(0)

    @pl.when(pl.program_id(1) == 0)
    def _():
        for s in range(tm // CHUNK):
            r = _mod_row(i * (tm // CHUNK) + s)
            shift = mod_ref[pl.ds(r, 1), 0:D_MODEL]
            scale = mod_ref[pl.ds(r, 1), D_MODEL:2 * D_MODEL]
            rows = slice(s * CHUNK, (s + 1) * CHUNK)
            h = _rmsnorm(x_ref[rows, :], g_ref[...]) * (1.0 + scale) + shift
            h_scr[rows, :] = h.astype(BF16)

    u_ref[...] = jnp.dot(h_scr[...], w_ref[...].astype(BF16), preferred_element_type=F32).astype(BF16)


def _input_projection(x, mod, norm_g, w_in, l, tm=2048, tn=512):
    return pl.pallas_call(
        functools.partial(_inproj_kernel, tm=tm),
        grid=(N_TOK // tm, D_IN // tn),
        in_specs=[
            pl.BlockSpec((tm, D_MODEL), lambda i, j: (i, 0)),
            pl.BlockSpec((None, MOD_ROWS, 6 * D_MODEL), lambda i, j: (l, 0, 0)),
            pl.BlockSpec((None, 1, D_MODEL), lambda i, j: (l, 0, 0)),
            pl.BlockSpec((None, D_MODEL, tn), lambda i, j: (l, 0, j)),
        ],
        out_specs=pl.BlockSpec((tm, tn), lambda i, j: (i, j)),
        out_shape=jax.ShapeDtypeStruct((N_TOK, D_IN), BF16),
        scratch_shapes=[pltpu.VMEM((tm, D_MODEL), BF16)],
        compiler_params=_params("arbitrary", "arbitrary"),
        name="input_projection",
    )(x, mod, norm_g, w_in)


def _fftconv(z, p_ref, order, wf_ref, wi_ref, L):
    spec = jnp.dot(wf_ref[...], z.astype(BF16), preferred_element_type=F32)
    z_re, z_sn = spec[:L], spec[L:]
    p_a, p_b, p_c = p_ref[order, 0], p_ref[order, 1], p_ref[order, 2]
    y_re = z_re * p_a + z_sn * p_b
    y_sn = z_sn * p_c - z_re * p_b
    y = jnp.concatenate([y_re, y_sn], axis=0).astype(BF16)
    return jnp.dot(wi_ref[...], y, preferred_element_type=F32)


def _hyena_kernel(v_ref, x1_ref, x2_ref, wv_ref, w1_ref, w2_ref, bv_ref, b1_ref, b2_ref, hb_ref,
                  ps_ref, pl_ref, wfs_ref, wis_ref, wfl_ref, wil_ref, o_ref):
    q = pl.program_id(1)

    def run(L, p_ref, wf_ref, wi_ref):
        def short_conv(u_ref, w_ref, b_ref):
            u = u_ref[...].astype(F32)
            w = w_ref[...]
            return (b_ref[...] + _shift_time(u, 1, L) * w[0:1] + u * w[1:2]
                    + _shift_time(u, -1, L) * w[2:3])

        v = short_conv(v_ref, wv_ref, bv_ref)
        x1 = short_conv(x1_ref, w1_ref, b1_ref)
        x2 = short_conv(x2_ref, w2_ref, b2_ref)
        hb = hb_ref[...]
        for s in range(HY_ROWS // L):
            rows = slice(s * L, (s + 1) * L)
            vs = v[rows]
            z = x1[rows] * (_fftconv(vs, p_ref, 0, wf_ref, wi_ref, L) + hb[0:1] * vs)
            y = x2[rows] * (_fftconv(z, p_ref, 1, wf_ref, wi_ref, L) + hb[1:2] * z)
            o_ref[rows, :] = y.astype(BF16)

    @pl.when(q < N_CTX_TOK // HY_ROWS)
    def _():
        run(SEQ, ps_ref, wfs_ref, wis_ref)

    @pl.when(q >= N_CTX_TOK // HY_ROWS)
    def _():
        run(DEC_SEQ, pl_ref, wfl_ref, wil_ref)


def _hyena(u, conv_w, conv_b, hbias, p_short, p_long, dft_short, dft_long, l):
    tc = HY_TC
    nc = D_HY // tc
    wfs, wis = dft_short
    wfl, wil = dft_long
    part = lambda k: pl.BlockSpec((HY_ROWS, tc), lambda c, q: (q, k * nc + c))
    wpart = lambda k: pl.BlockSpec((None, 3, tc), lambda c, q: (l, 0, k * nc + c))
    bpart = lambda k: pl.BlockSpec((None, 1, tc), lambda c, q: (l, 0, k * nc + c))
    const = lambda shape: pl.BlockSpec(shape, lambda c, q: (0, 0))
    planes = lambda L: pl.BlockSpec((None, 2, 3, L, tc), lambda c, q: (l, 0, 0, 0, c))
    conv_b = conv_b.reshape(DEPTH, 1, 3 * D_HY)
    return pl.pallas_call(
        _hyena_kernel,
        grid=(nc, N_TOK // HY_ROWS),
        in_specs=[
            part(0), part(1), part(2), wpart(0), wpart(1), wpart(2), bpart(0), bpart(1), bpart(2),
            pl.BlockSpec((None, 2, tc), lambda c, q: (l, 0, c)),
            planes(SEQ), planes(DEC_SEQ),
            const((2 * SEQ, SEQ)), const((SEQ, 2 * SEQ)),
            const((2 * DEC_SEQ, DEC_SEQ)), const((DEC_SEQ, 2 * DEC_SEQ)),
        ],
        out_specs=pl.BlockSpec((HY_ROWS, tc), lambda c, q: (q, c)),
        out_shape=jax.ShapeDtypeStruct((N_TOK, D_HY), BF16),
        compiler_params=_params("arbitrary", "arbitrary"),
        name="hyena_mixer",
    )(u, u, u, conv_w, conv_w, conv_w, conv_b, conv_b, conv_b, hbias, p_short, p_long,
      wfs, wis, wfl, wil)


def _fnet_kernel(u_ref, cs_ref, fs_ref, fl_ref, o_ref):
    q = pl.program_id(0)

    def run(L, f_ref):
        x = u_ref[...].astype(BF16)
        parts_c, parts_s = [], []
        for g in range(N_FN_GROUPS):
            r = jnp.dot(x[:, g * FN_GROUP:(g + 1) * FN_GROUP], cs_ref[...], preferred_element_type=F32)
            parts_c.append(r[:, :FN_GROUP])
            parts_s.append(r[:, FN_GROUP:])
        xc = jnp.concatenate(parts_c, axis=1).astype(BF16)
        xs = jnp.concatenate(parts_s, axis=1).astype(BF16)
        norm = 1.0 / math.sqrt(L * FN_GROUP)
        for s in range(CHUNK // L):
            rows = slice(s * L, (s + 1) * L)
            stacked = jnp.concatenate([xc[rows], xs[rows]], axis=0)
            y = jnp.dot(f_ref[...], stacked, preferred_element_type=F32) * norm
            o_ref[rows, :] = y.astype(BF16)

    @pl.when(q < NQ_CTX)
    def _():
        run(SEQ, fs_ref)

    @pl.when(q >= NQ_CTX)
    def _():
        run(DEC_SEQ, fl_ref)


def _fnet(u, cs, fs, fl):
    const = lambda shape: pl.BlockSpec(shape, lambda q: (0, 0))
    return pl.pallas_call(
        _fnet_kernel,
        grid=(NQ,),
        in_specs=[
            pl.BlockSpec((CHUNK, D_FN), lambda q: (q, COL_FN // D_FN)),
            const((FN_GROUP, 2 * FN_GROUP)), const((SEQ, 2 * SEQ)), const((DEC_SEQ, 2 * DEC_SEQ)),
        ],
        out_specs=pl.BlockSpec((CHUNK, D_FN), lambda q: (q, 0)),
        out_shape=jax.ShapeDtypeStruct((N_TOK, D_FN), BF16),
        compiler_params=_params("arbitrary"),
        name="fnet_mixer",
    )(u, cs, fs, fl)


def _scan_sequence(s, L, a_f, b_f, a_b, b_b, h0_ref, y_ref, st_ref):
    nb = L // 8
    tc = a_f.shape[1]
    s0 = s * L

    def local(j, carry):
        hf, pf, hb, pb = carry
        rf = pl.multiple_of(s0 + 8 * j, 8)
        rb = pl.multiple_of(s0 + L - 8 - 8 * j, 8)
        af = a_f[pl.ds(rf, 8), :]
        hf = af * hf + b_f[pl.ds(rf, 8), :]
        pf = af * pf
        b_f[pl.ds(rf, 8), :] = hf
        a_f[pl.ds(rf, 8), :] = pf
        ab = a_b[pl.ds(rb, 8), :]
        hb = ab * hb + b_b[pl.ds(rb, 8), :]
        pb = ab * pb
        b_b[pl.ds(rb, 8), :] = hb
        a_b[pl.ds(rb, 8), :] = pb
        return hf, pf, hb, pb

    zero = jnp.zeros((8, tc), F32)
    one = jnp.ones((8, tc), F32)
    hf, pf, hb, pb = lax.fori_loop(0, nb, local, (zero, one, zero, one), unroll=4)

    carry = h0_ref[s, 0:1, :]
    rows = []
    for k in range(8):
        rows.append(carry)
        carry = hf[k:k + 1] + pf[k:k + 1] * carry
    carry_f = jnp.concatenate(rows, axis=0)
    st_ref[s, 0:1, :] = carry
    carry = h0_ref[s, 1:2, :]
    rows = [None] * 8
    for k in range(7, -1, -1):
        rows[k] = carry
        carry = hb[k:k + 1] + pb[k:k + 1] * carry
    carry_b = jnp.concatenate(rows, axis=0)
    st_ref[s, 1:2, :] = carry

    sl = slice(s0, s0 + L)
    blocked = lambda ref: ref[sl, :].reshape(nb, 8, tc)
    h_sum = ((blocked(b_f) + blocked(a_f) * carry_f[None]) + (blocked(b_b) + blocked(a_b) * carry_b[None]))
    y_ref[sl, :] = h_sum.reshape(L, tc).astype(BF16)


def _rglru_kernel(ux_ref, h0_ref, cw_ref, cb_ref, wg_ref, br_ref, bi_ref, lam_ref,
                  y_ref, st_ref, a_f, b_f, a_b, b_b):
    q = pl.program_id(0)

    def run(L, reset):
        nseq = CHUNK // L
        u = ux_ref[...].astype(F32)
        w = cw_ref[...]
        xr = (cb_ref[...] + _shift_time(u, 2, L) * w[0:1] + _shift_time(u, 1, L) * w[1:2]
              + u * w[2:3] + _shift_time(u, -1, L) * w[3:4])
        xb = xr.astype(BF16)
        tpos = lax.broadcasted_iota(jnp.int32, (CHUNK, 1), 0) & (L - 1)
        neg_lam = -lam_ref[...]
        softplus = jnp.maximum(neg_lam, 0.0) + jnp.log1p(jnp.exp(-jnp.abs(neg_lam)))
        quarter_rate = (-0.25 * RG_C) * softplus
        half_br = 0.5 * br_ref[...]
        half_bi = 0.5 * bi_ref[...]
        for n in range(RG_TC // RG_BLOCK):
            cs = slice(n * RG_BLOCK, (n + 1) * RG_BLOCK)
            gates = jnp.dot(xb[:, cs], wg_ref[n].astype(BF16), preferred_element_type=F32)
            for d, (a_scr, b_scr) in enumerate(((a_f, b_f), (a_b, b_b))):
                c0 = 2 * d * RG_BLOCK
                t_r = jnp.tanh(gates[:, c0:c0 + RG_BLOCK] + half_br[d:d + 1, cs])
                t_i = jnp.tanh(gates[:, c0 + RG_BLOCK:c0 + 2 * RG_BLOCK] + half_bi[d:d + 1, cs])
                rate = quarter_rate[d:d + 1, cs]
                th = jnp.tanh(rate * t_r + rate)
                recip = 1.0 / (1.0 - th)
                neg_th = -th
                root = neg_th * lax.rsqrt(jnp.maximum(neg_th, TINY_F32))
                half_mult = recip * root
                if reset:
                    half_mult = jnp.where(tpos == (0 if d == 0 else L - 1), 0.5, half_mult)
                a_scr[:, cs] = (1.0 + th) * recip
                b_scr[:, cs] = (half_mult * xr[:, cs]) * (t_i + 1.0)

        st_ref[...] = jnp.zeros(st_ref.shape, F32)
        for s in range(nseq):
            _scan_sequence(s, L, a_f, b_f, a_b, b_b, h0_ref, y_ref, st_ref)

    @pl.when(q < NQ_CTX)
    def _():
        run(SEQ, True)

    @pl.when(q >= NQ_CTX)
    def _():
        run(DEC_SEQ, False)


def _rglru(u, h0, conv_w, conv_b, w_gates, b_r, b_i, lam, l):
    tc = RG_TC
    per_layer = lambda rows: pl.BlockSpec((None, rows, tc), lambda q, c: (l, 0, c))
    state_spec = pl.BlockSpec((None, CTX_PER_CHUNK, 2, tc), lambda q, c: (q, 0, 0, c))
    return pl.pallas_call(
        _rglru_kernel,
        grid=(NQ, D_RG // tc),
        in_specs=[
            pl.BlockSpec((CHUNK, tc), lambda q, c: (q, COL_RX // tc + c)),
            state_spec,
            per_layer(4), per_layer(1),
            pl.BlockSpec((None, tc // RG_BLOCK, RG_BLOCK, 4 * RG_BLOCK), lambda q, c: (l, c, 0, 0)),
            per_layer(2), per_layer(2), per_layer(2),
        ],
        out_specs=[pl.BlockSpec((CHUNK, tc), lambda q, c: (q, c)), state_spec],
        out_shape=[jax.ShapeDtypeStruct((N_TOK, D_RG), BF16),
                   jax.ShapeDtypeStruct((NQ, CTX_PER_CHUNK, 2, D_RG), F32)],
        scratch_shapes=[pltpu.VMEM((CHUNK, tc), F32)] * 4,
        compiler_params=_params("arbitrary", "arbitrary"),
        name="rglru_mixer",
    )(u, h0, conv_w, conv_b.reshape(DEPTH, 1, D_RG), w_gates, b_r, b_i, lam)


def _merge_kernel(x_ref, ya_ref, yb_ref, hc_ref, uy_ref, ga_ref, gb_ref, gc_ref, mod_ref,
                  wa_ref, wb_ref, wc_ref, wo_ref, o_ref, *, tm):
    r = _mod_row(pl.program_id(0) // (CHUNK // tm))
    gate = mod_ref[pl.ds(r, 1), 2 * D_MODEL:3 * D_MODEL]
    project = lambda y, w_ref: jnp.dot(y, w_ref[...].astype(BF16), preferred_element_type=F32)
    ya = project(ya_ref[...], wa_ref)
    yb = project(yb_ref[...], wb_ref)
    rg = hc_ref[...].astype(F32) * _gelu_tanh(uy_ref[...].astype(F32))
    yc = project(rg.astype(BF16), wc_ref)
    ga, gb, gc = (_sigmoid(g_ref[...].astype(F32)) for g_ref in (ga_ref, gb_ref, gc_ref))
    mix = ga * ya + gb * yb + gc * yc
    o_ref[...] = x_ref[...] + gate * project(mix.astype(BF16), wo_ref)


def _merge(x, y_hy, y_fn, h_rg, u, mod, w_a, w_b, w_c, w_o, l, tm=512):
    rows = lambda width, col=0: pl.BlockSpec((tm, width), lambda i: (i, col))
    weight = lambda k: pl.BlockSpec((None, k, D_MODEL), lambda i: (l, 0, 0))
    g0 = COL_G // D_MODEL
    return pl.pallas_call(
        functools.partial(_merge_kernel, tm=tm),
        grid=(N_TOK // tm,),
        in_specs=[
            rows(D_MODEL), rows(D_HY), rows(D_FN), rows(D_RG), rows(D_RG, COL_RY // D_RG),
            rows(D_MODEL, g0), rows(D_MODEL, g0 + 1), rows(D_MODEL, g0 + 2),
            pl.BlockSpec((None, MOD_ROWS, 6 * D_MODEL), lambda i: (l, 0, 0)),
            weight(D_HY), weight(D_FN), weight(D_RG), weight(D_MODEL),
        ],
        out_specs=rows(D_MODEL),
        out_shape=jax.ShapeDtypeStruct((N_TOK, D_MODEL), F32),
        compiler_params=_params("arbitrary"),
        name="branch_merge",
    )(x, y_hy, y_fn, h_rg, u, u, u, u, mod, w_a, w_b, w_c, w_o)


def _ffn_kernel(x_ref, mod_ref, g_ref, wg_ref, wu_ref, wd_ref, fg_ref, o_ref, h_scr, *, tm, final_norm):
    i = pl.program_id(0)
    f = pl.program_id(1)
    sub = tm // CHUNK

    @pl.when(f == 0)
    def _():
        for s in range(sub):
            r = _mod_row(i * sub + s)
            shift = mod_ref[pl.ds(r, 1), 3 * D_MODEL:4 * D_MODEL]
            scale = mod_ref[pl.ds(r, 1), 4 * D_MODEL:5 * D_MODEL]
            rows = slice(s * CHUNK, (s + 1) * CHUNK)
            h = _rmsnorm(x_ref[rows, :], g_ref[...]) * (1.0 + scale) + shift
            h_scr[rows, :] = h.astype(BF16)
        o_ref[...] = jnp.zeros(o_ref.shape, F32)

    h = h_scr[...]
    gt = jnp.dot(h, wg_ref[...].astype(BF16), preferred_element_type=F32)
    up = jnp.dot(h, wu_ref[...].astype(BF16), preferred_element_type=F32)
    act = (gt * _sigmoid(gt)) * up
    o_ref[...] += jnp.dot(act.astype(BF16), wd_ref[...].astype(BF16), preferred_element_type=F32)

    @pl.when(f == pl.num_programs(1) - 1)
    def _():
        for s in range(sub):
            r = _mod_row(i * sub + s)
            gate = mod_ref[pl.ds(r, 1), 5 * D_MODEL:6 * D_MODEL]
            rows = slice(s * CHUNK, (s + 1) * CHUNK)
            out = x_ref[rows, :] + gate * o_ref[rows, :]
            if final_norm:
                out = _rmsnorm(out, fg_ref[...])
            o_ref[rows, :] = out


def _ffn(x, mod, norm_g, w_gu, w_down, final_g, l, final_norm, tm=2048, tf=256):
    nf = D_FF // tf
    return pl.pallas_call(
        functools.partial(_ffn_kernel, tm=tm, final_norm=final_norm),
        grid=(N_TOK // tm, nf),
        in_specs=[
            pl.BlockSpec((tm, D_MODEL), lambda i, f: (i, 0)),
            pl.BlockSpec((None, MOD_ROWS, 6 * D_MODEL), lambda i, f: (l, 0, 0)),
            pl.BlockSpec((None, 1, D_MODEL), lambda i, f: (l, 0, 0)),
            pl.BlockSpec((None, D_MODEL, tf), lambda i, f: (l, 0, f)),
            pl.BlockSpec((None, D_MODEL, tf), lambda i, f: (l, 0, nf + f)),
            pl.BlockSpec((None, tf, D_MODEL), lambda i, f: (l, f, 0)),
            pl.BlockSpec((1, D_MODEL), lambda i, f: (0, 0)),
        ],
        out_specs=pl.BlockSpec((tm, D_MODEL), lambda i, f: (i, 0)),
        out_shape=jax.ShapeDtypeStruct((N_TOK, D_MODEL), F32),
        scratch_shapes=[pltpu.VMEM((tm, D_MODEL), BF16)],
        compiler_params=_params("arbitrary", "arbitrary"),
        name="swiglu_ffn",
    )(x, mod, norm_g, w_gu, w_gu, w_down, final_g)


def kernel(x_prompt, x_sample, c, state_rglru, c_ctx, norm1_g, norm2_g, w_ada, b_ada, w_in,
           hy_conv_w, hy_conv_b, hy_f_w1, hy_f_b1, hy_f_w2, hy_f_b2, hy_f_w3, hy_f_freq, hy_bias,
           w_a, w_b, rg_conv_w, rg_conv_b, rg_wr, rg_br, rg_wi, rg_bi, rg_lam, w_c, w_o,
           w_gu, w_down, final_g):
    x = jnp.concatenate([_to_time_permuted(x_prompt, SEQ), _to_time_permuted(x_sample, DEC_SEQ)])
    cvec = jnp.concatenate([c_ctx[None, :], c, jnp.zeros((MOD_ROWS - 1 - DEC_BATCH, D_MODEL), F32)])
    mod = _modulation(cvec, w_ada, b_ada)

    dft_short = _as_bf16(*_hyena_dft_tables(SEQ))
    dft_long = _as_bf16(*_hyena_dft_tables(DEC_SEQ))
    fnet_short, fnet_chan = _as_bf16(*_fnet_tables(SEQ))
    fnet_long, = _as_bf16(_fnet_tables(DEC_SEQ)[0])
    filt = (hy_f_w1, hy_f_b1, hy_f_w2, hy_f_b2, hy_f_w3, hy_f_freq)
    p_short = _hyena_filters(SEQ, dft_short[0], *filt)
    p_long = _hyena_filters(DEC_SEQ, dft_long[0], *filt)

    w_gates = 0.5 * jnp.concatenate([rg_wr[:, 0], rg_wi[:, 0], rg_wr[:, 1], rg_wi[:, 1]], axis=-1)
    lat_h0 = jnp.pad(state_rglru.astype(F32).transpose(1, 0, 2, 3)[:, :, None],
                     ((0, 0), (0, 0), (0, CTX_PER_CHUNK - 1), (0, 0), (0, 0)))
    h0_all = jnp.concatenate([jnp.zeros((DEPTH, NQ_CTX, CTX_PER_CHUNK, 2, D_RG), F32), lat_h0], axis=1)
    norm1 = norm1_g.reshape(DEPTH, 1, D_MODEL)
    norm2 = norm2_g.reshape(DEPTH, 1, D_MODEL)
    final = final_g.reshape(1, D_MODEL)

    states = []
    for l in range(DEPTH):
        u = _input_projection(x, mod, norm1, w_in, l)
        y_hy = _hyena(u, hy_conv_w, hy_conv_b, hy_bias, p_short, p_long, dft_short, dft_long, l)
        y_fn = _fnet(u, fnet_chan, fnet_short, fnet_long)
        h_rg, st = _rglru(u, h0_all[l], rg_conv_w, rg_conv_b, w_gates, rg_br, rg_bi, rg_lam, l)
        states.append(st[:NQ_CTX].reshape(BATCH, 2, D_RG))
        x = _merge(x, y_hy, y_fn, h_rg, u, mod, w_a, w_b, w_c, w_o, l)
        x = _ffn(x, mod, norm2, w_gu, w_down, final, l, final_norm=(l == DEPTH - 1))

    y_prompt = _from_time_permuted(x[:N_CTX_TOK], BATCH, SEQ)
    y_sample = _from_time_permuted(x[N_CTX_TOK:], DEC_BATCH, DEC_SEQ)
    new_state = jnp.stack(states, axis=1).astype(x_prompt.dtype)
    return (y_prompt, y_sample, new_state)
```

```python
import functools
import math

import numpy as np
import jax
import jax.numpy as jnp
from jax import lax
from jax.experimental import pallas as pl
from jax.experimental.pallas import tpu as pltpu

F32 = jnp.float32
BF16 = jnp.bfloat16
HIGHEST = lax.Precision.HIGHEST

D_MODEL = 1024
BATCH = 16
SEQ = 256
DEPTH = 4
DEC_BATCH = 4
DEC_SEQ = 1024
EPS = 1e-6
TINY_F32 = float(np.finfo(np.float32).tiny)
D_HY = 512
N_BANDS = 8
FILT_EMB = 1 + 2 * N_BANDS
FILT_HID = 64
DECAY_SLOW = -math.log(1e-2) / 1.5
DECAY_FAST = -math.log(1e-2) / 0.3
D_FN = 512
FN_GROUP = 128
N_FN_GROUPS = D_FN // FN_GROUP
D_RG = 1024
RG_BLOCK = 128
RG_C = 8.0
D_FF = -(-8 * D_MODEL // (3 * 256)) * 256
D_IN = 3 * D_HY + D_FN + 2 * D_RG + 3 * D_MODEL
COL_FN = 3 * D_HY
COL_RX = COL_FN + D_FN
COL_RY = COL_RX + D_RG
COL_G = COL_RY + D_RG

CHUNK = 1024
N_CTX_TOK = BATCH * SEQ
N_LAT_TOK = DEC_BATCH * DEC_SEQ
N_TOK = N_CTX_TOK + N_LAT_TOK
NQ_CTX = N_CTX_TOK // CHUNK
NQ = N_TOK // CHUNK
CTX_PER_CHUNK = CHUNK // SEQ
MOD_ROWS = 8
FEAT_PAD = 128
HY_TC = 256
RG_TC = 512
K_PIECE = 1024
GATE_ROWS = 256
VMEM_LIMIT = 56 * 1024 * 1024

assert DEC_SEQ == CHUNK and CHUNK % SEQ == 0 and N_CTX_TOK % CHUNK == 0
assert SEQ % 16 == 0 and DEC_SEQ % 16 == 0
assert 1 + DEC_BATCH <= MOD_ROWS


def _params(*sem):
    return pltpu.CompilerParams(dimension_semantics=sem, vmem_limit_bytes=VMEM_LIMIT)


def _mod_row(q):
    return jnp.maximum(q - (NQ_CTX - 1), 0)


def _rmsnorm(x, g):
    return x * lax.rsqrt(jnp.mean(x * x, axis=-1, keepdims=True) + EPS) * g


def _sigmoid(x):
    return 0.5 * jnp.tanh(0.5 * x) + 0.5


def _gelu_tanh(x):
    inner = x * (math.sqrt(2.0 / math.pi) * 0.044715 * (x * x) + math.sqrt(2.0 / math.pi))
    half = 0.5 * x
    return half * jnp.tanh(inner) + half


def _time_of_row(L):
    p = np.arange(L)
    return (p % 8) * (L // 8) + p // 8


def _to_time_permuted(x, L):
    b = x.shape[0]
    return x.reshape(b, 8, L // 8, x.shape[-1]).transpose(0, 2, 1, 3).reshape(b * L, x.shape[-1])


def _from_time_permuted(x, b, L):
    return x.reshape(b, L // 8, 8, x.shape[-1]).transpose(0, 2, 1, 3).reshape(b, L, x.shape[-1])


def _shift_time(x, k, L):
    sub = lax.broadcasted_iota(jnp.int32, (8, 1), 0)
    out = []
    for s0 in range(0, x.shape[0], L):
        xs = x[s0:s0 + L]
        if k > 0:
            wrap = [jnp.where(sub == 0, 0.0, pltpu.roll(xs[L - 8 * (k - i):L - 8 * (k - i - 1)], 1, axis=0))
                    for i in range(k)]
            out += wrap + [xs[:L - 8 * k]]
        else:
            wrap = [jnp.where(sub == 7, 0.0, pltpu.roll(xs[8 * i:8 * (i + 1)], 7, axis=0))
                    for i in range(-k)]
            out += [xs[-8 * k:]] + wrap
    return jnp.concatenate(out, axis=0)


def _as_bf16(*tables):
    return tuple(jnp.asarray(t, dtype=F32).astype(BF16) for t in tables)


def _angle_table(n_rows, n_cols, period):
    prod = np.outer(np.arange(n_rows, dtype=np.int64), np.arange(n_cols, dtype=np.int64)) % period
    return 2.0 * np.pi * prod.astype(np.float64) / period


@functools.lru_cache(maxsize=None)
def _hyena_dft_tables(L):
    n = 2 * L
    ang = _angle_table(L, L, n)
    alt = np.where(np.arange(L) % 2 == 0, 1.0, -1.0)
    cos_f, sin_f = np.cos(ang), np.sin(ang)
    sin_f[0, :] = alt
    fwd = np.concatenate([cos_f, sin_f], axis=0)
    cos_i, sin_i = 2.0 / n * np.cos(ang), 2.0 / n * np.sin(ang)
    cos_i[:, 0] = 1.0 / n
    sin_i[:, 0] = alt / n
    inv = np.concatenate([cos_i, sin_i], axis=1)
    perm = _time_of_row(L)
    return fwd[:, perm].astype(np.float32), inv[perm, :].astype(np.float32)


@functools.lru_cache(maxsize=None)
def _fnet_tables(L):
    perm = _time_of_row(L)
    ang = _angle_table(L, L, L)[perm][:, perm]
    seq = np.concatenate([np.cos(ang), -np.sin(ang)], axis=1)
    ang_c = _angle_table(FN_GROUP, FN_GROUP, FN_GROUP)
    chan = np.concatenate([np.cos(ang_c), np.sin(ang_c)], axis=1)
    return seq.astype(np.float32), chan.astype(np.float32)


@functools.lru_cache(maxsize=None)
def _filter_tables(L):
    t = np.arange(L, dtype=np.float32) / np.float32(L)
    ang = 2.0 * np.pi * t[:, None].astype(np.float64) * np.arange(1, N_BANDS + 1, dtype=np.float64)
    feats = np.zeros((L, FEAT_PAD), np.float64)
    feats[:, 0] = t
    feats[:, 1:1 + N_BANDS] = np.sin(ang)
    feats[:, 1 + N_BANDS:FILT_EMB] = np.cos(ang)
    deltas = np.linspace(DECAY_SLOW, DECAY_FAST, D_HY, dtype=np.float32).astype(np.float64)
    decay = np.exp(-t[:, None].astype(np.float64) * deltas)
    perm = _time_of_row(L)
    return jnp.asarray(feats[perm], dtype=F32), jnp.asarray(decay[perm], dtype=F32)


def _mod_kernel(c_ref, w_ref, b_ref, o_ref):
    cv = c_ref[...]
    act = cv * _sigmoid(cv)
    o_ref[...] = jnp.dot(act, w_ref[...], precision=HIGHEST, preferred_element_type=F32) + b_ref[...]


def _modulation(cvec, w_ada, b_ada):
    tn = 1536
    return pl.pallas_call(
        _mod_kernel,
        grid=(DEPTH, 6 * D_MODEL // tn),
        in_specs=[
            pl.BlockSpec((MOD_ROWS, D_MODEL), lambda l, j: (0, 0)),
            pl.BlockSpec((None, D_MODEL, tn), lambda l, j: (l, 0, j)),
            pl.BlockSpec((None, 1, tn), lambda l, j: (l, 0, j)),
        ],
        out_specs=pl.BlockSpec((None, MOD_ROWS, tn), lambda l, j: (l, 0, j)),
        out_shape=jax.ShapeDtypeStruct((DEPTH, MOD_ROWS, 6 * D_MODEL), F32),
        compiler_params=_params("arbitrary", "arbitrary"),
        name="adaln_modulation",
    )(cvec, w_ada, b_ada.reshape(DEPTH, 1, 6 * D_MODEL))


def _filter_kernel(feats_ref, decay_ref, wf_ref, w1_ref, b1_ref, w2_ref, b2_ref, w3_ref, fq_ref,
                   p_ref, nyq_ref, hid_scr, *, L):
    @pl.when(pl.program_id(1) == 0)
    def _():
        freq = fq_ref[...]
        h = jnp.sin(freq * (jnp.dot(feats_ref[...], w1_ref[...], precision=HIGHEST,
                                    preferred_element_type=F32) + b1_ref[...]))
        hid_scr[...] = jnp.sin(freq * (jnp.dot(h, w2_ref[...], precision=HIGHEST,
                                               preferred_element_type=F32) + b2_ref[...]))

    h = jnp.dot(hid_scr[...], w3_ref[...], precision=HIGHEST, preferred_element_type=F32)
    decay = decay_ref[...]
    row = lax.broadcasted_iota(jnp.int32, (L, 1), 0)
    h_fwd = h[:, :D_HY] * decay
    h_bwd = jnp.where(row == 0, 0.0, h[:, D_HY:] * decay)
    ssq = jnp.sum(h_fwd * h_fwd, axis=0, keepdims=True) + jnp.sum(h_bwd * h_bwd, axis=0, keepdims=True)
    scale = lax.rsqrt(ssq + EPS)
    even = (h_fwd + h_bwd) * scale
    odd = (h_bwd - h_fwd) * scale
    k_re = jnp.dot(wf_ref[0:L, :], even.astype(BF16), preferred_element_type=F32)
    k_im = jnp.dot(wf_ref[L:2 * L, :], odd.astype(BF16), preferred_element_type=F32)
    alt = jnp.where(((row >> 3) & 1) == 0, 1.0, -1.0)
    k_nyq = jnp.sum(even * alt, axis=0, keepdims=True)
    p_ref[0] = k_re
    p_ref[1] = jnp.where(row == 0, 0.0, k_im)
    nyq_ref[...] = k_nyq


def _hyena_filters(L, wf, fw1, fb1, fw2, fb2, fw3, ffreq):
    feats, decay = _filter_tables(L)
    w1 = jnp.pad(fw1, ((0, 0), (0, FEAT_PAD - FILT_EMB), (0, 0)))
    const = lambda shape: pl.BlockSpec(shape, lambda l, o: (0,) * len(shape))
    per_layer = lambda *shape: pl.BlockSpec((None,) + shape, lambda l, o: (l,) + (0,) * len(shape))
    return pl.pallas_call(
        functools.partial(_filter_kernel, L=L),
        grid=(DEPTH, 2),
        in_specs=[
            const((L, FEAT_PAD)), const((L, D_HY)), const((2 * L, L)),
            per_layer(FEAT_PAD, FILT_HID), per_layer(1, FILT_HID),
            per_layer(FILT_HID, FILT_HID), per_layer(1, FILT_HID),
            pl.BlockSpec((None, FILT_HID, 2 * D_HY), lambda l, o: (l, 0, o)),
            per_layer(1, FILT_HID),
        ],
        out_specs=[pl.BlockSpec((None, None, 2, L, D_HY), lambda l, o: (l, o, 0, 0, 0)),
                   pl.BlockSpec((None, None, 1, D_HY), lambda l, o: (l, o, 0, 0))],
        out_shape=[jax.ShapeDtypeStruct((DEPTH, 2, 2, L, D_HY), F32),
                   jax.ShapeDtypeStruct((DEPTH, 2, 1, D_HY), F32)],
        scratch_shapes=[pltpu.VMEM((L, FILT_HID), F32)],
        compiler_params=_params("arbitrary", "arbitrary"),
        name=f"hyena_filters_{L}",
    )(feats, decay, wf, w1, fb1.reshape(DEPTH, 1, FILT_HID), fw2, fb2.reshape(DEPTH, 1, FILT_HID),
      fw3, ffreq.reshape(DEPTH, 1, FILT_HID))


def _inproj_kernel(x_ref, mod_ref, g_ref, w_ref, u_ref, h_scr, *, tm):
    i = pl.program_id(0)

    @pl.when(pl.program_id(1) == 0)
    def _():
        for s in range(tm // CHUNK):
            r = _mod_row(i * (tm // CHUNK) + s)
            shift = mod_ref[pl.ds(r, 1), 0:D_MODEL]
            scale = mod_ref[pl.ds(r, 1), D_MODEL:2 * D_MODEL]
            rows = slice(s * CHUNK, (s + 1) * CHUNK)
            h = _rmsnorm(x_ref[rows, :], g_ref[...]) * (1.0 + scale) + shift
            h_scr[rows, :] = h.astype(BF16)

    u_ref[...] = jnp.dot(h_scr[...], w_ref[...].astype(BF16), preferred_element_type=F32).astype(BF16)


def _input_projection(x, mod, norm_g, w_in, l, tm=2048, tn=1024):
    return pl.pallas_call(
        functools.partial(_inproj_kernel, tm=tm),
        grid=(N_TOK // tm, D_IN // tn),
        in_specs=[
            pl.BlockSpec((tm, D_MODEL), lambda i, j: (i, 0)),
            pl.BlockSpec((None, MOD_ROWS, 6 * D_MODEL), lambda i, j: (l, 0, 0)),
            pl.BlockSpec((None, 1, D_MODEL), lambda i, j: (l, 0, 0)),
            pl.BlockSpec((None, D_MODEL, tn), lambda i, j: (l, 0, j)),
        ],
        out_specs=pl.BlockSpec((tm, tn), lambda i, j: (i, j)),
        out_shape=jax.ShapeDtypeStruct((N_TOK, D_IN), BF16),
        scratch_shapes=[pltpu.VMEM((tm, D_MODEL), BF16)],
        compiler_params=_params("arbitrary", "arbitrary"),
        name="input_projection",
    )(x, mod, norm_g, w_in)


def _interleave(*task_lists):
    steps = max(len(tasks) for tasks in task_lists)
    done = [0] * len(task_lists)
    for step in range(1, steps + 1):
        for i, tasks in enumerate(task_lists):
            while done[i] < (step * len(tasks)) // steps:
                tasks[done[i]]()
                done[i] += 1


def _dot_by_k_tiles(lhs_ref, rhs_of, out):
    k_tile = min(K_PIECE, lhs_ref.shape[1])

    def piece(k):
        def run():
            cols = slice(k * k_tile, (k + 1) * k_tile)
            part = jnp.dot(lhs_ref[:, cols], rhs_of()[cols], preferred_element_type=F32)
            out["acc"] = part if k == 0 else out["acc"] + part
        return run
    return [piece(k) for k in range(lhs_ref.shape[1] // k_tile)]


def _spectrum_product(spec, p_ref, nyq_ref, order, L):
    z_re, z_sn = spec[:L], spec[L:]
    p_re, p_im = p_ref[order, 0], p_ref[order, 1]
    y_re = z_re * p_re + z_sn * p_im
    y_sn = z_sn * p_re - z_re * p_im
    first = lax.broadcasted_iota(jnp.int32, (8, 1), 0) == 0
    y_sn_top = jnp.where(first, z_sn[:8] * nyq_ref[order], y_sn[:8])
    return jnp.concatenate([y_re, y_sn_top, y_sn[8:]], axis=0).astype(BF16)


def _hyena_tasks(L, v_ref, x1_ref, x2_ref, wv_ref, w1_ref, w2_ref, bv_ref, b1_ref, b2_ref, hb_ref,
                 p_ref, nyq_ref, wf_ref, wi_ref, o_ref):
    chunk = {}

    def prepare():
        def short_conv(u_ref, w_ref, b_ref):
            u = u_ref[...].astype(F32)
            w = w_ref[...]
            return (b_ref[...] + _shift_time(u, 1, L) * w[0:1] + u * w[1:2]
                    + _shift_time(u, -1, L) * w[2:3])
        chunk["v"] = short_conv(v_ref, wv_ref, bv_ref)
        chunk["x1"] = short_conv(x1_ref, w1_ref, b1_ref)
        chunk["x2"] = short_conv(x2_ref, w2_ref, b2_ref)
        chunk["bias"] = hb_ref[...]

    tasks = [prepare]
    for s in range(CHUNK // L):
        rows = slice(s * L, (s + 1) * L)
        seq = {}

        def begin(seq=seq, rows=rows):
            seq["in0"] = chunk["v"][rows]
            seq["in0_bf16"] = seq["in0"].astype(BF16)

        tasks.append(begin)
        for order in (0, 1):
            spec, conv = {}, {}
            tasks += _dot_by_k_tiles(wf_ref, lambda seq=seq, order=order: seq[f"in{order}_bf16"], spec)

            def pointwise(seq=seq, spec=spec, order=order):
                seq[f"y{order}"] = _spectrum_product(spec["acc"], p_ref, nyq_ref, order, L)

            tasks.append(pointwise)
            tasks += _dot_by_k_tiles(wi_ref, lambda seq=seq, order=order: seq[f"y{order}"], conv)

            def gate(seq=seq, conv=conv, order=order, rows=rows):
                x = chunk["x1" if order == 0 else "x2"][rows]
                z = seq[f"in{order}"]
                out = x * (conv["acc"] + chunk["bias"][order:order + 1] * z)
                if order == 0:
                    seq["in1"] = out
                    seq["in1_bf16"] = out.astype(BF16)
                else:
                    o_ref[rows, :] = out.astype(BF16)

            tasks.append(gate)
    return tasks


def _fnet_tasks(L, u_ref, cs_ref, f_ref, o_ref):
    chunk = {}

    def prepare():
        x = u_ref[...].astype(BF16)
        parts_c, parts_s = [], []
        for g in range(x.shape[1] // FN_GROUP):
            r = jnp.dot(x[:, g * FN_GROUP:(g + 1) * FN_GROUP], cs_ref[...], preferred_element_type=F32)
            parts_c.append(r[:, :FN_GROUP])
            parts_s.append(r[:, FN_GROUP:])
        chunk["xc"] = jnp.concatenate(parts_c, axis=1).astype(BF16)
        chunk["xs"] = jnp.concatenate(parts_s, axis=1).astype(BF16)

    tasks = [prepare]
    norm = 1.0 / math.sqrt(L * FN_GROUP)
    for s in range(CHUNK // L):
        rows = slice(s * L, (s + 1) * L)
        seq, out = {}, {}

        def stack(seq=seq, rows=rows):
            seq["stacked"] = jnp.concatenate([chunk["xc"][rows], chunk["xs"][rows]], axis=0)

        def finish(out=out, rows=rows):
            o_ref[rows, :] = (out["acc"] * norm).astype(BF16)

        tasks += [stack] + _dot_by_k_tiles(f_ref, lambda seq=seq: seq["stacked"], out) + [finish]
    return tasks


def _scan_sequence(s, L, a_f, b_f, a_b, b_b, h0_ref, y_ref, st_ref):
    nb = L // 8
    tc = a_f.shape[1]
    s0 = s * L

    def local(j, carry):
        hf, pf, hb, pb = carry
        rf = pl.multiple_of(s0 + 8 * j, 8)
        rb = pl.multiple_of(s0 + L - 8 - 8 * j, 8)
        af = a_f[pl.ds(rf, 8), :]
        hf = af * hf + b_f[pl.ds(rf, 8), :]
        pf = af * pf
        b_f[pl.ds(rf, 8), :] = hf
        a_f[pl.ds(rf, 8), :] = pf
        ab = a_b[pl.ds(rb, 8), :]
        hb = ab * hb + b_b[pl.ds(rb, 8), :]
        pb = ab * pb
        b_b[pl.ds(rb, 8), :] = hb
        a_b[pl.ds(rb, 8), :] = pb
        return hf, pf, hb, pb

    zero = jnp.zeros((8, tc), F32)
    one = jnp.ones((8, tc), F32)
    hf, pf, hb, pb = lax.fori_loop(0, nb, local, (zero, one, zero, one), unroll=4)

    carry = h0_ref[s, 0:1, :]
    rows = []
    for k in range(8):
        rows.append(carry)
        carry = hf[k:k + 1] + pf[k:k + 1] * carry
    carry_f = jnp.concatenate(rows, axis=0)
    st_ref[s, 0:1, :] = carry
    carry = h0_ref[s, 1:2, :]
    rows = [None] * 8
    for k in range(7, -1, -1):
        rows[k] = carry
        carry = hb[k:k + 1] + pb[k:k + 1] * carry
    carry_b = jnp.concatenate(rows, axis=0)
    st_ref[s, 1:2, :] = carry

    sl = slice(s0, s0 + L)
    blocked = lambda ref: ref[sl, :].reshape(nb, 8, tc)
    h_sum = ((blocked(b_f) + blocked(a_f) * carry_f[None]) + (blocked(b_b) + blocked(a_b) * carry_b[None]))
    y_ref[sl, :] = h_sum.reshape(L, tc).astype(BF16)


def _rglru_gate_tasks(L, reset, ux_ref, cw_ref, cb_ref, wg_ref, br_ref, bi_ref, lam_ref,
                      a_f, b_f, a_b, b_b):
    chunk = {}

    def prepare():
        u = ux_ref[...].astype(F32)
        w = cw_ref[...]
        xr = (cb_ref[...] + _shift_time(u, 2, L) * w[0:1] + _shift_time(u, 1, L) * w[1:2]
              + u * w[2:3] + _shift_time(u, -1, L) * w[3:4])
        chunk["x"] = xr
        chunk["x_bf16"] = xr.astype(BF16)
        neg_lam = -lam_ref[...]
        softplus = jnp.maximum(neg_lam, 0.0) + jnp.log1p(jnp.exp(-jnp.abs(neg_lam)))
        chunk["rate"] = (-0.25 * RG_C) * softplus
        chunk["half_br"] = 0.5 * br_ref[...]
        chunk["half_bi"] = 0.5 * bi_ref[...]

    tasks = [prepare]
    for n in range(RG_TC // RG_BLOCK):
        cs = slice(n * RG_BLOCK, (n + 1) * RG_BLOCK)

        def project(n=n, cs=cs):
            chunk["gates", n] = jnp.dot(chunk["x_bf16"][:, cs], wg_ref[n].astype(BF16),
                                        preferred_element_type=F32)

        tasks.append(project)
        for d, (a_scr, b_scr) in enumerate(((a_f, b_f), (a_b, b_b))):
            for r0 in range(0, CHUNK, GATE_ROWS):
                def gate(n=n, cs=cs, d=d, a_scr=a_scr, b_scr=b_scr, r0=r0):
                    rows = slice(r0, r0 + GATE_ROWS)
                    gates = chunk["gates", n][rows]
                    c0 = 2 * d * RG_BLOCK
                    t_r = jnp.tanh(gates[:, c0:c0 + RG_BLOCK] + chunk["half_br"][d:d + 1, cs])
                    t_i = jnp.tanh(gates[:, c0 + RG_BLOCK:c0 + 2 * RG_BLOCK] + chunk["half_bi"][d:d + 1, cs])
                    rate = chunk["rate"][d:d + 1, cs]
                    th = jnp.tanh(rate * t_r + rate)
                    recip = 1.0 / (1.0 - th)
                    neg_th = -th
                    root = neg_th * lax.rsqrt(jnp.maximum(neg_th, TINY_F32))
                    half_mult = recip * root
                    if reset:
                        tpos = (r0 + lax.broadcasted_iota(jnp.int32, (GATE_ROWS, 1), 0)) & (L - 1)
                        half_mult = jnp.where(tpos == (0 if d == 0 else L - 1), 0.5, half_mult)
                    a_scr[rows, cs] = (1.0 + th) * recip
                    b_scr[rows, cs] = (half_mult * chunk["x"][rows, cs]) * (t_i + 1.0)

                tasks.append(gate)
    return tasks


def _mixers_kernel(hv_ref, hx1_ref, hx2_ref, fn_ref, ux_ref,
                   hwv_ref, hw1_ref, hw2_ref, hbv_ref, hb1_ref, hb2_ref, hbias_ref,
                   ps_ref, ns_ref, pl_ref, nl_ref, wfs_ref, wis_ref, wfl_ref, wil_ref,
                   cs_ref, fs_ref, fl_ref,
                   h0_ref, cw_ref, cb_ref, wg_ref, br_ref, bi_ref, lam_ref,
                   yh_ref, yf_ref, y_ref, st_ref, a_f, b_f, a_b, b_b):
    q = pl.program_id(1)

    def run(L, reset, p_ref, nyq_ref, wf_ref, wi_ref, f_ref):
        matmul_side = (_hyena_tasks(L, hv_ref, hx1_ref, hx2_ref, hwv_ref, hw1_ref, hw2_ref, hbv_ref,
                                    hb1_ref, hb2_ref, hbias_ref, p_ref, nyq_ref, wf_ref, wi_ref, yh_ref)
                       + _fnet_tasks(L, fn_ref, cs_ref, f_ref, yf_ref))
        vector_side = _rglru_gate_tasks(L, reset, ux_ref, cw_ref, cb_ref, wg_ref, br_ref, bi_ref,
                                        lam_ref, a_f, b_f, a_b, b_b)
        _interleave(matmul_side, vector_side)
        st_ref[...] = jnp.zeros(st_ref.shape, F32)
        for s in range(CHUNK // L):
            _scan_sequence(s, L, a_f, b_f, a_b, b_b, h0_ref, y_ref, st_ref)

    @pl.when(q < NQ_CTX)
    def _():
        run(SEQ, True, ps_ref, ns_ref, wfs_ref, wis_ref, fs_ref)

    @pl.when(q >= NQ_CTX)
    def _():
        run(DEC_SEQ, False, pl_ref, nl_ref, wfl_ref, wil_ref, fl_ref)


def _mixers(u, l, hy_conv_w, hy_conv_b, hy_bias, spectra_short, spectra_long, dft_short, dft_long,
            fnet_chan, fnet_short, fnet_long, h0, rg_conv_w, rg_conv_b, w_gates, b_r, b_i, lam):
    assert D_HY // HY_TC == D_FN // HY_TC == D_RG // RG_TC
    halves = D_HY // HY_TC
    p_short, nyq_short = spectra_short
    p_long, nyq_long = spectra_long
    ucol = lambda width, col0: pl.BlockSpec((CHUNK, width), lambda c, q: (q, col0 // width + c))
    hy_part = lambda rows, k: pl.BlockSpec((None, rows, HY_TC), lambda c, q: (l, 0, k * halves + c))
    hy_tile = lambda rows: pl.BlockSpec((None, rows, HY_TC), lambda c, q: (l, 0, c))
    rg_tile = lambda rows: pl.BlockSpec((None, rows, RG_TC), lambda c, q: (l, 0, c))
    const = lambda shape: pl.BlockSpec(shape, lambda c, q: (0, 0))
    planes = lambda L: pl.BlockSpec((None, 2, 2, L, HY_TC), lambda c, q: (l, 0, 0, 0, c))
    nyquist = pl.BlockSpec((None, 2, 1, HY_TC), lambda c, q: (l, 0, 0, c))
    state_spec = pl.BlockSpec((None, CTX_PER_CHUNK, 2, RG_TC), lambda c, q: (q, 0, 0, c))
    out_tile = lambda width: pl.BlockSpec((CHUNK, width), lambda c, q: (q, c))
    hy_conv_b = hy_conv_b.reshape(DEPTH, 1, 3 * D_HY)
    return pl.pallas_call(
        _mixers_kernel,
        grid=(halves, NQ),
        in_specs=[
            ucol(HY_TC, 0), ucol(HY_TC, D_HY), ucol(HY_TC, 2 * D_HY), ucol(HY_TC, COL_FN),
            ucol(RG_TC, COL_RX),
            hy_part(3, 0), hy_part(3, 1), hy_part(3, 2), hy_part(1, 0), hy_part(1, 1), hy_part(1, 2),
            hy_tile(2),
            planes(SEQ), nyquist, planes(DEC_SEQ), nyquist,
            const((2 * SEQ, SEQ)), const((SEQ, 2 * SEQ)),
            const((2 * DEC_SEQ, DEC_SEQ)), const((DEC_SEQ, 2 * DEC_SEQ)),
            const((FN_GROUP, 2 * FN_GROUP)), const((SEQ, 2 * SEQ)), const((DEC_SEQ, 2 * DEC_SEQ)),
            state_spec, rg_tile(4), rg_tile(1),
            pl.BlockSpec((None, RG_TC // RG_BLOCK, RG_BLOCK, 4 * RG_BLOCK), lambda c, q: (l, c, 0, 0)),
            rg_tile(2), rg_tile(2), rg_tile(2),
        ],
        out_specs=[out_tile(HY_TC), out_tile(HY_TC), out_tile(RG_TC), state_spec],
        out_shape=[jax.ShapeDtypeStruct((N_TOK, D_HY), BF16),
                   jax.ShapeDtypeStruct((N_TOK, D_FN), BF16),
                   jax.ShapeDtypeStruct((N_TOK, D_RG), BF16),
                   jax.ShapeDtypeStruct((NQ, CTX_PER_CHUNK, 2, D_RG), F32)],
        scratch_shapes=[pltpu.VMEM((CHUNK, RG_TC), F32)] * 4,
        compiler_params=_params("arbitrary", "arbitrary"),
        name="sequence_mixers",
    )(u, u, u, u, u, hy_conv_w, hy_conv_w, hy_conv_w, hy_conv_b, hy_conv_b, hy_conv_b, hy_bias,
      p_short, nyq_short, p_long, nyq_long, *dft_short, *dft_long, fnet_chan, fnet_short, fnet_long,
      h0, rg_conv_w, rg_conv_b.reshape(DEPTH, 1, D_RG), w_gates, b_r, b_i, lam)


def _merge_kernel(x_ref, ya_ref, yb_ref, hc_ref, uy_ref, ga_ref, gb_ref, gc_ref, mod_ref,
                  wa_ref, wb_ref, wc_ref, wo_ref, o_ref, *, tm):
    r = _mod_row(pl.program_id(0) // (CHUNK // tm))
    gate = mod_ref[pl.ds(r, 1), 2 * D_MODEL:3 * D_MODEL]
    project = lambda y, w_ref: jnp.dot(y, w_ref[...].astype(BF16), preferred_element_type=F32)
    ya = project(ya_ref[...], wa_ref)
    yb = project(yb_ref[...], wb_ref)
    rg = hc_ref[...].astype(F32) * _gelu_tanh(uy_ref[...].astype(F32))
    yc = project(rg.astype(BF16), wc_ref)
    ga, gb, gc = (_sigmoid(g_ref[...].astype(F32)) for g_ref in (ga_ref, gb_ref, gc_ref))
    mix = ga * ya + gb * yb + gc * yc
    o_ref[...] = x_ref[...] + gate * project(mix.astype(BF16), wo_ref)


def _merge(x, y_hy, y_fn, h_rg, u, mod, w_a, w_b, w_c, w_o, l, tm=512):
    rows = lambda width, col=0: pl.BlockSpec((tm, width), lambda i: (i, col))
    weight = lambda k: pl.BlockSpec((None, k, D_MODEL), lambda i: (l, 0, 0))
    g0 = COL_G // D_MODEL
    return pl.pallas_call(
        functools.partial(_merge_kernel, tm=tm),
        grid=(N_TOK // tm,),
        in_specs=[
            rows(D_MODEL), rows(D_HY), rows(D_FN), rows(D_RG), rows(D_RG, COL_RY // D_RG),
            rows(D_MODEL, g0), rows(D_MODEL, g0 + 1), rows(D_MODEL, g0 + 2),
            pl.BlockSpec((None, MOD_ROWS, 6 * D_MODEL), lambda i: (l, 0, 0)),
            weight(D_HY), weight(D_FN), weight(D_RG), weight(D_MODEL),
        ],
        out_specs=rows(D_MODEL),
        out_shape=jax.ShapeDtypeStruct((N_TOK, D_MODEL), F32),
        compiler_params=_params("arbitrary"),
        name="branch_merge",
    )(x, y_hy, y_fn, h_rg, u, u, u, u, mod, w_a, w_b, w_c, w_o)


def _ffn_kernel(x_ref, mod_ref, g_ref, wg_ref, wu_ref, wd_ref, fg_ref, o_ref, h_scr, *, tm, final_norm):
    i = pl.program_id(0)
    f = pl.program_id(1)
    sub = tm // CHUNK

    @pl.when(f == 0)
    def _():
        for s in range(sub):
            r = _mod_row(i * sub + s)
            shift = mod_ref[pl.ds(r, 1), 3 * D_MODEL:4 * D_MODEL]
            scale = mod_ref[pl.ds(r, 1), 4 * D_MODEL:5 * D_MODEL]
            rows = slice(s * CHUNK, (s + 1) * CHUNK)
            h = _rmsnorm(x_ref[rows, :], g_ref[...]) * (1.0 + scale) + shift
            h_scr[rows, :] = h.astype(BF16)
        o_ref[...] = jnp.zeros(o_ref.shape, F32)

    h = h_scr[...]
    gt = jnp.dot(h, wg_ref[...].astype(BF16), preferred_element_type=F32)
    up = jnp.dot(h, wu_ref[...].astype(BF16), preferred_element_type=F32)
    act = (gt * _sigmoid(gt)) * up
    o_ref[...] += jnp.dot(act.astype(BF16), wd_ref[...].astype(BF16), preferred_element_type=F32)

    @pl.when(f == pl.num_programs(1) - 1)
    def _():
        for s in range(sub):
            r = _mod_row(i * sub + s)
            gate = mod_ref[pl.ds(r, 1), 5 * D_MODEL:6 * D_MODEL]
            rows = slice(s * CHUNK, (s + 1) * CHUNK)
            out = x_ref[rows, :] + gate * o_ref[rows, :]
            if final_norm:
                out = _rmsnorm(out, fg_ref[...])
            o_ref[rows, :] = out


def _ffn(x, mod, norm_g, w_gu, w_down, final_g, l, final_norm, tm=2048, tf=256):
    nf = D_FF // tf
    return pl.pallas_call(
        functools.partial(_ffn_kernel, tm=tm, final_norm=final_norm),
        grid=(N_TOK // tm, nf),
        in_specs=[
            pl.BlockSpec((tm, D_MODEL), lambda i, f: (i, 0)),
            pl.BlockSpec((None, MOD_ROWS, 6 * D_MODEL), lambda i, f: (l, 0, 0)),
            pl.BlockSpec((None, 1, D_MODEL), lambda i, f: (l, 0, 0)),
            pl.BlockSpec((None, D_MODEL, tf), lambda i, f: (l, 0, f)),
            pl.BlockSpec((None, D_MODEL, tf), lambda i, f: (l, 0, nf + f)),
            pl.BlockSpec((None, tf, D_MODEL), lambda i, f: (l, f, 0)),
            pl.BlockSpec((1, D_MODEL), lambda i, f: (0, 0)),
        ],
        out_specs=pl.BlockSpec((tm, D_MODEL), lambda i, f: (i, 0)),
        out_shape=jax.ShapeDtypeStruct((N_TOK, D_MODEL), F32),
        scratch_shapes=[pltpu.VMEM((tm, D_MODEL), BF16)],
        compiler_params=_params("arbitrary", "arbitrary"),
        name="swiglu_ffn",
    )(x, mod, norm_g, w_gu, w_gu, w_down, final_g)


def kernel(x_prompt, x_sample, c, state_rglru, c_ctx, norm1_g, norm2_g, w_ada, b_ada, w_in,
           hy_conv_w, hy_conv_b, hy_f_w1, hy_f_b1, hy_f_w2, hy_f_b2, hy_f_w3, hy_f_freq, hy_bias,
           w_a, w_b, rg_conv_w, rg_conv_b, rg_wr, rg_br, rg_wi, rg_bi, rg_lam, w_c, w_o,
           w_gu, w_down, final_g):
    x = jnp.concatenate([_to_time_permuted(x_prompt, SEQ), _to_time_permuted(x_sample, DEC_SEQ)])
    cvec = jnp.concatenate([c_ctx[None, :], c, jnp.zeros((MOD_ROWS - 1 - DEC_BATCH, D_MODEL), F32)])
    mod = _modulation(cvec, w_ada, b_ada)

    dft_short = _as_bf16(*_hyena_dft_tables(SEQ))
    dft_long = _as_bf16(*_hyena_dft_tables(DEC_SEQ))
    fnet_short, fnet_chan = _as_bf16(*_fnet_tables(SEQ))
    fnet_long, = _as_bf16(_fnet_tables(DEC_SEQ)[0])
    filt = (hy_f_w1, hy_f_b1, hy_f_w2, hy_f_b2, hy_f_w3, hy_f_freq)
    p_short = _hyena_filters(SEQ, dft_short[0], *filt)
    p_long = _hyena_filters(DEC_SEQ, dft_long[0], *filt)

    w_gates = 0.5 * jnp.concatenate([rg_wr[:, 0], rg_wi[:, 0], rg_wr[:, 1], rg_wi[:, 1]], axis=-1)
    lat_h0 = jnp.pad(state_rglru.astype(F32).transpose(1, 0, 2, 3)[:, :, None],
                     ((0, 0), (0, 0), (0, CTX_PER_CHUNK - 1), (0, 0), (0, 0)))
    h0_all = jnp.concatenate([jnp.zeros((DEPTH, NQ_CTX, CTX_PER_CHUNK, 2, D_RG), F32), lat_h0], axis=1)
    norm1 = norm1_g.reshape(DEPTH, 1, D_MODEL)
    norm2 = norm2_g.reshape(DEPTH, 1, D_MODEL)
    final = final_g.reshape(1, D_MODEL)

    states = []
    for l in range(DEPTH):
        u = _input_projection(x, mod, norm1, w_in, l)
        y_hy, y_fn, h_rg, st = _mixers(
            u, l, hy_conv_w, hy_conv_b, hy_bias, p_short, p_long, dft_short, dft_long,
            fnet_chan, fnet_short, fnet_long, h0_all[l], rg_conv_w, rg_conv_b, w_gates,
            rg_br, rg_bi, rg_lam)
        states.append(st[:NQ_CTX].reshape(BATCH, 2, D_RG))
        x = _merge(x, y_hy, y_fn, h_rg, u, mod, w_a, w_b, w_c, w_o, l)
        x = _ffn(x, mod, norm2, w_gu, w_down, final, l, final_norm=(l == DEPTH - 1))

    y_prompt = _from_time_permuted(x[:N_CTX_TOK], BATCH, SEQ)
    y_sample = _from_time_permuted(x[N_CTX_TOK:], DEC_BATCH, DEC_SEQ)
    new_state = jnp.stack(states, axis=1).astype(x_prompt.dtype)
    return (y_prompt, y_sample, new_state)
```

```python
import functools
import math

import numpy as np
import jax
import jax.numpy as jnp
from jax import lax
from jax.experimental import pallas as pl
from jax.experimental.pallas import tpu as pltpu

F32 = jnp.float32
BF16 = jnp.bfloat16
HIGHEST = lax.Precision.HIGHEST

D_MODEL = 1024
BATCH = 16
SEQ = 256
DEPTH = 4
DEC_BATCH = 4
DEC_SEQ = 1024
EPS = 1e-6
TINY_F32 = float(np.finfo(np.float32).tiny)
D_HY = 512
N_BANDS = 8
FILT_EMB = 1 + 2 * N_BANDS
FILT_HID = 64
DECAY_SLOW = -math.log(1e-2) / 1.5
DECAY_FAST = -math.log(1e-2) / 0.3
D_FN = 512
FN_GROUP = 128
N_FN_GROUPS = D_FN // FN_GROUP
D_RG = 1024
RG_BLOCK = 128
RG_C = 8.0
D_FF = -(-8 * D_MODEL // (3 * 256)) * 256
D_IN = 3 * D_HY + D_FN + 2 * D_RG + 3 * D_MODEL
COL_FN = 3 * D_HY
COL_RX = COL_FN + D_FN
COL_RY = COL_RX + D_RG
COL_G = COL_RY + D_RG

CHUNK = 1024
N_CTX_TOK = BATCH * SEQ
N_LAT_TOK = DEC_BATCH * DEC_SEQ
N_TOK = N_CTX_TOK + N_LAT_TOK
NQ_CTX = N_CTX_TOK // CHUNK
NQ = N_TOK // CHUNK
CTX_PER_CHUNK = CHUNK // SEQ
MOD_ROWS = 8
FEAT_PAD = 128
HY_TC = 256
RG_TC = 512
K_PIECE = 1024
GATE_ROWS = 256
VMEM_LIMIT = 60 * 1024 * 1024

assert DEC_SEQ == CHUNK and CHUNK % SEQ == 0 and N_CTX_TOK % CHUNK == 0
assert SEQ % 16 == 0 and DEC_SEQ % 16 == 0
assert 1 + DEC_BATCH <= MOD_ROWS


def _params(*sem):
    return pltpu.CompilerParams(dimension_semantics=sem, vmem_limit_bytes=VMEM_LIMIT)


def _mod_row(q):
    return jnp.maximum(q - (NQ_CTX - 1), 0)


def _rmsnorm(x, g):
    return x * lax.rsqrt(jnp.mean(x * x, axis=-1, keepdims=True) + EPS) * g


def _sigmoid(x):
    return 0.5 * jnp.tanh(0.5 * x) + 0.5


def _dot_split3(a, w):
    rows = a.shape[0]
    a_hi = a.astype(BF16)
    a_lo = (a - a_hi.astype(F32)).astype(BF16)
    w_hi = w.astype(BF16)
    w_lo = (w - w_hi.astype(F32)).astype(BF16)
    heads = jnp.dot(jnp.concatenate([a_hi, a_lo], axis=0), w_hi, preferred_element_type=F32)
    return (heads[:rows] + heads[rows:]) + jnp.dot(a_hi, w_lo, preferred_element_type=F32)


def _gelu_tanh(x):
    inner = x * (math.sqrt(2.0 / math.pi) * 0.044715 * (x * x) + math.sqrt(2.0 / math.pi))
    half = 0.5 * x
    return half * jnp.tanh(inner) + half


def _time_of_row(L):
    p = np.arange(L)
    return (p % 8) * (L // 8) + p // 8


def _to_time_permuted(x, L):
    b = x.shape[0]
    return x.reshape(b, 8, L // 8, x.shape[-1]).transpose(0, 2, 1, 3).reshape(b * L, x.shape[-1])


def _from_time_permuted(x, b, L):
    return x.reshape(b, L // 8, 8, x.shape[-1]).transpose(0, 2, 1, 3).reshape(b, L, x.shape[-1])


def _shift_time(x, k, L):
    sub = lax.broadcasted_iota(jnp.int32, (8, 1), 0)
    out = []
    for s0 in range(0, x.shape[0], L):
        xs = x[s0:s0 + L]
        if k > 0:
            wrap = [jnp.where(sub == 0, 0.0, pltpu.roll(xs[L - 8 * (k - i):L - 8 * (k - i - 1)], 1, axis=0))
                    for i in range(k)]
            out += wrap + [xs[:L - 8 * k]]
        else:
            wrap = [jnp.where(sub == 7, 0.0, pltpu.roll(xs[8 * i:8 * (i + 1)], 7, axis=0))
                    for i in range(-k)]
            out += [xs[-8 * k:]] + wrap
    return jnp.concatenate(out, axis=0)


def _as_bf16(*tables):
    return tuple(jnp.asarray(t, dtype=F32).astype(BF16) for t in tables)


def _angle_table(n_rows, n_cols, period):
    prod = np.outer(np.arange(n_rows, dtype=np.int64), np.arange(n_cols, dtype=np.int64)) % period
    return 2.0 * np.pi * prod.astype(np.float64) / period


@functools.lru_cache(maxsize=None)
def _hyena_dft_tables(L):
    n = 2 * L
    ang = _angle_table(L, L, n)
    alt = np.where(np.arange(L) % 2 == 0, 1.0, -1.0)
    cos_f, sin_f = np.cos(ang), np.sin(ang)
    sin_f[0, :] = alt
    fwd = np.concatenate([cos_f, sin_f], axis=0)
    cos_i, sin_i = 2.0 / n * np.cos(ang), 2.0 / n * np.sin(ang)
    cos_i[:, 0] = 1.0 / n
    sin_i[:, 0] = alt / n
    inv = np.concatenate([cos_i, sin_i], axis=1)
    perm = _time_of_row(L)
    return fwd[:, perm].astype(np.float32), inv[perm, :].astype(np.float32)


@functools.lru_cache(maxsize=None)
def _fnet_tables(L):
    perm = _time_of_row(L)
    ang = _angle_table(L, L, L)[perm][:, perm]
    seq = np.concatenate([np.cos(ang), -np.sin(ang)], axis=1)
    ang_c = _angle_table(FN_GROUP, FN_GROUP, FN_GROUP)
    chan = np.concatenate([np.cos(ang_c), np.sin(ang_c)], axis=1)
    return seq.astype(np.float32), chan.astype(np.float32)


@functools.lru_cache(maxsize=None)
def _filter_tables(L):
    t = np.arange(L, dtype=np.float32) / np.float32(L)
    ang = 2.0 * np.pi * t[:, None].astype(np.float64) * np.arange(1, N_BANDS + 1, dtype=np.float64)
    feats = np.zeros((L, FEAT_PAD), np.float64)
    feats[:, 0] = t
    feats[:, 1:1 + N_BANDS] = np.sin(ang)
    feats[:, 1 + N_BANDS:FILT_EMB] = np.cos(ang)
    deltas = np.linspace(DECAY_SLOW, DECAY_FAST, D_HY, dtype=np.float32).astype(np.float64)
    decay = np.exp(-t[:, None].astype(np.float64) * deltas)
    perm = _time_of_row(L)
    return jnp.asarray(feats[perm], dtype=F32), jnp.asarray(decay[perm], dtype=F32)


def _mod_kernel(c_ref, w_ref, b_ref, o_ref):
    cv = c_ref[...]
    act = cv * _sigmoid(cv)
    o_ref[...] = _dot_split3(act, w_ref[...]) + b_ref[...]


def _modulation(cvec, w_ada, b_ada):
    tn = 1536
    return pl.pallas_call(
        _mod_kernel,
        grid=(DEPTH, 6 * D_MODEL // tn),
        in_specs=[
            pl.BlockSpec((MOD_ROWS, D_MODEL), lambda l, j: (0, 0)),
            pl.BlockSpec((None, D_MODEL, tn), lambda l, j: (l, 0, j)),
            pl.BlockSpec((None, 1, tn), lambda l, j: (l, 0, j)),
        ],
        out_specs=pl.BlockSpec((None, MOD_ROWS, tn), lambda l, j: (l, 0, j)),
        out_shape=jax.ShapeDtypeStruct((DEPTH, MOD_ROWS, 6 * D_MODEL), F32),
        compiler_params=_params("arbitrary", "arbitrary"),
        name="adaln_modulation",
    )(cvec, w_ada, b_ada.reshape(DEPTH, 1, 6 * D_MODEL))


def _filter_kernel(feats_ref, decay_ref, wf_ref, w1_ref, b1_ref, w2_ref, b2_ref, w3_ref, fq_ref,
                   p_ref, nyq_ref, hid_scr, *, L):
    @pl.when(pl.program_id(1) == 0)
    def _():
        freq = fq_ref[...]
        h = jnp.sin(freq * (jnp.dot(feats_ref[...], w1_ref[...], precision=HIGHEST,
                                    preferred_element_type=F32) + b1_ref[...]))
        hid_scr[...] = jnp.sin(freq * (jnp.dot(h, w2_ref[...], precision=HIGHEST,
                                               preferred_element_type=F32) + b2_ref[...]))

    h = _dot_split3(hid_scr[...], w3_ref[...])
    decay = decay_ref[...]
    row = lax.broadcasted_iota(jnp.int32, (L, 1), 0)
    h_fwd = h[:, :D_HY] * decay
    h_bwd = jnp.where(row == 0, 0.0, h[:, D_HY:] * decay)
    ssq = jnp.sum(h_fwd * h_fwd, axis=0, keepdims=True) + jnp.sum(h_bwd * h_bwd, axis=0, keepdims=True)
    scale = lax.rsqrt(ssq + EPS)
    even = (h_fwd + h_bwd) * scale
    odd = (h_bwd - h_fwd) * scale
    k_re = jnp.dot(wf_ref[0:L, :], even.astype(BF16), preferred_element_type=F32)
    k_im = jnp.dot(wf_ref[L:2 * L, :], odd.astype(BF16), preferred_element_type=F32)
    alt = jnp.where(((row >> 3) & 1) == 0, 1.0, -1.0)
    k_nyq = jnp.sum(even * alt, axis=0, keepdims=True)
    p_ref[0] = k_re
    p_ref[1] = jnp.where(row == 0, 0.0, k_im)
    nyq_ref[...] = k_nyq


def _hyena_filters(L, wf, fw1, fb1, fw2, fb2, fw3, ffreq):
    feats, decay = _filter_tables(L)
    w1 = jnp.pad(fw1, ((0, 0), (0, FEAT_PAD - FILT_EMB), (0, 0)))
    const = lambda shape: pl.BlockSpec(shape, lambda l, o: (0,) * len(shape))
    per_layer = lambda *shape: pl.BlockSpec((None,) + shape, lambda l, o: (l,) + (0,) * len(shape))
    return pl.pallas_call(
        functools.partial(_filter_kernel, L=L),
        grid=(DEPTH, 2),
        in_specs=[
            const((L, FEAT_PAD)), const((L, D_HY)), const((2 * L, L)),
            per_layer(FEAT_PAD, FILT_HID), per_layer(1, FILT_HID),
            per_layer(FILT_HID, FILT_HID), per_layer(1, FILT_HID),
            pl.BlockSpec((None, FILT_HID, 2 * D_HY), lambda l, o: (l, 0, o)),
            per_layer(1, FILT_HID),
        ],
        out_specs=[pl.BlockSpec((None, None, 2, L, D_HY), lambda l, o: (l, o, 0, 0, 0)),
                   pl.BlockSpec((None, None, 1, D_HY), lambda l, o: (l, o, 0, 0))],
        out_shape=[jax.ShapeDtypeStruct((DEPTH, 2, 2, L, D_HY), F32),
                   jax.ShapeDtypeStruct((DEPTH, 2, 1, D_HY), F32)],
        scratch_shapes=[pltpu.VMEM((L, FILT_HID), F32)],
        compiler_params=_params("arbitrary", "arbitrary"),
        name=f"hyena_filters_{L}",
    )(feats, decay, wf, w1, fb1.reshape(DEPTH, 1, FILT_HID), fw2, fb2.reshape(DEPTH, 1, FILT_HID),
      fw3, ffreq.reshape(DEPTH, 1, FILT_HID))


def _modulated_norm(x, g, mod_ref, q, col):
    r = _mod_row(q)
    shift = mod_ref[pl.ds(r, 1), col * D_MODEL:(col + 1) * D_MODEL]
    scale = mod_ref[pl.ds(r, 1), (col + 1) * D_MODEL:(col + 2) * D_MODEL]
    return (_rmsnorm(x, g) * (1.0 + scale) + shift).astype(BF16)


def _normalize_tile(i, x_ref, mod_ref, g_ref, h_scr, *, tm, col):
    for s in range(tm // CHUNK):
        rows = slice(s * CHUNK, (s + 1) * CHUNK)
        h_scr[rows, :] = _modulated_norm(x_ref[rows, :], g_ref[...], mod_ref, i * (tm // CHUNK) + s, col)


def _inproj_kernel(x_ref, mod_ref, g_ref, w_ref, u_ref, h_scr, *, tm):
    @pl.when(pl.program_id(1) == 0)
    def _():
        _normalize_tile(pl.program_id(0), x_ref, mod_ref, g_ref, h_scr, tm=tm, col=0)

    u_ref[...] = jnp.dot(h_scr[...], w_ref[...].astype(BF16), preferred_element_type=F32).astype(BF16)


def _input_projection(x, mod, norm_g, w_in, l, tm=2048, tn=1024):
    return pl.pallas_call(
        functools.partial(_inproj_kernel, tm=tm),
        grid=(N_TOK // tm, D_IN // tn),
        in_specs=[
            pl.BlockSpec((tm, D_MODEL), lambda i, j: (i, 0)),
            pl.BlockSpec((None, MOD_ROWS, 6 * D_MODEL), lambda i, j: (l, 0, 0)),
            pl.BlockSpec((None, 1, D_MODEL), lambda i, j: (l, 0, 0)),
            pl.BlockSpec((None, D_MODEL, tn), lambda i, j: (l, 0, j)),
        ],
        out_specs=pl.BlockSpec((tm, tn), lambda i, j: (i, j)),
        out_shape=jax.ShapeDtypeStruct((N_TOK, D_IN), BF16),
        scratch_shapes=[pltpu.VMEM((tm, D_MODEL), BF16)],
        compiler_params=_params("arbitrary", "arbitrary"),
        name="input_projection",
    )(x, mod, norm_g, w_in)


def _interleave(*task_lists):
    steps = max(len(tasks) for tasks in task_lists)
    done = [0] * len(task_lists)
    for step in range(1, steps + 1):
        for i, tasks in enumerate(task_lists):
            while done[i] < (step * len(tasks)) // steps:
                tasks[done[i]]()
                done[i] += 1


def _dot_by_k_tiles(lhs_ref, rhs_of, out):
    k_tile = min(K_PIECE, lhs_ref.shape[1])

    def piece(k):
        def run():
            cols = slice(k * k_tile, (k + 1) * k_tile)
            part = jnp.dot(lhs_ref[:, cols], rhs_of()[cols], preferred_element_type=F32)
            out["acc"] = part if k == 0 else out["acc"] + part
        return run
    return [piece(k) for k in range(lhs_ref.shape[1] // k_tile)]


def _spectrum_product(spec, p_ref, nyq_ref, order, L):
    z_re, z_sn = spec[:L], spec[L:]
    p_re, p_im = p_ref[order, 0], p_ref[order, 1]
    y_re = z_re * p_re + z_sn * p_im
    y_sn = z_sn * p_re - z_re * p_im
    first = lax.broadcasted_iota(jnp.int32, (8, 1), 0) == 0
    y_sn_top = jnp.where(first, z_sn[:8] * nyq_ref[order], y_sn[:8])
    return jnp.concatenate([y_re, y_sn_top, y_sn[8:]], axis=0).astype(BF16)


def _hyena_tasks(L, v_ref, x1_ref, x2_ref, wv_ref, w1_ref, w2_ref, bv_ref, b1_ref, b2_ref, hb_ref,
                 p_ref, nyq_ref, wf_ref, wi_ref, o_ref):
    chunk = {}

    def prepare():
        def short_conv(u_ref, w_ref, b_ref):
            u = u_ref[...].astype(F32)
            w = w_ref[...]
            return (b_ref[...] + _shift_time(u, 1, L) * w[0:1] + u * w[1:2]
                    + _shift_time(u, -1, L) * w[2:3])
        chunk["v"] = short_conv(v_ref, wv_ref, bv_ref)
        chunk["x1"] = short_conv(x1_ref, w1_ref, b1_ref)
        chunk["x2"] = short_conv(x2_ref, w2_ref, b2_ref)
        chunk["bias"] = hb_ref[...]

    tasks = [prepare]
    for s in range(CHUNK // L):
        rows = slice(s * L, (s + 1) * L)
        seq = {}

        def begin(seq=seq, rows=rows):
            seq["in0"] = chunk["v"][rows]
            seq["in0_bf16"] = seq["in0"].astype(BF16)

        tasks.append(begin)
        for order in (0, 1):
            spec, conv = {}, {}
            tasks += _dot_by_k_tiles(wf_ref, lambda seq=seq, order=order: seq[f"in{order}_bf16"], spec)

            def pointwise(seq=seq, spec=spec, order=order):
                seq[f"y{order}"] = _spectrum_product(spec["acc"], p_ref, nyq_ref, order, L)

            tasks.append(pointwise)
            tasks += _dot_by_k_tiles(wi_ref, lambda seq=seq, order=order: seq[f"y{order}"], conv)

            def gate(seq=seq, conv=conv, order=order, rows=rows):
                x = chunk["x1" if order == 0 else "x2"][rows]
                z = seq[f"in{order}"]
                out = x * (conv["acc"] + chunk["bias"][order:order + 1] * z)
                if order == 0:
                    seq["in1"] = out
                    seq["in1_bf16"] = out.astype(BF16)
                else:
                    o_ref[rows, :] = out.astype(BF16)

            tasks.append(gate)
    return tasks


def _fnet_tasks(L, u_ref, cs_ref, f_ref, o_ref):
    chunk = {}

    def prepare():
        x = u_ref[...].astype(BF16)
        parts_c, parts_s = [], []
        for g in range(x.shape[1] // FN_GROUP):
            r = jnp.dot(x[:, g * FN_GROUP:(g + 1) * FN_GROUP], cs_ref[...], preferred_element_type=F32)
            parts_c.append(r[:, :FN_GROUP])
            parts_s.append(r[:, FN_GROUP:])
        chunk["xc"] = jnp.concatenate(parts_c, axis=1).astype(BF16)
        chunk["xs"] = jnp.concatenate(parts_s, axis=1).astype(BF16)

    tasks = [prepare]
    norm = 1.0 / math.sqrt(L * FN_GROUP)
    for s in range(CHUNK // L):
        rows = slice(s * L, (s + 1) * L)
        seq, out = {}, {}

        def stack(seq=seq, rows=rows):
            seq["stacked"] = jnp.concatenate([chunk["xc"][rows], chunk["xs"][rows]], axis=0)

        def finish(out=out, rows=rows):
            o_ref[rows, :] = (out["acc"] * norm).astype(BF16)

        tasks += [stack] + _dot_by_k_tiles(f_ref, lambda seq=seq: seq["stacked"], out) + [finish]
    return tasks


def _scan_sequence(s, L, a_f, b_f, a_b, b_b, h0_ref, y_ref, st_ref):
    nb = L // 8
    tc = a_f.shape[1]
    s0 = s * L

    def local(j, carry):
        hf, pf, hb, pb = carry
        rf = pl.multiple_of(s0 + 8 * j, 8)
        rb = pl.multiple_of(s0 + L - 8 - 8 * j, 8)
        af = a_f[pl.ds(rf, 8), :]
        hf = af * hf + b_f[pl.ds(rf, 8), :]
        pf = af * pf
        b_f[pl.ds(rf, 8), :] = hf
        a_f[pl.ds(rf, 8), :] = pf
        ab = a_b[pl.ds(rb, 8), :]
        hb = ab * hb + b_b[pl.ds(rb, 8), :]
        pb = ab * pb
        b_b[pl.ds(rb, 8), :] = hb
        a_b[pl.ds(rb, 8), :] = pb
        return hf, pf, hb, pb

    zero = jnp.zeros((8, tc), F32)
    one = jnp.ones((8, tc), F32)
    hf, pf, hb, pb = lax.fori_loop(0, nb, local, (zero, one, zero, one), unroll=4)

    carry = h0_ref[s, 0:1, :]
    rows = []
    for k in range(8):
        rows.append(carry)
        carry = hf[k:k + 1] + pf[k:k + 1] * carry
    carry_f = jnp.concatenate(rows, axis=0)
    st_ref[s, 0:1, :] = carry
    carry = h0_ref[s, 1:2, :]
    rows = [None] * 8
    for k in range(7, -1, -1):
        rows[k] = carry
        carry = hb[k:k + 1] + pb[k:k + 1] * carry
    carry_b = jnp.concatenate(rows, axis=0)
    st_ref[s, 1:2, :] = carry

    sl = slice(s0, s0 + L)
    blocked = lambda ref: ref[sl, :].reshape(nb, 8, tc)
    h_sum = ((blocked(b_f) + blocked(a_f) * carry_f[None]) + (blocked(b_b) + blocked(a_b) * carry_b[None]))
    y_ref[sl, :] = h_sum.reshape(L, tc).astype(BF16)


def _rglru_gate_tasks(L, reset, ux_ref, cw_ref, cb_ref, wg_ref, br_ref, bi_ref, lam_ref,
                      a_f, b_f, a_b, b_b):
    chunk = {}

    def prepare():
        u = ux_ref[...].astype(F32)
        w = cw_ref[...]
        xr = (cb_ref[...] + _shift_time(u, 2, L) * w[0:1] + _shift_time(u, 1, L) * w[1:2]
              + u * w[2:3] + _shift_time(u, -1, L) * w[3:4])
        chunk["x"] = xr
        chunk["x_bf16"] = xr.astype(BF16)
        neg_lam = -lam_ref[...]
        softplus = jnp.maximum(neg_lam, 0.0) + jnp.log1p(jnp.exp(-jnp.abs(neg_lam)))
        chunk["rate"] = (-0.25 * RG_C) * softplus
        chunk["half_br"] = 0.5 * br_ref[...]
        chunk["half_bi"] = 0.5 * bi_ref[...]

    tasks = [prepare]
    for n in range(RG_TC // RG_BLOCK):
        cs = slice(n * RG_BLOCK, (n + 1) * RG_BLOCK)

        def project(n=n, cs=cs):
            chunk["gates", n] = jnp.dot(chunk["x_bf16"][:, cs], wg_ref[n].astype(BF16),
                                        preferred_element_type=F32)

        tasks.append(project)
        for d, (a_scr, b_scr) in enumerate(((a_f, b_f), (a_b, b_b))):
            for r0 in range(0, CHUNK, GATE_ROWS):
                def gate(n=n, cs=cs, d=d, a_scr=a_scr, b_scr=b_scr, r0=r0):
                    rows = slice(r0, r0 + GATE_ROWS)
                    gates = chunk["gates", n][rows]
                    c0 = 2 * d * RG_BLOCK
                    t_r = jnp.tanh(gates[:, c0:c0 + RG_BLOCK] + chunk["half_br"][d:d + 1, cs])
                    t_i = jnp.tanh(gates[:, c0 + RG_BLOCK:c0 + 2 * RG_BLOCK] + chunk["half_bi"][d:d + 1, cs])
                    rate = chunk["rate"][d:d + 1, cs]
                    th = jnp.tanh(rate * t_r + rate)
                    recip = 1.0 / (1.0 - th)
                    neg_th = -th
                    root = neg_th * lax.rsqrt(jnp.maximum(neg_th, TINY_F32))
                    half_mult = recip * root
                    if reset:
                        tpos = (r0 + lax.broadcasted_iota(jnp.int32, (GATE_ROWS, 1), 0)) & (L - 1)
                        half_mult = jnp.where(tpos == (0 if d == 0 else L - 1), 0.5, half_mult)
                    a_scr[rows, cs] = (1.0 + th) * recip
                    b_scr[rows, cs] = (half_mult * chunk["x"][rows, cs]) * (t_i + 1.0)

                tasks.append(gate)
    return tasks


def _mixers_kernel(hv_ref, hx1_ref, hx2_ref, fn_ref, ux_ref,
                   hwv_ref, hw1_ref, hw2_ref, hbv_ref, hb1_ref, hb2_ref, hbias_ref,
                   ps_ref, ns_ref, pl_ref, nl_ref, wfs_ref, wis_ref, wfl_ref, wil_ref,
                   cs_ref, fs_ref, fl_ref,
                   h0_ref, cw_ref, cb_ref, wg_ref, br_ref, bi_ref, lam_ref,
                   yh_ref, yf_ref, y_ref, st_ref, a_f, b_f, a_b, b_b):
    q = pl.program_id(1)

    def run(L, reset, p_ref, nyq_ref, wf_ref, wi_ref, f_ref):
        matmul_side = (_hyena_tasks(L, hv_ref, hx1_ref, hx2_ref, hwv_ref, hw1_ref, hw2_ref, hbv_ref,
                                    hb1_ref, hb2_ref, hbias_ref, p_ref, nyq_ref, wf_ref, wi_ref, yh_ref)
                       + _fnet_tasks(L, fn_ref, cs_ref, f_ref, yf_ref))
        vector_side = _rglru_gate_tasks(L, reset, ux_ref, cw_ref, cb_ref, wg_ref, br_ref, bi_ref,
                                        lam_ref, a_f, b_f, a_b, b_b)
        _interleave(matmul_side, vector_side)
        st_ref[...] = jnp.zeros(st_ref.shape, F32)
        for s in range(CHUNK // L):
            _scan_sequence(s, L, a_f, b_f, a_b, b_b, h0_ref, y_ref, st_ref)

    @pl.when(q < NQ_CTX)
    def _():
        run(SEQ, True, ps_ref, ns_ref, wfs_ref, wis_ref, fs_ref)

    @pl.when(q >= NQ_CTX)
    def _():
        run(DEC_SEQ, False, pl_ref, nl_ref, wfl_ref, wil_ref, fl_ref)


def _mixers(u, l, hy_conv_w, hy_conv_b, hy_bias, spectra_short, spectra_long, dft_short, dft_long,
            fnet_chan, fnet_short, fnet_long, h0, rg_conv_w, rg_conv_b, w_gates, b_r, b_i, lam):
    assert D_HY // HY_TC == D_FN // HY_TC == D_RG // RG_TC
    halves = D_HY // HY_TC
    p_short, nyq_short = spectra_short
    p_long, nyq_long = spectra_long
    ucol = lambda width, col0: pl.BlockSpec((CHUNK, width), lambda c, q: (q, col0 // width + c))
    hy_part = lambda rows, k: pl.BlockSpec((None, rows, HY_TC), lambda c, q: (l, 0, k * halves + c))
    hy_tile = lambda rows: pl.BlockSpec((None, rows, HY_TC), lambda c, q: (l, 0, c))
    rg_tile = lambda rows: pl.BlockSpec((None, rows, RG_TC), lambda c, q: (l, 0, c))
    const = lambda shape: pl.BlockSpec(shape, lambda c, q: (0, 0))
    planes = lambda L: pl.BlockSpec((None, 2, 2, L, HY_TC), lambda c, q: (l, 0, 0, 0, c))
    nyquist = pl.BlockSpec((None, 2, 1, HY_TC), lambda c, q: (l, 0, 0, c))
    state_spec = pl.BlockSpec((None, CTX_PER_CHUNK, 2, RG_TC), lambda c, q: (q, 0, 0, c))
    out_tile = lambda width: pl.BlockSpec((CHUNK, width), lambda c, q: (q, c))
    hy_conv_b = hy_conv_b.reshape(DEPTH, 1, 3 * D_HY)
    return pl.pallas_call(
        _mixers_kernel,
        grid=(halves, NQ),
        in_specs=[
            ucol(HY_TC, 0), ucol(HY_TC, D_HY), ucol(HY_TC, 2 * D_HY), ucol(HY_TC, COL_FN),
            ucol(RG_TC, COL_RX),
            hy_part(3, 0), hy_part(3, 1), hy_part(3, 2), hy_part(1, 0), hy_part(1, 1), hy_part(1, 2),
            hy_tile(2),
            planes(SEQ), nyquist, planes(DEC_SEQ), nyquist,
            const((2 * SEQ, SEQ)), const((SEQ, 2 * SEQ)),
            const((2 * DEC_SEQ, DEC_SEQ)), const((DEC_SEQ, 2 * DEC_SEQ)),
            const((FN_GROUP, 2 * FN_GROUP)), const((SEQ, 2 * SEQ)), const((DEC_SEQ, 2 * DEC_SEQ)),
            state_spec, rg_tile(4), rg_tile(1),
            pl.BlockSpec((None, RG_TC // RG_BLOCK, RG_BLOCK, 4 * RG_BLOCK), lambda c, q: (l, c, 0, 0)),
            rg_tile(2), rg_tile(2), rg_tile(2),
        ],
        out_specs=[out_tile(HY_TC), out_tile(HY_TC), out_tile(RG_TC), state_spec],
        out_shape=[jax.ShapeDtypeStruct((N_TOK, D_HY), BF16),
                   jax.ShapeDtypeStruct((N_TOK, D_FN), BF16),
                   jax.ShapeDtypeStruct((N_TOK, D_RG), BF16),
                   jax.ShapeDtypeStruct((NQ, CTX_PER_CHUNK, 2, D_RG), F32)],
        scratch_shapes=[pltpu.VMEM((CHUNK, RG_TC), F32)] * 4,
        compiler_params=_params("arbitrary", "arbitrary"),
        name="sequence_mixers",
    )(u, u, u, u, u, hy_conv_w, hy_conv_w, hy_conv_w, hy_conv_b, hy_conv_b, hy_conv_b, hy_bias,
      p_short, nyq_short, p_long, nyq_long, *dft_short, *dft_long, fnet_chan, fnet_short, fnet_long,
      h0, rg_conv_w, rg_conv_b.reshape(DEPTH, 1, D_RG), w_gates, b_r, b_i, lam)


def _merge_kernel(x_ref, ya_ref, yb_ref, hc_ref, uy_ref, ga_ref, gb_ref, gc_ref, mod_ref,
                  wa_ref, wb_ref, wc_ref, wo_ref, o_ref, *, tm):
    r = _mod_row(pl.program_id(0) // (CHUNK // tm))
    gate = mod_ref[pl.ds(r, 1), 2 * D_MODEL:3 * D_MODEL]
    project = lambda y, w_ref: jnp.dot(y, w_ref[...].astype(BF16), preferred_element_type=F32)
    ya = project(ya_ref[...], wa_ref)
    yb = project(yb_ref[...], wb_ref)
    rg = hc_ref[...].astype(F32) * _gelu_tanh(uy_ref[...].astype(F32))
    yc = project(rg.astype(BF16), wc_ref)
    ga, gb, gc = (_sigmoid(g_ref[...].astype(F32)) for g_ref in (ga_ref, gb_ref, gc_ref))
    mix = ga * ya + gb * yb + gc * yc
    o_ref[...] = x_ref[...] + gate * project(mix.astype(BF16), wo_ref)


def _merge(x, y_hy, y_fn, h_rg, u, mod, w_a, w_b, w_c, w_o, l, tm=512):
    rows = lambda width, col=0: pl.BlockSpec((tm, width), lambda i: (i, col))
    weight = lambda k: pl.BlockSpec((None, k, D_MODEL), lambda i: (l, 0, 0))
    g0 = COL_G // D_MODEL
    return pl.pallas_call(
        functools.partial(_merge_kernel, tm=tm),
        grid=(N_TOK // tm,),
        in_specs=[
            rows(D_MODEL), rows(D_HY), rows(D_FN), rows(D_RG), rows(D_RG, COL_RY // D_RG),
            rows(D_MODEL, g0), rows(D_MODEL, g0 + 1), rows(D_MODEL, g0 + 2),
            pl.BlockSpec((None, MOD_ROWS, 6 * D_MODEL), lambda i: (l, 0, 0)),
            weight(D_HY), weight(D_FN), weight(D_RG), weight(D_MODEL),
        ],
        out_specs=rows(D_MODEL),
        out_shape=jax.ShapeDtypeStruct((N_TOK, D_MODEL), F32),
        compiler_params=_params("arbitrary"),
        name="branch_merge",
    )(x, y_hy, y_fn, h_rg, u, u, u, u, mod, w_a, w_b, w_c, w_o)


def _ffn_kernel(x_ref, mod_ref, g_ref, wg_ref, wu_ref, wd_ref, fg_ref, o_ref, h_scr, *, tm, final_norm):
    i = pl.program_id(0)
    f = pl.program_id(1)
    sub = tm // CHUNK

    @pl.when(f == 0)
    def _():
        _normalize_tile(i, x_ref, mod_ref, g_ref, h_scr, tm=tm, col=3)
        o_ref[...] = jnp.zeros(o_ref.shape, F32)

    h = h_scr[...]
    gt = jnp.dot(h, wg_ref[...].astype(BF16), preferred_element_type=F32)
    up = jnp.dot(h, wu_ref[...].astype(BF16), preferred_element_type=F32)
    act = (gt * _sigmoid(gt)) * up
    o_ref[...] += jnp.dot(act.astype(BF16), wd_ref[...].astype(BF16), preferred_element_type=F32)

    @pl.when(f == pl.num_programs(1) - 1)
    def _():
        for s in range(sub):
            r = _mod_row(i * sub + s)
            gate = mod_ref[pl.ds(r, 1), 5 * D_MODEL:6 * D_MODEL]
            rows = slice(s * CHUNK, (s + 1) * CHUNK)
            out = x_ref[rows, :] + gate * o_ref[rows, :]
            if final_norm:
                out = _rmsnorm(out, fg_ref[...])
            o_ref[rows, :] = out


def _ffn(x, mod, norm_g, w_gu, w_down, final_g, l, final_norm, tm=2048, tf=256):
    nf = D_FF // tf
    return pl.pallas_call(
        functools.partial(_ffn_kernel, tm=tm, final_norm=final_norm),
        grid=(N_TOK // tm, nf),
        in_specs=[
            pl.BlockSpec((tm, D_MODEL), lambda i, f: (i, 0)),
            pl.BlockSpec((None, MOD_ROWS, 6 * D_MODEL), lambda i, f: (l, 0, 0)),
            pl.BlockSpec((None, 1, D_MODEL), lambda i, f: (l, 0, 0)),
            pl.BlockSpec((None, D_MODEL, tf), lambda i, f: (l, 0, f)),
            pl.BlockSpec((None, D_MODEL, tf), lambda i, f: (l, 0, nf + f)),
            pl.BlockSpec((None, tf, D_MODEL), lambda i, f: (l, f, 0)),
            pl.BlockSpec((1, D_MODEL), lambda i, f: (0, 0)),
        ],
        out_specs=pl.BlockSpec((tm, D_MODEL), lambda i, f: (i, 0)),
        out_shape=jax.ShapeDtypeStruct((N_TOK, D_MODEL), F32),
        scratch_shapes=[pltpu.VMEM((tm, D_MODEL), BF16)],
        compiler_params=_params("arbitrary", "arbitrary"),
        name="swiglu_ffn",
    )(x, mod, norm_g, w_gu, w_gu, w_down, final_g)


def kernel(x_prompt, x_sample, c, state_rglru, c_ctx, norm1_g, norm2_g, w_ada, b_ada, w_in,
           hy_conv_w, hy_conv_b, hy_f_w1, hy_f_b1, hy_f_w2, hy_f_b2, hy_f_w3, hy_f_freq, hy_bias,
           w_a, w_b, rg_conv_w, rg_conv_b, rg_wr, rg_br, rg_wi, rg_bi, rg_lam, w_c, w_o,
           w_gu, w_down, final_g):
    x = jnp.concatenate([_to_time_permuted(x_prompt, SEQ), _to_time_permuted(x_sample, DEC_SEQ)])
    cvec = jnp.concatenate([c_ctx[None, :], c, jnp.zeros((MOD_ROWS - 1 - DEC_BATCH, D_MODEL), F32)])
    mod = _modulation(cvec, w_ada, b_ada)

    dft_short = _as_bf16(*_hyena_dft_tables(SEQ))
    dft_long = _as_bf16(*_hyena_dft_tables(DEC_SEQ))
    fnet_short, fnet_chan = _as_bf16(*_fnet_tables(SEQ))
    fnet_long, = _as_bf16(_fnet_tables(DEC_SEQ)[0])
    filt = (hy_f_w1, hy_f_b1, hy_f_w2, hy_f_b2, hy_f_w3, hy_f_freq)
    p_short = _hyena_filters(SEQ, dft_short[0], *filt)
    p_long = _hyena_filters(DEC_SEQ, dft_long[0], *filt)

    w_gates = 0.5 * jnp.concatenate([rg_wr[:, 0], rg_wi[:, 0], rg_wr[:, 1], rg_wi[:, 1]], axis=-1)
    lat_h0 = jnp.pad(state_rglru.astype(F32).transpose(1, 0, 2, 3)[:, :, None],
                     ((0, 0), (0, 0), (0, CTX_PER_CHUNK - 1), (0, 0), (0, 0)))
    h0_all = jnp.concatenate([jnp.zeros((DEPTH, NQ_CTX, CTX_PER_CHUNK, 2, D_RG), F32), lat_h0], axis=1)
    norm1 = norm1_g.reshape(DEPTH, 1, D_MODEL)
    norm2 = norm2_g.reshape(DEPTH, 1, D_MODEL)
    final = final_g.reshape(1, D_MODEL)

    states = []
    for l in range(DEPTH):
        u = _input_projection(x, mod, norm1, w_in, l)
        y_hy, y_fn, h_rg, st = _mixers(
            u, l, hy_conv_w, hy_conv_b, hy_bias, p_short, p_long, dft_short, dft_long,
            fnet_chan, fnet_short, fnet_long, h0_all[l], rg_conv_w, rg_conv_b, w_gates,
            rg_br, rg_bi, rg_lam)
        states.append(st[:NQ_CTX].reshape(BATCH, 2, D_RG))
        x = _merge(x, y_hy, y_fn, h_rg, u, mod, w_a, w_b, w_c, w_o, l)
        x = _ffn(x, mod, norm2, w_gu, w_down, final, l, final_norm=(l == DEPTH - 1))

    y_prompt = _from_time_permuted(x[:N_CTX_TOK], BATCH, SEQ)
    y_sample = _from_time_permuted(x[N_CTX_TOK:], DEC_BATCH, DEC_SEQ)
    new_state = jnp.stack(states, axis=1).astype(x_prompt.dtype)
    return (y_prompt, y_sample, new_state)
```

```python
import functools
import math

import numpy as np
import jax
import jax.numpy as jnp
from jax import lax
from jax.experimental import pallas as pl
from jax.experimental.pallas import tpu as pltpu

F32 = jnp.float32
BF16 = jnp.bfloat16
HIGHEST = lax.Precision.HIGHEST

D_MODEL = 1024
BATCH = 16
SEQ = 256
DEPTH = 4
DEC_BATCH = 4
DEC_SEQ = 1024
EPS = 1e-6
TINY_F32 = float(np.finfo(np.float32).tiny)
D_HY = 512
N_BANDS = 8
FILT_EMB = 1 + 2 * N_BANDS
FILT_HID = 64
DECAY_SLOW = -math.log(1e-2) / 1.5
DECAY_FAST = -math.log(1e-2) / 0.3
D_FN = 512
FN_GROUP = 128
N_FN_GROUPS = D_FN // FN_GROUP
D_RG = 1024
RG_BLOCK = 128
RG_C = 8.0
D_FF = -(-8 * D_MODEL // (3 * 256)) * 256
D_IN = 3 * D_HY + D_FN + 2 * D_RG + 3 * D_MODEL
COL_FN = 3 * D_HY
COL_RX = COL_FN + D_FN
COL_RY = COL_RX + D_RG
COL_G = COL_RY + D_RG

CHUNK = 1024
N_CTX_TOK = BATCH * SEQ
N_LAT_TOK = DEC_BATCH * DEC_SEQ
N_TOK = N_CTX_TOK + N_LAT_TOK
NQ_CTX = N_CTX_TOK // CHUNK
NQ = N_TOK // CHUNK
CTX_PER_CHUNK = CHUNK // SEQ
MOD_ROWS = 8
FEAT_PAD = 128
HY_TC = 256
RG_TC = 512
K_PIECE = 1024
GATE_ROWS = 256
SPLIT_DOT_ROWS = 1024
VMEM_LIMIT = 60 * 1024 * 1024

assert DEC_SEQ == CHUNK and CHUNK % SEQ == 0 and N_CTX_TOK % CHUNK == 0
assert SEQ % 16 == 0 and DEC_SEQ % 16 == 0
assert 1 + DEC_BATCH <= MOD_ROWS


def _params(*sem):
    return pltpu.CompilerParams(dimension_semantics=sem, vmem_limit_bytes=VMEM_LIMIT)


def _mod_row(q):
    return jnp.maximum(q - (NQ_CTX - 1), 0)


def _rmsnorm(x, g):
    return x * lax.rsqrt(jnp.mean(x * x, axis=-1, keepdims=True) + EPS) * g


def _sigmoid(x):
    return 0.5 * jnp.tanh(0.5 * x) + 0.5


def _dot_split3(a, w):
    rows = a.shape[0]
    a_hi = a.astype(BF16)
    a_lo = (a - a_hi.astype(F32)).astype(BF16)
    w_hi = w.astype(BF16)
    w_lo = (w - w_hi.astype(F32)).astype(BF16)
    heads = jnp.dot(jnp.concatenate([a_hi, a_lo], axis=0), w_hi, preferred_element_type=F32)
    return (heads[:rows] + heads[rows:]) + jnp.dot(a_hi, w_lo, preferred_element_type=F32)


def _gelu_tanh(x):
    inner = x * (math.sqrt(2.0 / math.pi) * 0.044715 * (x * x) + math.sqrt(2.0 / math.pi))
    half = 0.5 * x
    return half * jnp.tanh(inner) + half


def _time_of_row(L):
    p = np.arange(L)
    return (p % 8) * (L // 8) + p // 8


def _to_time_permuted(x, L):
    b = x.shape[0]
    return x.reshape(b, 8, L // 8, x.shape[-1]).transpose(0, 2, 1, 3).reshape(b * L, x.shape[-1])


def _from_time_permuted(x, b, L):
    return x.reshape(b, L // 8, 8, x.shape[-1]).transpose(0, 2, 1, 3).reshape(b, L, x.shape[-1])


def _shift_time(x, k, L):
    sub = lax.broadcasted_iota(jnp.int32, (8, 1), 0)
    out = []
    for s0 in range(0, x.shape[0], L):
        xs = x[s0:s0 + L]
        if k > 0:
            wrap = [jnp.where(sub == 0, 0.0, pltpu.roll(xs[L - 8 * (k - i):L - 8 * (k - i - 1)], 1, axis=0))
                    for i in range(k)]
            out += wrap + [xs[:L - 8 * k]]
        else:
            wrap = [jnp.where(sub == 7, 0.0, pltpu.roll(xs[8 * i:8 * (i + 1)], 7, axis=0))
                    for i in range(-k)]
            out += [xs[-8 * k:]] + wrap
    return jnp.concatenate(out, axis=0)


def _as_bf16(*tables):
    return tuple(jnp.asarray(t, dtype=F32).astype(BF16) for t in tables)


def _angle_table(n_rows, n_cols, period):
    prod = np.outer(np.arange(n_rows, dtype=np.int64), np.arange(n_cols, dtype=np.int64)) % period
    return 2.0 * np.pi * prod.astype(np.float64) / period


@functools.lru_cache(maxsize=None)
def _hyena_dft_tables(L):
    n = 2 * L
    ang = _angle_table(L, L, n)
    alt = np.where(np.arange(L) % 2 == 0, 1.0, -1.0)
    cos_f, sin_f = np.cos(ang), np.sin(ang)
    sin_f[0, :] = alt
    fwd = np.concatenate([cos_f, sin_f], axis=0)
    cos_i, sin_i = 2.0 / n * np.cos(ang), 2.0 / n * np.sin(ang)
    cos_i[:, 0] = 1.0 / n
    sin_i[:, 0] = alt / n
    inv = np.concatenate([cos_i, sin_i], axis=1)
    perm = _time_of_row(L)
    return fwd[:, perm].astype(np.float32), inv[perm, :].astype(np.float32)


@functools.lru_cache(maxsize=None)
def _fnet_tables(L):
    perm = _time_of_row(L)
    ang = _angle_table(L, L, L)[perm][:, perm]
    seq = np.concatenate([np.cos(ang), -np.sin(ang)], axis=1)
    ang_c = _angle_table(FN_GROUP, FN_GROUP, FN_GROUP)
    chan = np.concatenate([np.cos(ang_c), np.sin(ang_c)], axis=1)
    return seq.astype(np.float32), chan.astype(np.float32)


@functools.lru_cache(maxsize=None)
def _filter_tables(L):
    t = np.arange(L, dtype=np.float32) / np.float32(L)
    ang = 2.0 * np.pi * t[:, None].astype(np.float64) * np.arange(1, N_BANDS + 1, dtype=np.float64)
    feats = np.zeros((L, FEAT_PAD), np.float64)
    feats[:, 0] = t
    feats[:, 1:1 + N_BANDS] = np.sin(ang)
    feats[:, 1 + N_BANDS:FILT_EMB] = np.cos(ang)
    deltas = np.linspace(DECAY_SLOW, DECAY_FAST, D_HY, dtype=np.float32).astype(np.float64)
    decay = np.exp(-t[:, None].astype(np.float64) * deltas)
    perm = _time_of_row(L)
    return jnp.asarray(feats[perm], dtype=F32), jnp.asarray(decay[perm], dtype=F32)


def _mod_kernel(c_ref, w_ref, b_ref, o_ref):
    cv = c_ref[...]
    act = cv * _sigmoid(cv)
    o_ref[...] = _dot_split3(act, w_ref[...]) + b_ref[...]


def _modulation(cvec, w_ada, b_ada):
    tn = 1536
    return pl.pallas_call(
        _mod_kernel,
        grid=(DEPTH, 6 * D_MODEL // tn),
        in_specs=[
            pl.BlockSpec((MOD_ROWS, D_MODEL), lambda l, j: (0, 0)),
            pl.BlockSpec((None, D_MODEL, tn), lambda l, j: (l, 0, j)),
            pl.BlockSpec((None, 1, tn), lambda l, j: (l, 0, j)),
        ],
        out_specs=pl.BlockSpec((None, MOD_ROWS, tn), lambda l, j: (l, 0, j)),
        out_shape=jax.ShapeDtypeStruct((DEPTH, MOD_ROWS, 6 * D_MODEL), F32),
        compiler_params=_params("arbitrary", "arbitrary"),
        name="adaln_modulation",
    )(cvec, w_ada, b_ada.reshape(DEPTH, 1, 6 * D_MODEL))


def _filter_kernel(feats_ref, decay_ref, wf_ref, w1_ref, b1_ref, w2_ref, b2_ref, w3_ref, fq_ref,
                   p_ref, nyq_ref, hid_scr, *, L):
    @pl.when(pl.program_id(1) == 0)
    def _():
        freq = fq_ref[...]
        h = jnp.sin(freq * (jnp.dot(feats_ref[...], w1_ref[...], precision=HIGHEST,
                                    preferred_element_type=F32) + b1_ref[...]))
        hid_scr[...] = jnp.sin(freq * (jnp.dot(h, w2_ref[...], precision=HIGHEST,
                                               preferred_element_type=F32) + b2_ref[...]))

    h = _dot_split3(hid_scr[...], w3_ref[...])
    decay = decay_ref[...]
    row = lax.broadcasted_iota(jnp.int32, (L, 1), 0)
    h_fwd = h[:, :D_HY] * decay
    h_bwd = jnp.where(row == 0, 0.0, h[:, D_HY:] * decay)
    ssq = jnp.sum(h_fwd * h_fwd, axis=0, keepdims=True) + jnp.sum(h_bwd * h_bwd, axis=0, keepdims=True)
    scale = lax.rsqrt(ssq + EPS)
    even = (h_fwd + h_bwd) * scale
    odd = (h_bwd - h_fwd) * scale
    k_re = jnp.dot(wf_ref[0:L, :], even.astype(BF16), preferred_element_type=F32)
    k_im = jnp.dot(wf_ref[L:2 * L, :], odd.astype(BF16), preferred_element_type=F32)
    alt = jnp.where(((row >> 3) & 1) == 0, 1.0, -1.0)
    k_nyq = jnp.sum(even * alt, axis=0, keepdims=True)
    p_ref[0] = k_re
    p_ref[1] = jnp.where(row == 0, 0.0, k_im)
    nyq_ref[...] = k_nyq


def _hyena_filters(L, wf, fw1, fb1, fw2, fb2, fw3, ffreq):
    feats, decay = _filter_tables(L)
    w1 = jnp.pad(fw1, ((0, 0), (0, FEAT_PAD - FILT_EMB), (0, 0)))
    const = lambda shape: pl.BlockSpec(shape, lambda l, o: (0,) * len(shape))
    per_layer = lambda *shape: pl.BlockSpec((None,) + shape, lambda l, o: (l,) + (0,) * len(shape))
    return pl.pallas_call(
        functools.partial(_filter_kernel, L=L),
        grid=(DEPTH, 2),
        in_specs=[
            const((L, FEAT_PAD)), const((L, D_HY)), const((2 * L, L)),
            per_layer(FEAT_PAD, FILT_HID), per_layer(1, FILT_HID),
            per_layer(FILT_HID, FILT_HID), per_layer(1, FILT_HID),
            pl.BlockSpec((None, FILT_HID, 2 * D_HY), lambda l, o: (l, 0, o)),
            per_layer(1, FILT_HID),
        ],
        out_specs=[pl.BlockSpec((None, None, 2, L, D_HY), lambda l, o: (l, o, 0, 0, 0)),
                   pl.BlockSpec((None, None, 1, D_HY), lambda l, o: (l, o, 0, 0))],
        out_shape=[jax.ShapeDtypeStruct((DEPTH, 2, 2, L, D_HY), F32),
                   jax.ShapeDtypeStruct((DEPTH, 2, 1, D_HY), F32)],
        scratch_shapes=[pltpu.VMEM((L, FILT_HID), F32)],
        compiler_params=_params("arbitrary", "arbitrary"),
        name=f"hyena_filters_{L}",
    )(feats, decay, wf, w1, fb1.reshape(DEPTH, 1, FILT_HID), fw2, fb2.reshape(DEPTH, 1, FILT_HID),
      fw3, ffreq.reshape(DEPTH, 1, FILT_HID))


def _modulated_norm(x, g, mod_ref, q, col):
    r = _mod_row(q)
    shift = mod_ref[pl.ds(r, 1), col * D_MODEL:(col + 1) * D_MODEL]
    scale = mod_ref[pl.ds(r, 1), (col + 1) * D_MODEL:(col + 2) * D_MODEL]
    return (_rmsnorm(x, g) * (1.0 + scale) + shift).astype(BF16)


def _normalize_tile(i, x_ref, mod_ref, g_ref, h_scr, *, tm, col):
    for s in range(tm // CHUNK):
        rows = slice(s * CHUNK, (s + 1) * CHUNK)
        h_scr[rows, :] = _modulated_norm(x_ref[rows, :], g_ref[...], mod_ref, i * (tm // CHUNK) + s, col)


def _inproj_kernel(x_ref, mod_ref, g_ref, w_ref, u_ref, h_scr, *, tm):
    @pl.when(pl.program_id(1) == 0)
    def _():
        _normalize_tile(pl.program_id(0), x_ref, mod_ref, g_ref, h_scr, tm=tm, col=0)

    u_ref[...] = jnp.dot(h_scr[...], w_ref[...].astype(BF16), preferred_element_type=F32).astype(BF16)


def _input_projection(x, mod, norm_g, w_in, l, tm=2048, tn=1024):
    return pl.pallas_call(
        functools.partial(_inproj_kernel, tm=tm),
        grid=(N_TOK // tm, D_IN // tn),
        in_specs=[
            pl.BlockSpec((tm, D_MODEL), lambda i, j: (i, 0)),
            pl.BlockSpec((None, MOD_ROWS, 6 * D_MODEL), lambda i, j: (l, 0, 0)),
            pl.BlockSpec((None, 1, D_MODEL), lambda i, j: (l, 0, 0)),
            pl.BlockSpec((None, D_MODEL, tn), lambda i, j: (l, 0, j)),
        ],
        out_specs=pl.BlockSpec((tm, tn), lambda i, j: (i, j)),
        out_shape=jax.ShapeDtypeStruct((N_TOK, D_IN), BF16),
        scratch_shapes=[pltpu.VMEM((tm, D_MODEL), BF16)],
        compiler_params=_params("arbitrary", "arbitrary"),
        name="input_projection",
    )(x, mod, norm_g, w_in)


def _interleave(*task_lists):
    steps = max(len(tasks) for tasks in task_lists)
    done = [0] * len(task_lists)
    for step in range(1, steps + 1):
        for i, tasks in enumerate(task_lists):
            while done[i] < (step * len(tasks)) // steps:
                tasks[done[i]]()
                done[i] += 1


def _dot_by_k_tiles(lhs_ref, rhs_of, out):
    k_tile = min(K_PIECE, lhs_ref.shape[1])

    def piece(k):
        def run():
            cols = slice(k * k_tile, (k + 1) * k_tile)
            rhs = rhs_of()[cols]
            n_split = 2 if lhs_ref.shape[0] >= SPLIT_DOT_ROWS else 1
            rows = lhs_ref.shape[0] // n_split
            part = jnp.concatenate(
                [jnp.dot(lhs_ref[r * rows:(r + 1) * rows, cols], rhs, preferred_element_type=F32)
                 for r in range(n_split)], axis=0)
            out["acc"] = part if k == 0 else out["acc"] + part
        return run
    return [piece(k) for k in range(lhs_ref.shape[1] // k_tile)]


def _spectrum_product(spec, p_ref, nyq_ref, order, L):
    z_re, z_sn = spec[:L], spec[L:]
    p_re, p_im = p_ref[order, 0], p_ref[order, 1]
    y_re = z_re * p_re + z_sn * p_im
    y_sn = z_sn * p_re - z_re * p_im
    first = lax.broadcasted_iota(jnp.int32, (8, 1), 0) == 0
    y_sn_top = jnp.where(first, z_sn[:8] * nyq_ref[order], y_sn[:8])
    return jnp.concatenate([y_re, y_sn_top, y_sn[8:]], axis=0).astype(BF16)


def _hyena_tasks(L, v_ref, x1_ref, x2_ref, wv_ref, w1_ref, w2_ref, bv_ref, b1_ref, b2_ref, hb_ref,
                 p_ref, nyq_ref, wf_ref, wi_ref, o_ref):
    chunk = {}

    def prepare():
        def short_conv(u_ref, w_ref, b_ref):
            u = u_ref[...].astype(F32)
            w = w_ref[...]
            return (b_ref[...] + _shift_time(u, 1, L) * w[0:1] + u * w[1:2]
                    + _shift_time(u, -1, L) * w[2:3])
        chunk["v"] = short_conv(v_ref, wv_ref, bv_ref)
        chunk["x1"] = short_conv(x1_ref, w1_ref, b1_ref)
        chunk["x2"] = short_conv(x2_ref, w2_ref, b2_ref)
        chunk["bias"] = hb_ref[...]

    tasks = [prepare]
    for s in range(CHUNK // L):
        rows = slice(s * L, (s + 1) * L)
        seq = {}

        def begin(seq=seq, rows=rows):
            seq["in0"] = chunk["v"][rows]
            seq["in0_bf16"] = seq["in0"].astype(BF16)

        tasks.append(begin)
        for order in (0, 1):
            spec, conv = {}, {}
            tasks += _dot_by_k_tiles(wf_ref, lambda seq=seq, order=order: seq[f"in{order}_bf16"], spec)

            def pointwise(seq=seq, spec=spec, order=order):
                seq[f"y{order}"] = _spectrum_product(spec["acc"], p_ref, nyq_ref, order, L)

            tasks.append(pointwise)
            tasks += _dot_by_k_tiles(wi_ref, lambda seq=seq, order=order: seq[f"y{order}"], conv)

            def gate(seq=seq, conv=conv, order=order, rows=rows):
                x = chunk["x1" if order == 0 else "x2"][rows]
                z = seq[f"in{order}"]
                out = x * (conv["acc"] + chunk["bias"][order:order + 1] * z)
                if order == 0:
                    seq["in1"] = out
                    seq["in1_bf16"] = out.astype(BF16)
                else:
                    o_ref[rows, :] = out.astype(BF16)

            tasks.append(gate)
    return tasks


def _fnet_tasks(L, u_ref, cs_ref, f_ref, o_ref):
    chunk = {}

    def prepare():
        x = u_ref[...].astype(BF16)
        parts_c, parts_s = [], []
        for g in range(x.shape[1] // FN_GROUP):
            r = jnp.dot(x[:, g * FN_GROUP:(g + 1) * FN_GROUP], cs_ref[...], preferred_element_type=F32)
            parts_c.append(r[:, :FN_GROUP])
            parts_s.append(r[:, FN_GROUP:])
        chunk["xc"] = jnp.concatenate(parts_c, axis=1).astype(BF16)
        chunk["xs"] = jnp.concatenate(parts_s, axis=1).astype(BF16)

    tasks = [prepare]
    norm = 1.0 / math.sqrt(L * FN_GROUP)
    for s in range(CHUNK // L):
        rows = slice(s * L, (s + 1) * L)
        seq, out = {}, {}

        def stack(seq=seq, rows=rows):
            seq["stacked"] = jnp.concatenate([chunk["xc"][rows], chunk["xs"][rows]], axis=0)

        def finish(out=out, rows=rows):
            o_ref[rows, :] = (out["acc"] * norm).astype(BF16)

        tasks += [stack] + _dot_by_k_tiles(f_ref, lambda seq=seq: seq["stacked"], out) + [finish]
    return tasks


def _scan_sequence(s, L, a_f, b_f, a_b, b_b, h0_ref, y_ref, st_ref):
    nb = L // 8
    tc = a_f.shape[1]
    s0 = s * L

    def local(j, carry):
        hf, pf, hb, pb = carry
        rf = pl.multiple_of(s0 + 8 * j, 8)
        rb = pl.multiple_of(s0 + L - 8 - 8 * j, 8)
        af = a_f[pl.ds(rf, 8), :]
        hf = af * hf + b_f[pl.ds(rf, 8), :]
        pf = af * pf
        b_f[pl.ds(rf, 8), :] = hf
        a_f[pl.ds(rf, 8), :] = pf
        ab = a_b[pl.ds(rb, 8), :]
        hb = ab * hb + b_b[pl.ds(rb, 8), :]
        pb = ab * pb
        b_b[pl.ds(rb, 8), :] = hb
        a_b[pl.ds(rb, 8), :] = pb
        return hf, pf, hb, pb

    zero = jnp.zeros((8, tc), F32)
    one = jnp.ones((8, tc), F32)
    hf, pf, hb, pb = lax.fori_loop(0, nb, local, (zero, one, zero, one), unroll=4)

    carry = h0_ref[s, 0:1, :]
    rows = []
    for k in range(8):
        rows.append(carry)
        carry = hf[k:k + 1] + pf[k:k + 1] * carry
    carry_f = jnp.concatenate(rows, axis=0)
    st_ref[s, 0:1, :] = carry
    carry = h0_ref[s, 1:2, :]
    rows = [None] * 8
    for k in range(7, -1, -1):
        rows[k] = carry
        carry = hb[k:k + 1] + pb[k:k + 1] * carry
    carry_b = jnp.concatenate(rows, axis=0)
    st_ref[s, 1:2, :] = carry

    sl = slice(s0, s0 + L)
    blocked = lambda ref: ref[sl, :].reshape(nb, 8, tc)
    h_sum = ((blocked(b_f) + blocked(a_f) * carry_f[None]) + (blocked(b_b) + blocked(a_b) * carry_b[None]))
    y_ref[sl, :] = h_sum.reshape(L, tc).astype(BF16)


def _rglru_gate_tasks(L, reset, ux_ref, cw_ref, cb_ref, wg_ref, br_ref, bi_ref, lam_ref,
                      a_f, b_f, a_b, b_b):
    chunk = {}

    def prepare():
        u = ux_ref[...].astype(F32)
        w = cw_ref[...]
        xr = (cb_ref[...] + _shift_time(u, 2, L) * w[0:1] + _shift_time(u, 1, L) * w[1:2]
              + u * w[2:3] + _shift_time(u, -1, L) * w[3:4])
        chunk["x"] = xr
        chunk["x_bf16"] = xr.astype(BF16)
        neg_lam = -lam_ref[...]
        softplus = jnp.maximum(neg_lam, 0.0) + jnp.log1p(jnp.exp(-jnp.abs(neg_lam)))
        chunk["rate"] = (-0.25 * RG_C) * softplus
        chunk["half_br"] = 0.5 * br_ref[...]
        chunk["half_bi"] = 0.5 * bi_ref[...]

    tasks = [prepare]
    for n in range(RG_TC // RG_BLOCK):
        cs = slice(n * RG_BLOCK, (n + 1) * RG_BLOCK)

        def project(n=n, cs=cs):
            chunk["gates", n] = jnp.dot(chunk["x_bf16"][:, cs], wg_ref[n].astype(BF16),
                                        preferred_element_type=F32)

        tasks.append(project)
        for d, (a_scr, b_scr) in enumerate(((a_f, b_f), (a_b, b_b))):
            for r0 in range(0, CHUNK, GATE_ROWS):
                def gate(n=n, cs=cs, d=d, a_scr=a_scr, b_scr=b_scr, r0=r0):
                    rows = slice(r0, r0 + GATE_ROWS)
                    gates = chunk["gates", n][rows]
                    c0 = 2 * d * RG_BLOCK
                    t_r = jnp.tanh(gates[:, c0:c0 + RG_BLOCK] + chunk["half_br"][d:d + 1, cs])
                    t_i = jnp.tanh(gates[:, c0 + RG_BLOCK:c0 + 2 * RG_BLOCK] + chunk["half_bi"][d:d + 1, cs])
                    rate = chunk["rate"][d:d + 1, cs]
                    th = jnp.tanh(rate * t_r + rate)
                    recip = 1.0 / (1.0 - th)
                    neg_th = -th
                    root = neg_th * lax.rsqrt(jnp.maximum(neg_th, TINY_F32))
                    half_mult = recip * root
                    if reset:
                        tpos = (r0 + lax.broadcasted_iota(jnp.int32, (GATE_ROWS, 1), 0)) & (L - 1)
                        half_mult = jnp.where(tpos == (0 if d == 0 else L - 1), 0.5, half_mult)
                    a_scr[rows, cs] = (1.0 + th) * recip
                    b_scr[rows, cs] = (half_mult * chunk["x"][rows, cs]) * (t_i + 1.0)

                tasks.append(gate)
    return tasks


def _mixers_kernel(hv_ref, hx1_ref, hx2_ref, fn_ref, ux_ref,
                   hwv_ref, hw1_ref, hw2_ref, hbv_ref, hb1_ref, hb2_ref, hbias_ref,
                   ps_ref, ns_ref, pl_ref, nl_ref, wfs_ref, wis_ref, wfl_ref, wil_ref,
                   cs_ref, fs_ref, fl_ref,
                   h0_ref, cw_ref, cb_ref, wg_ref, br_ref, bi_ref, lam_ref,
                   yh_ref, yf_ref, y_ref, st_ref, a_f, b_f, a_b, b_b):
    q = pl.program_id(1)

    def run(L, reset, p_ref, nyq_ref, wf_ref, wi_ref, f_ref):
        matmul_side = (_hyena_tasks(L, hv_ref, hx1_ref, hx2_ref, hwv_ref, hw1_ref, hw2_ref, hbv_ref,
                                    hb1_ref, hb2_ref, hbias_ref, p_ref, nyq_ref, wf_ref, wi_ref, yh_ref)
                       + _fnet_tasks(L, fn_ref, cs_ref, f_ref, yf_ref))
        vector_side = _rglru_gate_tasks(L, reset, ux_ref, cw_ref, cb_ref, wg_ref, br_ref, bi_ref,
                                        lam_ref, a_f, b_f, a_b, b_b)
        _interleave(matmul_side, vector_side)
        st_ref[...] = jnp.zeros(st_ref.shape, F32)
        for s in range(CHUNK // L):
            _scan_sequence(s, L, a_f, b_f, a_b, b_b, h0_ref, y_ref, st_ref)

    @pl.when(q < NQ_CTX)
    def _():
        run(SEQ, True, ps_ref, ns_ref, wfs_ref, wis_ref, fs_ref)

    @pl.when(q >= NQ_CTX)
    def _():
        run(DEC_SEQ, False, pl_ref, nl_ref, wfl_ref, wil_ref, fl_ref)


def _mixers(u, l, hy_conv_w, hy_conv_b, hy_bias, spectra_short, spectra_long, dft_short, dft_long,
            fnet_chan, fnet_short, fnet_long, h0, rg_conv_w, rg_conv_b, w_gates, b_r, b_i, lam):
    assert D_HY // HY_TC == D_FN // HY_TC == D_RG // RG_TC
    halves = D_HY // HY_TC
    p_short, nyq_short = spectra_short
    p_long, nyq_long = spectra_long
    ucol = lambda width, col0: pl.BlockSpec((CHUNK, width), lambda c, q: (q, col0 // width + c))
    hy_part = lambda rows, k: pl.BlockSpec((None, rows, HY_TC), lambda c, q: (l, 0, k * halves + c))
    hy_tile = lambda rows: pl.BlockSpec((None, rows, HY_TC), lambda c, q: (l, 0, c))
    rg_tile = lambda rows: pl.BlockSpec((None, rows, RG_TC), lambda c, q: (l, 0, c))
    const = lambda shape: pl.BlockSpec(shape, lambda c, q: (0, 0))
    planes = lambda L: pl.BlockSpec((None, 2, 2, L, HY_TC), lambda c, q: (l, 0, 0, 0, c))
    nyquist = pl.BlockSpec((None, 2, 1, HY_TC), lambda c, q: (l, 0, 0, c))
    state_spec = pl.BlockSpec((None, CTX_PER_CHUNK, 2, RG_TC), lambda c, q: (q, 0, 0, c))
    out_tile = lambda width: pl.BlockSpec((CHUNK, width), lambda c, q: (q, c))
    hy_conv_b = hy_conv_b.reshape(DEPTH, 1, 3 * D_HY)
    return pl.pallas_call(
        _mixers_kernel,
        grid=(halves, NQ),
        in_specs=[
            ucol(HY_TC, 0), ucol(HY_TC, D_HY), ucol(HY_TC, 2 * D_HY), ucol(HY_TC, COL_FN),
            ucol(RG_TC, COL_RX),
            hy_part(3, 0), hy_part(3, 1), hy_part(3, 2), hy_part(1, 0), hy_part(1, 1), hy_part(1, 2),
            hy_tile(2),
            planes(SEQ), nyquist, planes(DEC_SEQ), nyquist,
            const((2 * SEQ, SEQ)), const((SEQ, 2 * SEQ)),
            const((2 * DEC_SEQ, DEC_SEQ)), const((DEC_SEQ, 2 * DEC_SEQ)),
            const((FN_GROUP, 2 * FN_GROUP)), const((SEQ, 2 * SEQ)), const((DEC_SEQ, 2 * DEC_SEQ)),
            state_spec, rg_tile(4), rg_tile(1),
            pl.BlockSpec((None, RG_TC // RG_BLOCK, RG_BLOCK, 4 * RG_BLOCK), lambda c, q: (l, c, 0, 0)),
            rg_tile(2), rg_tile(2), rg_tile(2),
        ],
        out_specs=[out_tile(HY_TC), out_tile(HY_TC), out_tile(RG_TC), state_spec],
        out_shape=[jax.ShapeDtypeStruct((N_TOK, D_HY), BF16),
                   jax.ShapeDtypeStruct((N_TOK, D_FN), BF16),
                   jax.ShapeDtypeStruct((N_TOK, D_RG), BF16),
                   jax.ShapeDtypeStruct((NQ, CTX_PER_CHUNK, 2, D_RG), F32)],
        scratch_shapes=[pltpu.VMEM((CHUNK, RG_TC), F32)] * 4,
        compiler_params=_params("arbitrary", "arbitrary"),
        name="sequence_mixers",
    )(u, u, u, u, u, hy_conv_w, hy_conv_w, hy_conv_w, hy_conv_b, hy_conv_b, hy_conv_b, hy_bias,
      p_short, nyq_short, p_long, nyq_long, *dft_short, *dft_long, fnet_chan, fnet_short, fnet_long,
      h0, rg_conv_w, rg_conv_b.reshape(DEPTH, 1, D_RG), w_gates, b_r, b_i, lam)


def _merge_kernel(x_ref, ya_ref, yb_ref, hc_ref, uy_ref, ga_ref, gb_ref, gc_ref, mod_ref,
                  wa_ref, wb_ref, wc_ref, wo_ref, o_ref, *, tm):
    r = _mod_row(pl.program_id(0) // (CHUNK // tm))
    gate = mod_ref[pl.ds(r, 1), 2 * D_MODEL:3 * D_MODEL]
    project = lambda y, w_ref: jnp.dot(y, w_ref[...].astype(BF16), preferred_element_type=F32)
    ya = project(ya_ref[...], wa_ref)
    yb = project(yb_ref[...], wb_ref)
    rg = hc_ref[...].astype(F32) * _gelu_tanh(uy_ref[...].astype(F32))
    yc = project(rg.astype(BF16), wc_ref)
    ga, gb, gc = (_sigmoid(g_ref[...].astype(F32)) for g_ref in (ga_ref, gb_ref, gc_ref))
    mix = ga * ya + gb * yb + gc * yc
    o_ref[...] = x_ref[...] + gate * project(mix.astype(BF16), wo_ref)


def _merge(x, y_hy, y_fn, h_rg, u, mod, w_a, w_b, w_c, w_o, l, tm=512):
    rows = lambda width, col=0: pl.BlockSpec((tm, width), lambda i: (i, col))
    weight = lambda k: pl.BlockSpec((None, k, D_MODEL), lambda i: (l, 0, 0))
    g0 = COL_G // D_MODEL
    return pl.pallas_call(
        functools.partial(_merge_kernel, tm=tm),
        grid=(N_TOK // tm,),
        in_specs=[
            rows(D_MODEL), rows(D_HY), rows(D_FN), rows(D_RG), rows(D_RG, COL_RY // D_RG),
            rows(D_MODEL, g0), rows(D_MODEL, g0 + 1), rows(D_MODEL, g0 + 2),
            pl.BlockSpec((None, MOD_ROWS, 6 * D_MODEL), lambda i: (l, 0, 0)),
            weight(D_HY), weight(D_FN), weight(D_RG), weight(D_MODEL),
        ],
        out_specs=rows(D_MODEL),
        out_shape=jax.ShapeDtypeStruct((N_TOK, D_MODEL), F32),
        compiler_params=_params("arbitrary"),
        name="branch_merge",
    )(x, y_hy, y_fn, h_rg, u, u, u, u, mod, w_a, w_b, w_c, w_o)


def _ffn_kernel(x_ref, mod_ref, g_ref, wg_ref, wu_ref, wd_ref, fg_ref, o_ref, h_scr, *, tm, final_norm):
    i = pl.program_id(0)
    f = pl.program_id(1)
    sub = tm // CHUNK

    @pl.when(f == 0)
    def _():
        _normalize_tile(i, x_ref, mod_ref, g_ref, h_scr, tm=tm, col=3)
        o_ref[...] = jnp.zeros(o_ref.shape, F32)

    h = h_scr[...]
    gt = jnp.dot(h, wg_ref[...].astype(BF16), preferred_element_type=F32)
    up = jnp.dot(h, wu_ref[...].astype(BF16), preferred_element_type=F32)
    act = (gt * _sigmoid(gt)) * up
    o_ref[...] += jnp.dot(act.astype(BF16), wd_ref[...].astype(BF16), preferred_element_type=F32)

    @pl.when(f == pl.num_programs(1) - 1)
    def _():
        for s in range(sub):
            r = _mod_row(i * sub + s)
            gate = mod_ref[pl.ds(r, 1), 5 * D_MODEL:6 * D_MODEL]
            rows = slice(s * CHUNK, (s + 1) * CHUNK)
            out = x_ref[rows, :] + gate * o_ref[rows, :]
            if final_norm:
                out = _rmsnorm(out, fg_ref[...])
            o_ref[rows, :] = out


def _ffn(x, mod, norm_g, w_gu, w_down, final_g, l, final_norm, tm=2048, tf=256):
    nf = D_FF // tf
    return pl.pallas_call(
        functools.partial(_ffn_kernel, tm=tm, final_norm=final_norm),
        grid=(N_TOK // tm, nf),
        in_specs=[
            pl.BlockSpec((tm, D_MODEL), lambda i, f: (i, 0)),
            pl.BlockSpec((None, MOD_ROWS, 6 * D_MODEL), lambda i, f: (l, 0, 0)),
            pl.BlockSpec((None, 1, D_MODEL), lambda i, f: (l, 0, 0)),
            pl.BlockSpec((None, D_MODEL, tf), lambda i, f: (l, 0, f)),
            pl.BlockSpec((None, D_MODEL, tf), lambda i, f: (l, 0, nf + f)),
            pl.BlockSpec((None, tf, D_MODEL), lambda i, f: (l, f, 0)),
            pl.BlockSpec((1, D_MODEL), lambda i, f: (0, 0)),
        ],
        out_specs=pl.BlockSpec((tm, D_MODEL), lambda i, f: (i, 0)),
        out_shape=jax.ShapeDtypeStruct((N_TOK, D_MODEL), F32),
        scratch_shapes=[pltpu.VMEM((tm, D_MODEL), BF16)],
        compiler_params=_params("arbitrary", "arbitrary"),
        name="swiglu_ffn",
    )(x, mod, norm_g, w_gu, w_gu, w_down, final_g)


def kernel(x_prompt, x_sample, c, state_rglru, c_ctx, norm1_g, norm2_g, w_ada, b_ada, w_in,
           hy_conv_w, hy_conv_b, hy_f_w1, hy_f_b1, hy_f_w2, hy_f_b2, hy_f_w3, hy_f_freq, hy_bias,
           w_a, w_b, rg_conv_w, rg_conv_b, rg_wr, rg_br, rg_wi, rg_bi, rg_lam, w_c, w_o,
           w_gu, w_down, final_g):
    x = jnp.concatenate([_to_time_permuted(x_prompt, SEQ), _to_time_permuted(x_sample, DEC_SEQ)])
    cvec = jnp.concatenate([c_ctx[None, :], c, jnp.zeros((MOD_ROWS - 1 - DEC_BATCH, D_MODEL), F32)])
    mod = _modulation(cvec, w_ada, b_ada)

    dft_short = _as_bf16(*_hyena_dft_tables(SEQ))
    dft_long = _as_bf16(*_hyena_dft_tables(DEC_SEQ))
    fnet_short, fnet_chan = _as_bf16(*_fnet_tables(SEQ))
    fnet_long, = _as_bf16(_fnet_tables(DEC_SEQ)[0])
    filt = (hy_f_w1, hy_f_b1, hy_f_w2, hy_f_b2, hy_f_w3, hy_f_freq)
    p_short = _hyena_filters(SEQ, dft_short[0], *filt)
    p_long = _hyena_filters(DEC_SEQ, dft_long[0], *filt)

    w_gates = 0.5 * jnp.concatenate([rg_wr[:, 0], rg_wi[:, 0], rg_wr[:, 1], rg_wi[:, 1]], axis=-1)
    lat_h0 = jnp.pad(state_rglru.astype(F32).transpose(1, 0, 2, 3)[:, :, None],
                     ((0, 0), (0, 0), (0, CTX_PER_CHUNK - 1), (0, 0), (0, 0)))
    h0_all = jnp.concatenate([jnp.zeros((DEPTH, NQ_CTX, CTX_PER_CHUNK, 2, D_RG), F32), lat_h0], axis=1)
    norm1 = norm1_g.reshape(DEPTH, 1, D_MODEL)
    norm2 = norm2_g.reshape(DEPTH, 1, D_MODEL)
    final = final_g.reshape(1, D_MODEL)

    states = []
    for l in range(DEPTH):
        u = _input_projection(x, mod, norm1, w_in, l)
        y_hy, y_fn, h_rg, st = _mixers(
            u, l, hy_conv_w, hy_conv_b, hy_bias, p_short, p_long, dft_short, dft_long,
            fnet_chan, fnet_short, fnet_long, h0_all[l], rg_conv_w, rg_conv_b, w_gates,
            rg_br, rg_bi, rg_lam)
        states.append(st[:NQ_CTX].reshape(BATCH, 2, D_RG))
        x = _merge(x, y_hy, y_fn, h_rg, u, mod, w_a, w_b, w_c, w_o, l)
        x = _ffn(x, mod, norm2, w_gu, w_down, final, l, final_norm=(l == DEPTH - 1))

    y_prompt = _from_time_permuted(x[:N_CTX_TOK], BATCH, SEQ)
    y_sample = _from_time_permuted(x[N_CTX_TOK:], DEC_BATCH, DEC_SEQ)
    new_state = jnp.stack(states, axis=1).astype(x_prompt.dtype)
    return (y_prompt, y_sample, new_state)
```

```python
import functools
import math

import numpy as np
import jax
import jax.numpy as jnp
from jax import lax
from jax.experimental import pallas as pl
from jax.experimental.pallas import tpu as pltpu

F32 = jnp.float32
BF16 = jnp.bfloat16
HIGHEST = lax.Precision.HIGHEST

D_MODEL = 1024
BATCH = 16
SEQ = 256
DEPTH = 4
DEC_BATCH = 4
DEC_SEQ = 1024
EPS = 1e-6
TINY_F32 = float(np.finfo(np.float32).tiny)
D_HY = 512
N_BANDS = 8
FILT_EMB = 1 + 2 * N_BANDS
FILT_HID = 64
DECAY_SLOW = -math.log(1e-2) / 1.5
DECAY_FAST = -math.log(1e-2) / 0.3
D_FN = 512
FN_GROUP = 128
N_FN_GROUPS = D_FN // FN_GROUP
D_RG = 1024
RG_BLOCK = 128
RG_C = 8.0
D_FF = -(-8 * D_MODEL // (3 * 256)) * 256
D_IN = 3 * D_HY + D_FN + 2 * D_RG + 3 * D_MODEL
COL_FN = 3 * D_HY
COL_RX = COL_FN + D_FN
COL_RY = COL_RX + D_RG
COL_G = COL_RY + D_RG

CHUNK = 1024
N_CTX_TOK = BATCH * SEQ
N_LAT_TOK = DEC_BATCH * DEC_SEQ
N_TOK = N_CTX_TOK + N_LAT_TOK
NQ_CTX = N_CTX_TOK // CHUNK
NQ = N_TOK // CHUNK
CTX_PER_CHUNK = CHUNK // SEQ
MOD_ROWS = 8
FEAT_PAD = 128
HY_TC = 256
RG_TC = 512
K_PIECE = 1024
GATE_ROWS = 256
SPLIT_DOT_ROWS = 1024
VMEM_LIMIT = 60 * 1024 * 1024

assert DEC_SEQ == CHUNK and CHUNK % SEQ == 0 and N_CTX_TOK % CHUNK == 0
assert SEQ % 16 == 0 and DEC_SEQ % 16 == 0
assert 1 + DEC_BATCH <= MOD_ROWS


def _params(*sem):
    return pltpu.CompilerParams(dimension_semantics=sem, vmem_limit_bytes=VMEM_LIMIT)


def _mod_row(q):
    return jnp.maximum(q - (NQ_CTX - 1), 0)


def _rmsnorm(x, g):
    return x * lax.rsqrt(jnp.mean(x * x, axis=-1, keepdims=True) + EPS) * g


def _sigmoid(x):
    return 0.5 * jnp.tanh(0.5 * x) + 0.5


def _dot_split3(a, w):
    rows = a.shape[0]
    a_hi = a.astype(BF16)
    a_lo = (a - a_hi.astype(F32)).astype(BF16)
    w_hi = w.astype(BF16)
    w_lo = (w - w_hi.astype(F32)).astype(BF16)
    heads = jnp.dot(jnp.concatenate([a_hi, a_lo], axis=0), w_hi, preferred_element_type=F32)
    return (heads[:rows] + heads[rows:]) + jnp.dot(a_hi, w_lo, preferred_element_type=F32)


def _gelu_tanh(x):
    inner = x * (math.sqrt(2.0 / math.pi) * 0.044715 * (x * x) + math.sqrt(2.0 / math.pi))
    half = 0.5 * x
    return half * jnp.tanh(inner) + half


def _time_of_row(L):
    p = np.arange(L)
    return (p % 8) * (L // 8) + p // 8


def _to_time_permuted(x, L):
    b = x.shape[0]
    return x.reshape(b, 8, L // 8, x.shape[-1]).transpose(0, 2, 1, 3).reshape(b * L, x.shape[-1])


def _from_time_permuted(x, b, L):
    return x.reshape(b, L // 8, 8, x.shape[-1]).transpose(0, 2, 1, 3).reshape(b, L, x.shape[-1])


def _shift_time(x, k, L):
    sub = lax.broadcasted_iota(jnp.int32, (8, 1), 0)
    out = []
    for s0 in range(0, x.shape[0], L):
        xs = x[s0:s0 + L]
        if k > 0:
            wrap = [jnp.where(sub == 0, 0.0, pltpu.roll(xs[L - 8 * (k - i):L - 8 * (k - i - 1)], 1, axis=0))
                    for i in range(k)]
            out += wrap + [xs[:L - 8 * k]]
        else:
            wrap = [jnp.where(sub == 7, 0.0, pltpu.roll(xs[8 * i:8 * (i + 1)], 7, axis=0))
                    for i in range(-k)]
            out += [xs[-8 * k:]] + wrap
    return jnp.concatenate(out, axis=0)


def _as_bf16(*tables):
    return tuple(jnp.asarray(t, dtype=F32).astype(BF16) for t in tables)


def _angle_table(n_rows, n_cols, period):
    prod = np.outer(np.arange(n_rows, dtype=np.int64), np.arange(n_cols, dtype=np.int64)) % period
    return 2.0 * np.pi * prod.astype(np.float64) / period


@functools.lru_cache(maxsize=None)
def _hyena_dft_tables(L):
    n = 2 * L
    ang = _angle_table(L, L, n)
    alt = np.where(np.arange(L) % 2 == 0, 1.0, -1.0)
    cos_f, sin_f = np.cos(ang), np.sin(ang)
    sin_f[0, :] = alt
    fwd = np.concatenate([cos_f, sin_f], axis=0)
    cos_i, sin_i = 2.0 / n * np.cos(ang), 2.0 / n * np.sin(ang)
    cos_i[:, 0] = 1.0 / n
    sin_i[:, 0] = alt / n
    inv = np.concatenate([cos_i, sin_i], axis=1)
    perm = _time_of_row(L)
    return fwd[:, perm].astype(np.float32), inv[perm, :].astype(np.float32)


@functools.lru_cache(maxsize=None)
def _fnet_tables(L):
    perm = _time_of_row(L)
    ang = _angle_table(L, L, L)[perm][:, perm]
    seq = np.concatenate([np.cos(ang), -np.sin(ang)], axis=1)
    ang_c = _angle_table(FN_GROUP, FN_GROUP, FN_GROUP)
    chan = np.concatenate([np.cos(ang_c), np.sin(ang_c)], axis=1)
    return seq.astype(np.float32), chan.astype(np.float32)


@functools.lru_cache(maxsize=None)
def _filter_tables(L):
    t = np.arange(L, dtype=np.float32) / np.float32(L)
    ang = 2.0 * np.pi * t[:, None].astype(np.float64) * np.arange(1, N_BANDS + 1, dtype=np.float64)
    feats = np.zeros((L, FEAT_PAD), np.float64)
    feats[:, 0] = t
    feats[:, 1:1 + N_BANDS] = np.sin(ang)
    feats[:, 1 + N_BANDS:FILT_EMB] = np.cos(ang)
    deltas = np.linspace(DECAY_SLOW, DECAY_FAST, D_HY, dtype=np.float32).astype(np.float64)
    decay = np.exp(-t[:, None].astype(np.float64) * deltas)
    perm = _time_of_row(L)
    return jnp.asarray(feats[perm], dtype=F32), jnp.asarray(decay[perm], dtype=F32)


def _mod_kernel(c_ref, w_ref, b_ref, o_ref):
    cv = c_ref[...]
    part = _dot_split3(cv * _sigmoid(cv), w_ref[...])

    @pl.when(pl.program_id(1) == 0)
    def _():
        o_ref[...] = part + b_ref[...]

    @pl.when(pl.program_id(1) != 0)
    def _():
        o_ref[...] += part


def _modulation(cvec, w_ada, b_ada):
    tk = 256
    nk = D_MODEL // tk
    c_slabs = cvec.reshape(MOD_ROWS, nk, tk).transpose(1, 0, 2)
    return pl.pallas_call(
        _mod_kernel,
        grid=(DEPTH, nk),
        in_specs=[
            pl.BlockSpec((None, MOD_ROWS, tk), lambda l, k: (k, 0, 0)),
            pl.BlockSpec((None, tk, 6 * D_MODEL), lambda l, k: (l, k, 0)),
            pl.BlockSpec((None, 1, 6 * D_MODEL), lambda l, k: (l, 0, 0)),
        ],
        out_specs=pl.BlockSpec((None, MOD_ROWS, 6 * D_MODEL), lambda l, k: (l, 0, 0)),
        out_shape=jax.ShapeDtypeStruct((DEPTH, MOD_ROWS, 6 * D_MODEL), F32),
        compiler_params=_params("arbitrary", "arbitrary"),
        name="adaln_modulation",
    )(c_slabs, w_ada, b_ada.reshape(DEPTH, 1, 6 * D_MODEL))


def _filter_kernel(feats_ref, decay_ref, wf_ref, w1_ref, b1_ref, w2_ref, b2_ref, w3_ref, fq_ref,
                   p_ref, nyq_ref, hid_scr, *, L):
    @pl.when(pl.program_id(1) == 0)
    def _():
        freq = fq_ref[...]
        h = jnp.sin(freq * (jnp.dot(feats_ref[...], w1_ref[...], precision=HIGHEST,
                                    preferred_element_type=F32) + b1_ref[...]))
        hid_scr[...] = jnp.sin(freq * (jnp.dot(h, w2_ref[...], precision=HIGHEST,
                                               preferred_element_type=F32) + b2_ref[...]))

    h = _dot_split3(hid_scr[...], w3_ref[...])
    decay = decay_ref[...]
    row = lax.broadcasted_iota(jnp.int32, (L, 1), 0)
    h_fwd = h[:, :D_HY] * decay
    h_bwd = jnp.where(row == 0, 0.0, h[:, D_HY:] * decay)
    ssq = jnp.sum(h_fwd * h_fwd, axis=0, keepdims=True) + jnp.sum(h_bwd * h_bwd, axis=0, keepdims=True)
    scale = lax.rsqrt(ssq + EPS)
    even = (h_fwd + h_bwd) * scale
    odd = (h_bwd - h_fwd) * scale
    k_re = jnp.dot(wf_ref[0:L, :], even.astype(BF16), preferred_element_type=F32)
    k_im = jnp.dot(wf_ref[L:2 * L, :], odd.astype(BF16), preferred_element_type=F32)
    alt = jnp.where(((row >> 3) & 1) == 0, 1.0, -1.0)
    k_nyq = jnp.sum(even * alt, axis=0, keepdims=True)
    p_ref[0] = k_re
    p_ref[1] = jnp.where(row == 0, 0.0, k_im)
    nyq_ref[...] = k_nyq


def _hyena_filters(L, wf, fw1, fb1, fw2, fb2, fw3, ffreq):
    feats, decay = _filter_tables(L)
    w1 = jnp.pad(fw1, ((0, 0), (0, FEAT_PAD - FILT_EMB), (0, 0)))
    const = lambda shape: pl.BlockSpec(shape, lambda l, o: (0,) * len(shape))
    per_layer = lambda *shape: pl.BlockSpec((None,) + shape, lambda l, o: (l,) + (0,) * len(shape))
    return pl.pallas_call(
        functools.partial(_filter_kernel, L=L),
        grid=(DEPTH, 2),
        in_specs=[
            const((L, FEAT_PAD)), const((L, D_HY)), const((2 * L, L)),
            per_layer(FEAT_PAD, FILT_HID), per_layer(1, FILT_HID),
            per_layer(FILT_HID, FILT_HID), per_layer(1, FILT_HID),
            pl.BlockSpec((None, FILT_HID, 2 * D_HY), lambda l, o: (l, 0, o)),
            per_layer(1, FILT_HID),
        ],
        out_specs=[pl.BlockSpec((None, None, 2, L, D_HY), lambda l, o: (l, o, 0, 0, 0)),
                   pl.BlockSpec((None, None, 1, D_HY), lambda l, o: (l, o, 0, 0))],
        out_shape=[jax.ShapeDtypeStruct((DEPTH, 2, 2, L, D_HY), F32),
                   jax.ShapeDtypeStruct((DEPTH, 2, 1, D_HY), F32)],
        scratch_shapes=[pltpu.VMEM((L, FILT_HID), F32)],
        compiler_params=_params("arbitrary", "arbitrary"),
        name=f"hyena_filters_{L}",
    )(feats, decay, wf, w1, fb1.reshape(DEPTH, 1, FILT_HID), fw2, fb2.reshape(DEPTH, 1, FILT_HID),
      fw3, ffreq.reshape(DEPTH, 1, FILT_HID))


def _modulated_norm(x, g, mod_ref, q, col):
    r = _mod_row(q)
    shift = mod_ref[pl.ds(r, 1), col * D_MODEL:(col + 1) * D_MODEL]
    scale = mod_ref[pl.ds(r, 1), (col + 1) * D_MODEL:(col + 2) * D_MODEL]
    return (_rmsnorm(x, g) * (1.0 + scale) + shift).astype(BF16)


def _inproj_kernel(x_ref, mod_ref, g_ref, w_ref, u_ref, h_scr, *, tm):
    i = pl.program_id(0)
    project = lambda h, w: jnp.dot(h, w, preferred_element_type=F32).astype(BF16)

    @pl.when(pl.program_id(1) == 0)
    def _():
        w = w_ref[...].astype(BF16)
        for s in range(tm // CHUNK):
            rows = slice(s * CHUNK, (s + 1) * CHUNK)
            h = _modulated_norm(x_ref[rows, :], g_ref[...], mod_ref, i * (tm // CHUNK) + s, 0)
            h_scr[rows, :] = h
            u_ref[rows, :] = project(h, w)

    @pl.when(pl.program_id(1) != 0)
    def _():
        u_ref[...] = project(h_scr[...], w_ref[...].astype(BF16))


def _input_projection(x, mod, norm_g, w_in, l, tm=2048, tn=1024):
    return pl.pallas_call(
        functools.partial(_inproj_kernel, tm=tm),
        grid=(N_TOK // tm, D_IN // tn),
        in_specs=[
            pl.BlockSpec((tm, D_MODEL), lambda i, j: (i, 0)),
            pl.BlockSpec((None, MOD_ROWS, 6 * D_MODEL), lambda i, j: (l, 0, 0)),
            pl.BlockSpec((None, 1, D_MODEL), lambda i, j: (l, 0, 0)),
            pl.BlockSpec((None, D_MODEL, tn), lambda i, j: (l, 0, j)),
        ],
        out_specs=pl.BlockSpec((tm, tn), lambda i, j: (i, j)),
        out_shape=jax.ShapeDtypeStruct((N_TOK, D_IN), BF16),
        scratch_shapes=[pltpu.VMEM((tm, D_MODEL), BF16)],
        compiler_params=_params("arbitrary", "arbitrary"),
        name="input_projection",
    )(x, mod, norm_g, w_in)


def _interleave(*task_lists):
    steps = max(len(tasks) for tasks in task_lists)
    done = [0] * len(task_lists)
    for step in range(1, steps + 1):
        for i, tasks in enumerate(task_lists):
            while done[i] < (step * len(tasks)) // steps:
                tasks[done[i]]()
                done[i] += 1


def _dot_by_k_tiles(lhs_ref, rhs_of, out):
    k_tile = min(K_PIECE, lhs_ref.shape[1])

    def piece(k):
        def run():
            cols = slice(k * k_tile, (k + 1) * k_tile)
            rhs = rhs_of()[cols]
            n_split = 2 if lhs_ref.shape[0] >= SPLIT_DOT_ROWS else 1
            rows = lhs_ref.shape[0] // n_split
            part = jnp.concatenate(
                [jnp.dot(lhs_ref[r * rows:(r + 1) * rows, cols], rhs, preferred_element_type=F32)
                 for r in range(n_split)], axis=0)
            out["acc"] = part if k == 0 else out["acc"] + part
        return run
    return [piece(k) for k in range(lhs_ref.shape[1] // k_tile)]


def _spectrum_product(spec, p_ref, nyq_ref, order, L):
    z_re, z_sn = spec[:L], spec[L:]
    p_re, p_im = p_ref[order, 0], p_ref[order, 1]
    y_re = z_re * p_re + z_sn * p_im
    y_sn = z_sn * p_re - z_re * p_im
    first = lax.broadcasted_iota(jnp.int32, (8, 1), 0) == 0
    y_sn_top = jnp.where(first, z_sn[:8] * nyq_ref[order], y_sn[:8])
    return jnp.concatenate([y_re, y_sn_top, y_sn[8:]], axis=0).astype(BF16)


def _hyena_tasks(L, v_ref, x1_ref, x2_ref, wv_ref, w1_ref, w2_ref, bv_ref, b1_ref, b2_ref, hb_ref,
                 p_ref, nyq_ref, wf_ref, wi_ref, o_ref):
    chunk = {}

    def prepare():
        def short_conv(u_ref, w_ref, b_ref):
            u = u_ref[...].astype(F32)
            w = w_ref[...]
            return (b_ref[...] + _shift_time(u, 1, L) * w[0:1] + u * w[1:2]
                    + _shift_time(u, -1, L) * w[2:3])
        chunk["v"] = short_conv(v_ref, wv_ref, bv_ref)
        chunk["x1"] = short_conv(x1_ref, w1_ref, b1_ref)
        chunk["x2"] = short_conv(x2_ref, w2_ref, b2_ref)
        chunk["bias"] = hb_ref[...]

    tasks = [prepare]
    for s in range(CHUNK // L):
        rows = slice(s * L, (s + 1) * L)
        seq = {}

        def begin(seq=seq, rows=rows):
            seq["in0"] = chunk["v"][rows]
            seq["in0_bf16"] = seq["in0"].astype(BF16)

        tasks.append(begin)
        for order in (0, 1):
            spec, conv = {}, {}
            tasks += _dot_by_k_tiles(wf_ref, lambda seq=seq, order=order: seq[f"in{order}_bf16"], spec)

            def pointwise(seq=seq, spec=spec, order=order):
                seq[f"y{order}"] = _spectrum_product(spec["acc"], p_ref, nyq_ref, order, L)

            tasks.append(pointwise)
            tasks += _dot_by_k_tiles(wi_ref, lambda seq=seq, order=order: seq[f"y{order}"], conv)

            def gate(seq=seq, conv=conv, order=order, rows=rows):
                x = chunk["x1" if order == 0 else "x2"][rows]
                z = seq[f"in{order}"]
                out = x * (conv["acc"] + chunk["bias"][order:order + 1] * z)
                if order == 0:
                    seq["in1"] = out
                    seq["in1_bf16"] = out.astype(BF16)
                else:
                    o_ref[rows, :] = out.astype(BF16)

            tasks.append(gate)
    return tasks


def _fnet_tasks(L, u_ref, cs_ref, f_ref, o_ref):
    chunk = {}

    def prepare():
        x = u_ref[...].astype(BF16)
        parts_c, parts_s = [], []
        for g in range(x.shape[1] // FN_GROUP):
            r = jnp.dot(x[:, g * FN_GROUP:(g + 1) * FN_GROUP], cs_ref[...], preferred_element_type=F32)
            parts_c.append(r[:, :FN_GROUP])
            parts_s.append(r[:, FN_GROUP:])
        chunk["xc"] = jnp.concatenate(parts_c, axis=1).astype(BF16)
        chunk["xs"] = jnp.concatenate(parts_s, axis=1).astype(BF16)

    tasks = [prepare]
    norm = 1.0 / math.sqrt(L * FN_GROUP)
    for s in range(CHUNK // L):
        rows = slice(s * L, (s + 1) * L)
        seq, out = {}, {}

        def stack(seq=seq, rows=rows):
            seq["stacked"] = jnp.concatenate([chunk["xc"][rows], chunk["xs"][rows]], axis=0)

        def finish(out=out, rows=rows):
            o_ref[rows, :] = (out["acc"] * norm).astype(BF16)

        tasks += [stack] + _dot_by_k_tiles(f_ref, lambda seq=seq: seq["stacked"], out) + [finish]
    return tasks


def _scan_sequence(s, L, a_f, b_f, a_b, b_b, h0_ref, y_ref, st_ref):
    nb = L // 8
    tc = a_f.shape[1]
    s0 = s * L

    def local(j, carry):
        hf, pf, hb, pb = carry
        rf = pl.multiple_of(s0 + 8 * j, 8)
        rb = pl.multiple_of(s0 + L - 8 - 8 * j, 8)
        af = a_f[pl.ds(rf, 8), :]
        hf = af * hf + b_f[pl.ds(rf, 8), :]
        pf = af * pf
        b_f[pl.ds(rf, 8), :] = hf
        a_f[pl.ds(rf, 8), :] = pf
        ab = a_b[pl.ds(rb, 8), :]
        hb = ab * hb + b_b[pl.ds(rb, 8), :]
        pb = ab * pb
        b_b[pl.ds(rb, 8), :] = hb
        a_b[pl.ds(rb, 8), :] = pb
        return hf, pf, hb, pb

    zero = jnp.zeros((8, tc), F32)
    one = jnp.ones((8, tc), F32)
    hf, pf, hb, pb = lax.fori_loop(0, nb, local, (zero, one, zero, one), unroll=4)

    carry = h0_ref[s, 0:1, :]
    rows = []
    for k in range(8):
        rows.append(carry)
        carry = hf[k:k + 1] + pf[k:k + 1] * carry
    carry_f = jnp.concatenate(rows, axis=0)
    st_ref[s, 0:1, :] = carry
    carry = h0_ref[s, 1:2, :]
    rows = [None] * 8
    for k in range(7, -1, -1):
        rows[k] = carry
        carry = hb[k:k + 1] + pb[k:k + 1] * carry
    carry_b = jnp.concatenate(rows, axis=0)
    st_ref[s, 1:2, :] = carry

    sl = slice(s0, s0 + L)
    blocked = lambda ref: ref[sl, :].reshape(nb, 8, tc)
    h_sum = ((blocked(b_f) + blocked(a_f) * carry_f[None]) + (blocked(b_b) + blocked(a_b) * carry_b[None]))
    y_ref[sl, :] = h_sum.reshape(L, tc).astype(BF16)


def _rglru_gate_tasks(L, reset, ux_ref, cw_ref, cb_ref, wg_ref, br_ref, bi_ref, lam_ref,
                      a_f, b_f, a_b, b_b):
    chunk = {}

    def prepare():
        u = ux_ref[...].astype(F32)
        w = cw_ref[...]
        xr = (cb_ref[...] + _shift_time(u, 2, L) * w[0:1] + _shift_time(u, 1, L) * w[1:2]
              + u * w[2:3] + _shift_time(u, -1, L) * w[3:4])
        chunk["x"] = xr
        chunk["x_bf16"] = xr.astype(BF16)
        neg_lam = -lam_ref[...]
        softplus = jnp.maximum(neg_lam, 0.0) + jnp.log1p(jnp.exp(-jnp.abs(neg_lam)))
        chunk["rate"] = (-0.25 * RG_C) * softplus
        chunk["half_br"] = 0.5 * br_ref[...]
        chunk["half_bi"] = 0.5 * bi_ref[...]

    tasks = [prepare]
    for n in range(RG_TC // RG_BLOCK):
        cs = slice(n * RG_BLOCK, (n + 1) * RG_BLOCK)

        def project(n=n, cs=cs):
            chunk["gates", n] = jnp.dot(chunk["x_bf16"][:, cs], wg_ref[n].astype(BF16),
                                        preferred_element_type=F32)

        tasks.append(project)
        for d, (a_scr, b_scr) in enumerate(((a_f, b_f), (a_b, b_b))):
            for r0 in range(0, CHUNK, GATE_ROWS):
                def gate(n=n, cs=cs, d=d, a_scr=a_scr, b_scr=b_scr, r0=r0):
                    rows = slice(r0, r0 + GATE_ROWS)
                    gates = chunk["gates", n][rows]
                    c0 = 2 * d * RG_BLOCK
                    t_r = jnp.tanh(gates[:, c0:c0 + RG_BLOCK] + chunk["half_br"][d:d + 1, cs])
                    t_i = jnp.tanh(gates[:, c0 + RG_BLOCK:c0 + 2 * RG_BLOCK] + chunk["half_bi"][d:d + 1, cs])
                    rate = chunk["rate"][d:d + 1, cs]
                    th = jnp.tanh(rate * t_r + rate)
                    recip = 1.0 / (1.0 - th)
                    neg_th = -th
                    root = neg_th * lax.rsqrt(jnp.maximum(neg_th, TINY_F32))
                    half_mult = recip * root
                    if reset:
                        tpos = (r0 + lax.broadcasted_iota(jnp.int32, (GATE_ROWS, 1), 0)) & (L - 1)
                        half_mult = jnp.where(tpos == (0 if d == 0 else L - 1), 0.5, half_mult)
                    a_scr[rows, cs] = (1.0 + th) * recip
                    b_scr[rows, cs] = (half_mult * chunk["x"][rows, cs]) * (t_i + 1.0)

                tasks.append(gate)
    return tasks


def _mixers_kernel(hv_ref, hx1_ref, hx2_ref, fn_ref, ux_ref,
                   hwv_ref, hw1_ref, hw2_ref, hbv_ref, hb1_ref, hb2_ref, hbias_ref,
                   ps_ref, ns_ref, pl_ref, nl_ref, wfs_ref, wis_ref, wfl_ref, wil_ref,
                   cs_ref, fs_ref, fl_ref,
                   h0_ref, cw_ref, cb_ref, wg_ref, br_ref, bi_ref, lam_ref,
                   yh_ref, yf_ref, y_ref, st_ref, a_f, b_f, a_b, b_b):
    q = pl.program_id(1)

    def run(L, reset, p_ref, nyq_ref, wf_ref, wi_ref, f_ref):
        matmul_side = (_hyena_tasks(L, hv_ref, hx1_ref, hx2_ref, hwv_ref, hw1_ref, hw2_ref, hbv_ref,
                                    hb1_ref, hb2_ref, hbias_ref, p_ref, nyq_ref, wf_ref, wi_ref, yh_ref)
                       + _fnet_tasks(L, fn_ref, cs_ref, f_ref, yf_ref))
        vector_side = _rglru_gate_tasks(L, reset, ux_ref, cw_ref, cb_ref, wg_ref, br_ref, bi_ref,
                                        lam_ref, a_f, b_f, a_b, b_b)
        _interleave(matmul_side, vector_side)
        st_ref[...] = jnp.zeros(st_ref.shape, F32)
        for s in range(CHUNK // L):
            _scan_sequence(s, L, a_f, b_f, a_b, b_b, h0_ref, y_ref, st_ref)

    @pl.when(q < NQ_CTX)
    def _():
        run(SEQ, True, ps_ref, ns_ref, wfs_ref, wis_ref, fs_ref)

    @pl.when(q >= NQ_CTX)
    def _():
        run(DEC_SEQ, False, pl_ref, nl_ref, wfl_ref, wil_ref, fl_ref)


def _mixers(u, l, hy_conv_w, hy_conv_b, hy_bias, spectra_short, spectra_long, dft_short, dft_long,
            fnet_chan, fnet_short, fnet_long, h0, rg_conv_w, rg_conv_b, w_gates, b_r, b_i, lam):
    assert D_HY // HY_TC == D_FN // HY_TC == D_RG // RG_TC
    halves = D_HY // HY_TC
    p_short, nyq_short = spectra_short
    p_long, nyq_long = spectra_long
    ucol = lambda width, col0: pl.BlockSpec((CHUNK, width), lambda c, q: (q, col0 // width + c))
    hy_part = lambda rows, k: pl.BlockSpec((None, rows, HY_TC), lambda c, q: (l, 0, k * halves + c))
    hy_tile = lambda rows: pl.BlockSpec((None, rows, HY_TC), lambda c, q: (l, 0, c))
    rg_tile = lambda rows: pl.BlockSpec((None, rows, RG_TC), lambda c, q: (l, 0, c))
    const = lambda shape: pl.BlockSpec(shape, lambda c, q: (0, 0))
    planes = lambda L: pl.BlockSpec((None, 2, 2, L, HY_TC), lambda c, q: (l, 0, 0, 0, c))
    nyquist = pl.BlockSpec((None, 2, 1, HY_TC), lambda c, q: (l, 0, 0, c))
    state_spec = pl.BlockSpec((None, CTX_PER_CHUNK, 2, RG_TC), lambda c, q: (q, 0, 0, c))
    out_tile = lambda width: pl.BlockSpec((CHUNK, width), lambda c, q: (q, c))
    hy_conv_b = hy_conv_b.reshape(DEPTH, 1, 3 * D_HY)
    return pl.pallas_call(
        _mixers_kernel,
        grid=(halves, NQ),
        in_specs=[
            ucol(HY_TC, 0), ucol(HY_TC, D_HY), ucol(HY_TC, 2 * D_HY), ucol(HY_TC, COL_FN),
            ucol(RG_TC, COL_RX),
            hy_part(3, 0), hy_part(3, 1), hy_part(3, 2), hy_part(1, 0), hy_part(1, 1), hy_part(1, 2),
            hy_tile(2),
            planes(SEQ), nyquist, planes(DEC_SEQ), nyquist,
            const((2 * SEQ, SEQ)), const((SEQ, 2 * SEQ)),
            const((2 * DEC_SEQ, DEC_SEQ)), const((DEC_SEQ, 2 * DEC_SEQ)),
            const((FN_GROUP, 2 * FN_GROUP)), const((SEQ, 2 * SEQ)), const((DEC_SEQ, 2 * DEC_SEQ)),
            state_spec, rg_tile(4), rg_tile(1),
            pl.BlockSpec((None, RG_TC // RG_BLOCK, RG_BLOCK, 4 * RG_BLOCK), lambda c, q: (l, c, 0, 0)),
            rg_tile(2), rg_tile(2), rg_tile(2),
        ],
        out_specs=[out_tile(HY_TC), out_tile(HY_TC), out_tile(RG_TC), state_spec],
        out_shape=[jax.ShapeDtypeStruct((N_TOK, D_HY), BF16),
                   jax.ShapeDtypeStruct((N_TOK, D_FN), BF16),
                   jax.ShapeDtypeStruct((N_TOK, D_RG), BF16),
                   jax.ShapeDtypeStruct((NQ, CTX_PER_CHUNK, 2, D_RG), F32)],
        scratch_shapes=[pltpu.VMEM((CHUNK, RG_TC), F32)] * 4,
        compiler_params=_params("arbitrary", "arbitrary"),
        name="sequence_mixers",
    )(u, u, u, u, u, hy_conv_w, hy_conv_w, hy_conv_w, hy_conv_b, hy_conv_b, hy_conv_b, hy_bias,
      p_short, nyq_short, p_long, nyq_long, *dft_short, *dft_long, fnet_chan, fnet_short, fnet_long,
      h0, rg_conv_w, rg_conv_b.reshape(DEPTH, 1, D_RG), w_gates, b_r, b_i, lam)


def _merge_kernel(x_ref, ya_ref, yb_ref, hc_ref, uy_ref, ga_ref, gb_ref, gc_ref, mod_ref,
                  wa_ref, wb_ref, wc_ref, wo_ref, o_ref, *, tm):
    r = _mod_row(pl.program_id(0) // (CHUNK // tm))
    gate = mod_ref[pl.ds(r, 1), 2 * D_MODEL:3 * D_MODEL]
    project = lambda y, w_ref: jnp.dot(y, w_ref[...].astype(BF16), preferred_element_type=F32)
    ya = project(ya_ref[...], wa_ref)
    yb = project(yb_ref[...], wb_ref)
    rg = hc_ref[...].astype(F32) * _gelu_tanh(uy_ref[...].astype(F32))
    yc = project(rg.astype(BF16), wc_ref)
    ga, gb, gc = (_sigmoid(g_ref[...].astype(F32)) for g_ref in (ga_ref, gb_ref, gc_ref))
    mix = ga * ya + gb * yb + gc * yc
    o_ref[...] = x_ref[...] + gate * project(mix.astype(BF16), wo_ref)


def _merge(x, y_hy, y_fn, h_rg, u, mod, w_a, w_b, w_c, w_o, l, tm=512):
    rows = lambda width, col=0: pl.BlockSpec((tm, width), lambda i: (i, col))
    weight = lambda k: pl.BlockSpec((None, k, D_MODEL), lambda i: (l, 0, 0))
    g0 = COL_G // D_MODEL
    return pl.pallas_call(
        functools.partial(_merge_kernel, tm=tm),
        grid=(N_TOK // tm,),
        in_specs=[
            rows(D_MODEL), rows(D_HY), rows(D_FN), rows(D_RG), rows(D_RG, COL_RY // D_RG),
            rows(D_MODEL, g0), rows(D_MODEL, g0 + 1), rows(D_MODEL, g0 + 2),
            pl.BlockSpec((None, MOD_ROWS, 6 * D_MODEL), lambda i: (l, 0, 0)),
            weight(D_HY), weight(D_FN), weight(D_RG), weight(D_MODEL),
        ],
        out_specs=rows(D_MODEL),
        out_shape=jax.ShapeDtypeStruct((N_TOK, D_MODEL), F32),
        compiler_params=_params("arbitrary"),
        name="branch_merge",
    )(x, y_hy, y_fn, h_rg, u, u, u, u, mod, w_a, w_b, w_c, w_o)


def _ffn_kernel(x_ref, mod_ref, g_ref, wg_ref, wu_ref, wd_ref, fg_ref, o_ref, h_scr, *, tm, final_norm):
    i = pl.program_id(0)
    f = pl.program_id(1)
    sub = tm // CHUNK

    def gated_mlp(h, wg, wu, wd):
        gt = jnp.dot(h, wg, preferred_element_type=F32)
        up = jnp.dot(h, wu, preferred_element_type=F32)
        act = (gt * _sigmoid(gt)) * up
        return jnp.dot(act.astype(BF16), wd, preferred_element_type=F32)

    @pl.when(f == 0)
    def _():
        wg, wu, wd = (w_ref[...].astype(BF16) for w_ref in (wg_ref, wu_ref, wd_ref))
        for s in range(sub):
            rows = slice(s * CHUNK, (s + 1) * CHUNK)
            h = _modulated_norm(x_ref[rows, :], g_ref[...], mod_ref, i * sub + s, 3)
            h_scr[rows, :] = h
            o_ref[rows, :] = gated_mlp(h, wg, wu, wd)

    @pl.when(f != 0)
    def _():
        o_ref[...] += gated_mlp(h_scr[...], *(w_ref[...].astype(BF16) for w_ref in (wg_ref, wu_ref, wd_ref)))

    @pl.when(f == pl.num_programs(1) - 1)
    def _():
        for s in range(sub):
            r = _mod_row(i * sub + s)
            gate = mod_ref[pl.ds(r, 1), 5 * D_MODEL:6 * D_MODEL]
            rows = slice(s * CHUNK, (s + 1) * CHUNK)
            out = x_ref[rows, :] + gate * o_ref[rows, :]
            if final_norm:
                out = _rmsnorm(out, fg_ref[...])
            o_ref[rows, :] = out


def _ffn(x, mod, norm_g, w_gu, w_down, final_g, l, final_norm, tm=2048, tf=256):
    nf = D_FF // tf
    return pl.pallas_call(
        functools.partial(_ffn_kernel, tm=tm, final_norm=final_norm),
        grid=(N_TOK // tm, nf),
        in_specs=[
            pl.BlockSpec((tm, D_MODEL), lambda i, f: (i, 0)),
            pl.BlockSpec((None, MOD_ROWS, 6 * D_MODEL), lambda i, f: (l, 0, 0)),
            pl.BlockSpec((None, 1, D_MODEL), lambda i, f: (l, 0, 0)),
            pl.BlockSpec((None, D_MODEL, tf), lambda i, f: (l, 0, f)),
            pl.BlockSpec((None, D_MODEL, tf), lambda i, f: (l, 0, nf + f)),
            pl.BlockSpec((None, tf, D_MODEL), lambda i, f: (l, f, 0)),
            pl.BlockSpec((1, D_MODEL), lambda i, f: (0, 0)),
        ],
        out_specs=pl.BlockSpec((tm, D_MODEL), lambda i, f: (i, 0)),
        out_shape=jax.ShapeDtypeStruct((N_TOK, D_MODEL), F32),
        scratch_shapes=[pltpu.VMEM((tm, D_MODEL), BF16)],
        compiler_params=_params("arbitrary", "arbitrary"),
        name="swiglu_ffn",
    )(x, mod, norm_g, w_gu, w_gu, w_down, final_g)


def kernel(x_prompt, x_sample, c, state_rglru, c_ctx, norm1_g, norm2_g, w_ada, b_ada, w_in,
           hy_conv_w, hy_conv_b, hy_f_w1, hy_f_b1, hy_f_w2, hy_f_b2, hy_f_w3, hy_f_freq, hy_bias,
           w_a, w_b, rg_conv_w, rg_conv_b, rg_wr, rg_br, rg_wi, rg_bi, rg_lam, w_c, w_o,
           w_gu, w_down, final_g):
    x = jnp.concatenate([_to_time_permuted(x_prompt, SEQ), _to_time_permuted(x_sample, DEC_SEQ)])
    cvec = jnp.concatenate([c_ctx[None, :], c, jnp.zeros((MOD_ROWS - 1 - DEC_BATCH, D_MODEL), F32)])
    mod = _modulation(cvec, w_ada, b_ada)

    dft_short = _as_bf16(*_hyena_dft_tables(SEQ))
    dft_long = _as_bf16(*_hyena_dft_tables(DEC_SEQ))
    fnet_short, fnet_chan = _as_bf16(*_fnet_tables(SEQ))
    fnet_long, = _as_bf16(_fnet_tables(DEC_SEQ)[0])
    filt = (hy_f_w1, hy_f_b1, hy_f_w2, hy_f_b2, hy_f_w3, hy_f_freq)
    p_short = _hyena_filters(SEQ, dft_short[0], *filt)
    p_long = _hyena_filters(DEC_SEQ, dft_long[0], *filt)

    w_gates = 0.5 * jnp.concatenate([rg_wr[:, 0], rg_wi[:, 0], rg_wr[:, 1], rg_wi[:, 1]], axis=-1)
    lat_h0 = jnp.pad(state_rglru.astype(F32).transpose(1, 0, 2, 3)[:, :, None],
                     ((0, 0), (0, 0), (0, CTX_PER_CHUNK - 1), (0, 0), (0, 0)))
    h0_all = jnp.concatenate([jnp.zeros((DEPTH, NQ_CTX, CTX_PER_CHUNK, 2, D_RG), F32), lat_h0], axis=1)
    norm1 = norm1_g.reshape(DEPTH, 1, D_MODEL)
    norm2 = norm2_g.reshape(DEPTH, 1, D_MODEL)
    final = final_g.reshape(1, D_MODEL)

    states = []
    for l in range(DEPTH):
        u = _input_projection(x, mod, norm1, w_in, l)
        y_hy, y_fn, h_rg, st = _mixers(
            u, l, hy_conv_w, hy_conv_b, hy_bias, p_short, p_long, dft_short, dft_long,
            fnet_chan, fnet_short, fnet_long, h0_all[l], rg_conv_w, rg_conv_b, w_gates,
            rg_br, rg_bi, rg_lam)
        states.append(st[:NQ_CTX].reshape(BATCH, 2, D_RG))
        x = _merge(x, y_hy, y_fn, h_rg, u, mod, w_a, w_b, w_c, w_o, l)
        x = _ffn(x, mod, norm2, w_gu, w_down, final, l, final_norm=(l == DEPTH - 1))

    y_prompt = _from_time_permuted(x[:N_CTX_TOK], BATCH, SEQ)
    y_sample = _from_time_permuted(x[N_CTX_TOK:], DEC_BATCH, DEC_SEQ)
    new_state = jnp.stack(states, axis=1).astype(x_prompt.dtype)
    return (y_prompt, y_sample, new_state)
```

```python
import functools
import math

import numpy as np
import jax
import jax.numpy as jnp
from jax import lax
from jax.experimental import pallas as pl
from jax.experimental.pallas import tpu as pltpu

F32 = jnp.float32
BF16 = jnp.bfloat16
HIGHEST = lax.Precision.HIGHEST

D_MODEL = 1024
BATCH = 16
SEQ = 256
DEPTH = 4
DEC_BATCH = 4
DEC_SEQ = 1024
EPS = 1e-6
TINY_F32 = float(np.finfo(np.float32).tiny)
D_HY = 512
N_BANDS = 8
FILT_EMB = 1 + 2 * N_BANDS
FILT_HID = 64
DECAY_SLOW = -math.log(1e-2) / 1.5
DECAY_FAST = -math.log(1e-2) / 0.3
D_FN = 512
FN_GROUP = 128
N_FN_GROUPS = D_FN // FN_GROUP
D_RG = 1024
RG_BLOCK = 128
RG_C = 8.0
D_FF = -(-8 * D_MODEL // (3 * 256)) * 256
D_IN = 3 * D_HY + D_FN + 2 * D_RG + 3 * D_MODEL
COL_FN = 3 * D_HY
COL_RX = COL_FN + D_FN
COL_RY = COL_RX + D_RG
COL_G = COL_RY + D_RG

CHUNK = 1024
N_CTX_TOK = BATCH * SEQ
N_LAT_TOK = DEC_BATCH * DEC_SEQ
N_TOK = N_CTX_TOK + N_LAT_TOK
NQ_CTX = N_CTX_TOK // CHUNK
NQ = N_TOK // CHUNK
CTX_PER_CHUNK = CHUNK // SEQ
MOD_ROWS = 8
FEAT_PAD = 128
HY_TC = 256
RG_TC = 512
K_PIECE = 1024
GATE_ROWS = 256
SPLIT_DOT_ROWS = 1024
VMEM_LIMIT = 60 * 1024 * 1024

assert DEC_SEQ == CHUNK and CHUNK % SEQ == 0 and N_CTX_TOK % CHUNK == 0
assert SEQ % 16 == 0 and DEC_SEQ % 16 == 0
assert 1 + DEC_BATCH <= MOD_ROWS


def _params(*sem):
    return pltpu.CompilerParams(dimension_semantics=sem, vmem_limit_bytes=VMEM_LIMIT)


def _mod_row(q):
    return jnp.maximum(q - (NQ_CTX - 1), 0)


def _rmsnorm(x, g):
    return x * lax.rsqrt(jnp.mean(x * x, axis=-1, keepdims=True) + EPS) * g


def _sigmoid(x):
    return 0.5 * jnp.tanh(0.5 * x) + 0.5


def _dot_split3(a, w):
    rows = a.shape[0]
    a_hi = a.astype(BF16)
    a_lo = (a - a_hi.astype(F32)).astype(BF16)
    w_hi = w.astype(BF16)
    w_lo = (w - w_hi.astype(F32)).astype(BF16)
    heads = jnp.dot(jnp.concatenate([a_hi, a_lo], axis=0), w_hi, preferred_element_type=F32)
    return (heads[:rows] + heads[rows:]) + jnp.dot(a_hi, w_lo, preferred_element_type=F32)


def _gelu_tanh(x):
    inner = x * (math.sqrt(2.0 / math.pi) * 0.044715 * (x * x) + math.sqrt(2.0 / math.pi))
    half = 0.5 * x
    return half * jnp.tanh(inner) + half


def _time_of_row(L):
    p = np.arange(L)
    return (p % 8) * (L // 8) + p // 8


def _to_time_permuted(x, L):
    b = x.shape[0]
    return x.reshape(b, 8, L // 8, x.shape[-1]).transpose(0, 2, 1, 3).reshape(b * L, x.shape[-1])


def _from_time_permuted(x, b, L):
    return x.reshape(b, L // 8, 8, x.shape[-1]).transpose(0, 2, 1, 3).reshape(b, L, x.shape[-1])


def _shift_time(x, k, L):
    sub = lax.broadcasted_iota(jnp.int32, (8, 1), 0)
    out = []
    for s0 in range(0, x.shape[0], L):
        xs = x[s0:s0 + L]
        if k > 0:
            wrap = [jnp.where(sub == 0, 0.0, pltpu.roll(xs[L - 8 * (k - i):L - 8 * (k - i - 1)], 1, axis=0))
                    for i in range(k)]
            out += wrap + [xs[:L - 8 * k]]
        else:
            wrap = [jnp.where(sub == 7, 0.0, pltpu.roll(xs[8 * i:8 * (i + 1)], 7, axis=0))
                    for i in range(-k)]
            out += [xs[-8 * k:]] + wrap
    return jnp.concatenate(out, axis=0)


def _as_bf16(*tables):
    return tuple(jnp.asarray(t, dtype=F32).astype(BF16) for t in tables)


def _angle_table(n_rows, n_cols, period):
    prod = np.outer(np.arange(n_rows, dtype=np.int64), np.arange(n_cols, dtype=np.int64)) % period
    return 2.0 * np.pi * prod.astype(np.float64) / period


@functools.lru_cache(maxsize=None)
def _hyena_dft_tables(L):
    n = 2 * L
    ang = _angle_table(L, L, n)
    alt = np.where(np.arange(L) % 2 == 0, 1.0, -1.0)
    cos_f, sin_f = np.cos(ang), np.sin(ang)
    sin_f[0, :] = alt
    fwd = np.concatenate([cos_f, sin_f], axis=0)
    cos_i, sin_i = 2.0 / n * np.cos(ang), 2.0 / n * np.sin(ang)
    cos_i[:, 0] = 1.0 / n
    sin_i[:, 0] = alt / n
    inv = np.concatenate([cos_i, sin_i], axis=1)
    perm = _time_of_row(L)
    return fwd[:, perm].astype(np.float32), inv[perm, :].astype(np.float32)


@functools.lru_cache(maxsize=None)
def _fnet_tables(L):
    perm = _time_of_row(L)
    ang = _angle_table(L, L, L)[perm][:, perm]
    seq = np.concatenate([np.cos(ang), -np.sin(ang)], axis=1)
    ang_c = _angle_table(FN_GROUP, FN_GROUP, FN_GROUP)
    chan = np.concatenate([np.cos(ang_c), np.sin(ang_c)], axis=1)
    return seq.astype(np.float32), chan.astype(np.float32)


@functools.lru_cache(maxsize=None)
def _filter_tables(L):
    t = np.arange(L, dtype=np.float32) / np.float32(L)
    ang = 2.0 * np.pi * t[:, None].astype(np.float64) * np.arange(1, N_BANDS + 1, dtype=np.float64)
    feats = np.zeros((L, FEAT_PAD), np.float64)
    feats[:, 0] = t
    feats[:, 1:1 + N_BANDS] = np.sin(ang)
    feats[:, 1 + N_BANDS:FILT_EMB] = np.cos(ang)
    deltas = np.linspace(DECAY_SLOW, DECAY_FAST, D_HY, dtype=np.float32).astype(np.float64)
    decay = np.exp(-t[:, None].astype(np.float64) * deltas)
    perm = _time_of_row(L)
    return jnp.asarray(feats[perm], dtype=F32), jnp.asarray(decay[perm], dtype=F32)


def _mod_kernel(ca_ref, cb_ref, wa_ref, wb_ref, b_ref, o_ref):
    silu = lambda c_ref: c_ref[...] * _sigmoid(c_ref[...])
    part = _dot_split3(silu(ca_ref), wa_ref[...]) + _dot_split3(silu(cb_ref), wb_ref[...])

    @pl.when(pl.program_id(1) == 0)
    def _():
        o_ref[...] = part + b_ref[...]

    @pl.when(pl.program_id(1) != 0)
    def _():
        o_ref[...] += part


def _modulation(cvec, w_ada, b_ada):
    tk = 256
    nk = D_MODEL // tk
    c_slabs = cvec.reshape(MOD_ROWS, nk, tk).transpose(1, 0, 2)
    c_spec = lambda half: pl.BlockSpec((None, MOD_ROWS, tk), lambda l, k: (2 * k + half, 0, 0))
    w_spec = lambda half: pl.BlockSpec((None, tk, 6 * D_MODEL), lambda l, k: (l, 2 * k + half, 0))
    return pl.pallas_call(
        _mod_kernel,
        grid=(DEPTH, nk // 2),
        in_specs=[
            c_spec(0), c_spec(1), w_spec(0), w_spec(1),
            pl.BlockSpec((None, 1, 6 * D_MODEL), lambda l, k: (l, 0, 0)),
        ],
        out_specs=pl.BlockSpec((None, MOD_ROWS, 6 * D_MODEL), lambda l, k: (l, 0, 0)),
        out_shape=jax.ShapeDtypeStruct((DEPTH, MOD_ROWS, 6 * D_MODEL), F32),
        compiler_params=_params("arbitrary", "arbitrary"),
        name="adaln_modulation",
    )(c_slabs, c_slabs, w_ada, w_ada, b_ada.reshape(DEPTH, 1, 6 * D_MODEL))


def _filter_kernel(feats_ref, decay_ref, wf_ref, w1_ref, b1_ref, w2_ref, b2_ref, w3_ref, fq_ref,
                   p_ref, nyq_ref, hid_scr, *, L):
    @pl.when(pl.program_id(1) == 0)
    def _():
        freq = fq_ref[...]
        h = jnp.sin(freq * (jnp.dot(feats_ref[...], w1_ref[...], precision=HIGHEST,
                                    preferred_element_type=F32) + b1_ref[...]))
        hid_scr[...] = jnp.sin(freq * (jnp.dot(h, w2_ref[...], precision=HIGHEST,
                                               preferred_element_type=F32) + b2_ref[...]))

    h = _dot_split3(hid_scr[...], w3_ref[...])
    decay = decay_ref[...]
    row = lax.broadcasted_iota(jnp.int32, (L, 1), 0)
    h_fwd = h[:, :D_HY] * decay
    h_bwd = jnp.where(row == 0, 0.0, h[:, D_HY:] * decay)
    ssq = jnp.sum(h_fwd * h_fwd, axis=0, keepdims=True) + jnp.sum(h_bwd * h_bwd, axis=0, keepdims=True)
    scale = lax.rsqrt(ssq + EPS)
    even = (h_fwd + h_bwd) * scale
    odd = (h_bwd - h_fwd) * scale
    k_re = jnp.dot(wf_ref[0:L, :], even.astype(BF16), preferred_element_type=F32)
    k_im = jnp.dot(wf_ref[L:2 * L, :], odd.astype(BF16), preferred_element_type=F32)
    alt = jnp.where(((row >> 3) & 1) == 0, 1.0, -1.0)
    k_nyq = jnp.sum(even * alt, axis=0, keepdims=True)
    p_ref[0] = k_re
    p_ref[1] = jnp.where(row == 0, 0.0, k_im)
    nyq_ref[...] = k_nyq


def _hyena_filters(L, wf, fw1, fb1, fw2, fb2, fw3, ffreq):
    feats, decay = _filter_tables(L)
    w1 = jnp.pad(fw1, ((0, 0), (0, FEAT_PAD - FILT_EMB), (0, 0)))
    const = lambda shape: pl.BlockSpec(shape, lambda l, o: (0,) * len(shape))
    per_layer = lambda *shape: pl.BlockSpec((None,) + shape, lambda l, o: (l,) + (0,) * len(shape))
    return pl.pallas_call(
        functools.partial(_filter_kernel, L=L),
        grid=(DEPTH, 2),
        in_specs=[
            const((L, FEAT_PAD)), const((L, D_HY)), const((2 * L, L)),
            per_layer(FEAT_PAD, FILT_HID), per_layer(1, FILT_HID),
            per_layer(FILT_HID, FILT_HID), per_layer(1, FILT_HID),
            pl.BlockSpec((None, FILT_HID, 2 * D_HY), lambda l, o: (l, 0, o)),
            per_layer(1, FILT_HID),
        ],
        out_specs=[pl.BlockSpec((None, None, 2, L, D_HY), lambda l, o: (l, o, 0, 0, 0)),
                   pl.BlockSpec((None, None, 1, D_HY), lambda l, o: (l, o, 0, 0))],
        out_shape=[jax.ShapeDtypeStruct((DEPTH, 2, 2, L, D_HY), F32),
                   jax.ShapeDtypeStruct((DEPTH, 2, 1, D_HY), F32)],
        scratch_shapes=[pltpu.VMEM((L, FILT_HID), F32)],
        compiler_params=_params("arbitrary", "arbitrary"),
        name=f"hyena_filters_{L}",
    )(feats, decay, wf, w1, fb1.reshape(DEPTH, 1, FILT_HID), fw2, fb2.reshape(DEPTH, 1, FILT_HID),
      fw3, ffreq.reshape(DEPTH, 1, FILT_HID))


def _modulated_norm(x, g, mod_ref, q, col):
    r = _mod_row(q)
    shift = mod_ref[pl.ds(r, 1), col * D_MODEL:(col + 1) * D_MODEL]
    scale = mod_ref[pl.ds(r, 1), (col + 1) * D_MODEL:(col + 2) * D_MODEL]
    return (_rmsnorm(x, g) * (1.0 + scale) + shift).astype(BF16)


def _inproj_kernel(x_ref, mod_ref, g_ref, w_ref, u_ref, h_scr, *, tm):
    i = pl.program_id(0)
    project = lambda h, w: jnp.dot(h, w, preferred_element_type=F32).astype(BF16)

    @pl.when(pl.program_id(1) == 0)
    def _():
        w = w_ref[...].astype(BF16)
        for s in range(tm // CHUNK):
            rows = slice(s * CHUNK, (s + 1) * CHUNK)
            h = _modulated_norm(x_ref[rows, :], g_ref[...], mod_ref, i * (tm // CHUNK) + s, 0)
            h_scr[rows, :] = h
            u_ref[rows, :] = project(h, w)

    @pl.when(pl.program_id(1) != 0)
    def _():
        u_ref[...] = project(h_scr[...], w_ref[...].astype(BF16))


def _input_projection(x, mod, norm_g, w_in, l, tm=2048, tn=1024):
    return pl.pallas_call(
        functools.partial(_inproj_kernel, tm=tm),
        grid=(N_TOK // tm, D_IN // tn),
        in_specs=[
            pl.BlockSpec((tm, D_MODEL), lambda i, j: (i, 0)),
            pl.BlockSpec((None, MOD_ROWS, 6 * D_MODEL), lambda i, j: (l, 0, 0)),
            pl.BlockSpec((None, 1, D_MODEL), lambda i, j: (l, 0, 0)),
            pl.BlockSpec((None, D_MODEL, tn), lambda i, j: (l, 0, j)),
        ],
        out_specs=pl.BlockSpec((tm, tn), lambda i, j: (i, j)),
        out_shape=jax.ShapeDtypeStruct((N_TOK, D_IN), BF16),
        scratch_shapes=[pltpu.VMEM((tm, D_MODEL), BF16)],
        compiler_params=_params("arbitrary", "arbitrary"),
        name="input_projection",
    )(x, mod, norm_g, w_in)


def _interleave(*task_lists):
    steps = max(len(tasks) for tasks in task_lists)
    done = [0] * len(task_lists)
    for step in range(1, steps + 1):
        for i, tasks in enumerate(task_lists):
            while done[i] < (step * len(tasks)) // steps:
                tasks[done[i]]()
                done[i] += 1


def _dot_by_k_tiles(lhs_ref, rhs_of, out):
    k_tile = min(K_PIECE, lhs_ref.shape[1])

    def piece(k):
        def run():
            cols = slice(k * k_tile, (k + 1) * k_tile)
            rhs = rhs_of()[cols]
            n_split = 2 if lhs_ref.shape[0] >= SPLIT_DOT_ROWS else 1
            rows = lhs_ref.shape[0] // n_split
            part = jnp.concatenate(
                [jnp.dot(lhs_ref[r * rows:(r + 1) * rows, cols], rhs, preferred_element_type=F32)
                 for r in range(n_split)], axis=0)
            out["acc"] = part if k == 0 else out["acc"] + part
        return run
    return [piece(k) for k in range(lhs_ref.shape[1] // k_tile)]


def _spectrum_product(spec, p_ref, nyq_ref, order, L):
    z_re, z_sn = spec[:L], spec[L:]
    p_re, p_im = p_ref[order, 0], p_ref[order, 1]
    y_re = z_re * p_re + z_sn * p_im
    y_sn = z_sn * p_re - z_re * p_im
    first = lax.broadcasted_iota(jnp.int32, (8, 1), 0) == 0
    y_sn_top = jnp.where(first, z_sn[:8] * nyq_ref[order], y_sn[:8])
    return jnp.concatenate([y_re, y_sn_top, y_sn[8:]], axis=0).astype(BF16)


def _hyena_tasks(L, v_ref, x1_ref, x2_ref, wv_ref, w1_ref, w2_ref, bv_ref, b1_ref, b2_ref, hb_ref,
                 p_ref, nyq_ref, wf_ref, wi_ref, o_ref):
    chunk = {}

    def prepare():
        def short_conv(u_ref, w_ref, b_ref):
            u = u_ref[...].astype(F32)
            w = w_ref[...]
            return (b_ref[...] + _shift_time(u, 1, L) * w[0:1] + u * w[1:2]
                    + _shift_time(u, -1, L) * w[2:3])
        chunk["v"] = short_conv(v_ref, wv_ref, bv_ref)
        chunk["x1"] = short_conv(x1_ref, w1_ref, b1_ref)
        chunk["x2"] = short_conv(x2_ref, w2_ref, b2_ref)
        chunk["bias"] = hb_ref[...]

    tasks = [prepare]
    for s in range(CHUNK // L):
        rows = slice(s * L, (s + 1) * L)
        seq = {}

        def begin(seq=seq, rows=rows):
            seq["in0"] = chunk["v"][rows]
            seq["in0_bf16"] = seq["in0"].astype(BF16)

        tasks.append(begin)
        for order in (0, 1):
            spec, conv = {}, {}
            tasks += _dot_by_k_tiles(wf_ref, lambda seq=seq, order=order: seq[f"in{order}_bf16"], spec)

            def pointwise(seq=seq, spec=spec, order=order):
                seq[f"y{order}"] = _spectrum_product(spec["acc"], p_ref, nyq_ref, order, L)

            tasks.append(pointwise)
            tasks += _dot_by_k_tiles(wi_ref, lambda seq=seq, order=order: seq[f"y{order}"], conv)

            def gate(seq=seq, conv=conv, order=order, rows=rows):
                x = chunk["x1" if order == 0 else "x2"][rows]
                z = seq[f"in{order}"]
                out = x * (conv["acc"] + chunk["bias"][order:order + 1] * z)
                if order == 0:
                    seq["in1"] = out
                    seq["in1_bf16"] = out.astype(BF16)
                else:
                    o_ref[rows, :] = out.astype(BF16)

            tasks.append(gate)
    return tasks


def _fnet_tasks(L, u_ref, cs_ref, f_ref, o_ref):
    chunk = {}

    def prepare():
        x = u_ref[...].astype(BF16)
        parts_c, parts_s = [], []
        for g in range(x.shape[1] // FN_GROUP):
            r = jnp.dot(x[:, g * FN_GROUP:(g + 1) * FN_GROUP], cs_ref[...], preferred_element_type=F32)
            parts_c.append(r[:, :FN_GROUP])
            parts_s.append(r[:, FN_GROUP:])
        chunk["xc"] = jnp.concatenate(parts_c, axis=1).astype(BF16)
        chunk["xs"] = jnp.concatenate(parts_s, axis=1).astype(BF16)

    tasks = [prepare]
    norm = 1.0 / math.sqrt(L * FN_GROUP)
    for s in range(CHUNK // L):
        rows = slice(s * L, (s + 1) * L)
        seq, out = {}, {}

        def stack(seq=seq, rows=rows):
            seq["stacked"] = jnp.concatenate([chunk["xc"][rows], chunk["xs"][rows]], axis=0)

        def finish(out=out, rows=rows):
            o_ref[rows, :] = (out["acc"] * norm).astype(BF16)

        tasks += [stack] + _dot_by_k_tiles(f_ref, lambda seq=seq: seq["stacked"], out) + [finish]
    return tasks


def _scan_sequence(s, L, a_f, b_f, a_b, b_b, h0_ref, y_ref, st_ref):
    nb = L // 8
    tc = a_f.shape[1]
    s0 = s * L

    def local(j, carry):
        hf, pf, hb, pb = carry
        rf = pl.multiple_of(s0 + 8 * j, 8)
        rb = pl.multiple_of(s0 + L - 8 - 8 * j, 8)
        af = a_f[pl.ds(rf, 8), :]
        hf = af * hf + b_f[pl.ds(rf, 8), :]
        pf = af * pf
        b_f[pl.ds(rf, 8), :] = hf
        a_f[pl.ds(rf, 8), :] = pf
        ab = a_b[pl.ds(rb, 8), :]
        hb = ab * hb + b_b[pl.ds(rb, 8), :]
        pb = ab * pb
        b_b[pl.ds(rb, 8), :] = hb
        a_b[pl.ds(rb, 8), :] = pb
        return hf, pf, hb, pb

    zero = jnp.zeros((8, tc), F32)
    one = jnp.ones((8, tc), F32)
    hf, pf, hb, pb = lax.fori_loop(0, nb, local, (zero, one, zero, one), unroll=4)

    carry = h0_ref[s, 0:1, :]
    rows = []
    for k in range(8):
        rows.append(carry)
        carry = hf[k:k + 1] + pf[k:k + 1] * carry
    carry_f = jnp.concatenate(rows, axis=0)
    st_ref[s, 0:1, :] = carry
    carry = h0_ref[s, 1:2, :]
    rows = [None] * 8
    for k in range(7, -1, -1):
        rows[k] = carry
        carry = hb[k:k + 1] + pb[k:k + 1] * carry
    carry_b = jnp.concatenate(rows, axis=0)
    st_ref[s, 1:2, :] = carry

    sl = slice(s0, s0 + L)
    blocked = lambda ref: ref[sl, :].reshape(nb, 8, tc)
    h_sum = ((blocked(b_f) + blocked(a_f) * carry_f[None]) + (blocked(b_b) + blocked(a_b) * carry_b[None]))
    y_ref[sl, :] = h_sum.reshape(L, tc).astype(BF16)


def _rglru_gate_tasks(L, reset, ux_ref, cw_ref, cb_ref, wg_ref, br_ref, bi_ref, lam_ref,
                      a_f, b_f, a_b, b_b):
    chunk = {}

    def prepare():
        u = ux_ref[...].astype(F32)
        w = cw_ref[...]
        xr = (cb_ref[...] + _shift_time(u, 2, L) * w[0:1] + _shift_time(u, 1, L) * w[1:2]
              + u * w[2:3] + _shift_time(u, -1, L) * w[3:4])
        chunk["x"] = xr
        chunk["x_bf16"] = xr.astype(BF16)
        neg_lam = -lam_ref[...]
        softplus = jnp.maximum(neg_lam, 0.0) + jnp.log1p(jnp.exp(-jnp.abs(neg_lam)))
        chunk["rate"] = (-0.25 * RG_C) * softplus
        chunk["half_br"] = 0.5 * br_ref[...]
        chunk["half_bi"] = 0.5 * bi_ref[...]

    tasks = [prepare]
    for n in range(RG_TC // RG_BLOCK):
        cs = slice(n * RG_BLOCK, (n + 1) * RG_BLOCK)

        def project(n=n, cs=cs):
            chunk["gates", n] = jnp.dot(chunk["x_bf16"][:, cs], wg_ref[n].astype(BF16),
                                        preferred_element_type=F32)

        tasks.append(project)
        for d, (a_scr, b_scr) in enumerate(((a_f, b_f), (a_b, b_b))):
            for r0 in range(0, CHUNK, GATE_ROWS):
                def gate(n=n, cs=cs, d=d, a_scr=a_scr, b_scr=b_scr, r0=r0):
                    rows = slice(r0, r0 + GATE_ROWS)
                    gates = chunk["gates", n][rows]
                    c0 = 2 * d * RG_BLOCK
                    t_r = jnp.tanh(gates[:, c0:c0 + RG_BLOCK] + chunk["half_br"][d:d + 1, cs])
                    t_i = jnp.tanh(gates[:, c0 + RG_BLOCK:c0 + 2 * RG_BLOCK] + chunk["half_bi"][d:d + 1, cs])
                    rate = chunk["rate"][d:d + 1, cs]
                    th = jnp.tanh(rate * t_r + rate)
                    recip = 1.0 / (1.0 - th)
                    neg_th = -th
                    root = neg_th * lax.rsqrt(jnp.maximum(neg_th, TINY_F32))
                    half_mult = recip * root
                    if reset:
                        tpos = (r0 + lax.broadcasted_iota(jnp.int32, (GATE_ROWS, 1), 0)) & (L - 1)
                        half_mult = jnp.where(tpos == (0 if d == 0 else L - 1), 0.5, half_mult)
                    a_scr[rows, cs] = (1.0 + th) * recip
                    b_scr[rows, cs] = (half_mult * chunk["x"][rows, cs]) * (t_i + 1.0)

                tasks.append(gate)
    return tasks


def _mixers_kernel(hv_ref, hx1_ref, hx2_ref, fn_ref, ux_ref,
                   hwv_ref, hw1_ref, hw2_ref, hbv_ref, hb1_ref, hb2_ref, hbias_ref,
                   ps_ref, ns_ref, pl_ref, nl_ref, wfs_ref, wis_ref, wfl_ref, wil_ref,
                   cs_ref, fs_ref, fl_ref,
                   h0_ref, cw_ref, cb_ref, wg_ref, br_ref, bi_ref, lam_ref,
                   yh_ref, yf_ref, y_ref, st_ref, a_f, b_f, a_b, b_b):
    q = pl.program_id(1)

    def run(L, reset, p_ref, nyq_ref, wf_ref, wi_ref, f_ref):
        matmul_side = (_hyena_tasks(L, hv_ref, hx1_ref, hx2_ref, hwv_ref, hw1_ref, hw2_ref, hbv_ref,
                                    hb1_ref, hb2_ref, hbias_ref, p_ref, nyq_ref, wf_ref, wi_ref, yh_ref)
                       + _fnet_tasks(L, fn_ref, cs_ref, f_ref, yf_ref))
        vector_side = _rglru_gate_tasks(L, reset, ux_ref, cw_ref, cb_ref, wg_ref, br_ref, bi_ref,
                                        lam_ref, a_f, b_f, a_b, b_b)
        _interleave(matmul_side, vector_side)
        st_ref[...] = jnp.zeros(st_ref.shape, F32)
        for s in range(CHUNK // L):
            _scan_sequence(s, L, a_f, b_f, a_b, b_b, h0_ref, y_ref, st_ref)

    @pl.when(q < NQ_CTX)
    def _():
        run(SEQ, True, ps_ref, ns_ref, wfs_ref, wis_ref, fs_ref)

    @pl.when(q >= NQ_CTX)
    def _():
        run(DEC_SEQ, False, pl_ref, nl_ref, wfl_ref, wil_ref, fl_ref)


def _mixers(u, l, hy_conv_w, hy_conv_b, hy_bias, spectra_short, spectra_long, dft_short, dft_long,
            fnet_chan, fnet_short, fnet_long, h0, rg_conv_w, rg_conv_b, w_gates, b_r, b_i, lam):
    assert D_HY // HY_TC == D_FN // HY_TC == D_RG // RG_TC
    halves = D_HY // HY_TC
    p_short, nyq_short = spectra_short
    p_long, nyq_long = spectra_long
    ucol = lambda width, col0: pl.BlockSpec((CHUNK, width), lambda c, q: (q, col0 // width + c))
    hy_part = lambda rows, k: pl.BlockSpec((None, rows, HY_TC), lambda c, q: (l, 0, k * halves + c))
    hy_tile = lambda rows: pl.BlockSpec((None, rows, HY_TC), lambda c, q: (l, 0, c))
    rg_tile = lambda rows: pl.BlockSpec((None, rows, RG_TC), lambda c, q: (l, 0, c))
    const = lambda shape: pl.BlockSpec(shape, lambda c, q: (0, 0))
    planes = lambda L: pl.BlockSpec((None, 2, 2, L, HY_TC), lambda c, q: (l, 0, 0, 0, c))
    nyquist = pl.BlockSpec((None, 2, 1, HY_TC), lambda c, q: (l, 0, 0, c))
    state_spec = pl.BlockSpec((None, CTX_PER_CHUNK, 2, RG_TC), lambda c, q: (q, 0, 0, c))
    out_tile = lambda width: pl.BlockSpec((CHUNK, width), lambda c, q: (q, c))
    hy_conv_b = hy_conv_b.reshape(DEPTH, 1, 3 * D_HY)
    return pl.pallas_call(
        _mixers_kernel,
        grid=(halves, NQ),
        in_specs=[
            ucol(HY_TC, 0), ucol(HY_TC, D_HY), ucol(HY_TC, 2 * D_HY), ucol(HY_TC, COL_FN),
            ucol(RG_TC, COL_RX),
            hy_part(3, 0), hy_part(3, 1), hy_part(3, 2), hy_part(1, 0), hy_part(1, 1), hy_part(1, 2),
            hy_tile(2),
            planes(SEQ), nyquist, planes(DEC_SEQ), nyquist,
            const((2 * SEQ, SEQ)), const((SEQ, 2 * SEQ)),
            const((2 * DEC_SEQ, DEC_SEQ)), const((DEC_SEQ, 2 * DEC_SEQ)),
            const((FN_GROUP, 2 * FN_GROUP)), const((SEQ, 2 * SEQ)), const((DEC_SEQ, 2 * DEC_SEQ)),
            state_spec, rg_tile(4), rg_tile(1),
            pl.BlockSpec((None, RG_TC // RG_BLOCK, RG_BLOCK, 4 * RG_BLOCK), lambda c, q: (l, c, 0, 0)),
            rg_tile(2), rg_tile(2), rg_tile(2),
        ],
        out_specs=[out_tile(HY_TC), out_tile(HY_TC), out_tile(RG_TC), state_spec],
        out_shape=[jax.ShapeDtypeStruct((N_TOK, D_HY), BF16),
                   jax.ShapeDtypeStruct((N_TOK, D_FN), BF16),
                   jax.ShapeDtypeStruct((N_TOK, D_RG), BF16),
                   jax.ShapeDtypeStruct((NQ, CTX_PER_CHUNK, 2, D_RG), F32)],
        scratch_shapes=[pltpu.VMEM((CHUNK, RG_TC), F32)] * 4,
        compiler_params=_params("arbitrary", "arbitrary"),
        name="sequence_mixers",
    )(u, u, u, u, u, hy_conv_w, hy_conv_w, hy_conv_w, hy_conv_b, hy_conv_b, hy_conv_b, hy_bias,
      p_short, nyq_short, p_long, nyq_long, *dft_short, *dft_long, fnet_chan, fnet_short, fnet_long,
      h0, rg_conv_w, rg_conv_b.reshape(DEPTH, 1, D_RG), w_gates, b_r, b_i, lam)


def _merge_kernel(x_ref, ya_ref, yb_ref, hc_ref, uy_ref, ga_ref, gb_ref, gc_ref, mod_ref,
                  wa_ref, wb_ref, wc_ref, wo_ref, o_ref, *, tm):
    r = _mod_row(pl.program_id(0) // (CHUNK // tm))
    gate = mod_ref[pl.ds(r, 1), 2 * D_MODEL:3 * D_MODEL]
    project = lambda y, w_ref: jnp.dot(y, w_ref[...].astype(BF16), preferred_element_type=F32)
    ya = project(ya_ref[...], wa_ref)
    yb = project(yb_ref[...], wb_ref)
    rg = hc_ref[...].astype(F32) * _gelu_tanh(uy_ref[...].astype(F32))
    yc = project(rg.astype(BF16), wc_ref)
    ga, gb, gc = (_sigmoid(g_ref[...].astype(F32)) for g_ref in (ga_ref, gb_ref, gc_ref))
    mix = ga * ya + gb * yb + gc * yc
    o_ref[...] = x_ref[...] + gate * project(mix.astype(BF16), wo_ref)


def _merge(x, y_hy, y_fn, h_rg, u, mod, w_a, w_b, w_c, w_o, l, tm=512):
    rows = lambda width, col=0: pl.BlockSpec((tm, width), lambda i: (i, col))
    weight = lambda k: pl.BlockSpec((None, k, D_MODEL), lambda i: (l, 0, 0))
    g0 = COL_G // D_MODEL
    return pl.pallas_call(
        functools.partial(_merge_kernel, tm=tm),
        grid=(N_TOK // tm,),
        in_specs=[
            rows(D_MODEL), rows(D_HY), rows(D_FN), rows(D_RG), rows(D_RG, COL_RY // D_RG),
            rows(D_MODEL, g0), rows(D_MODEL, g0 + 1), rows(D_MODEL, g0 + 2),
            pl.BlockSpec((None, MOD_ROWS, 6 * D_MODEL), lambda i: (l, 0, 0)),
            weight(D_HY), weight(D_FN), weight(D_RG), weight(D_MODEL),
        ],
        out_specs=rows(D_MODEL),
        out_shape=jax.ShapeDtypeStruct((N_TOK, D_MODEL), F32),
        compiler_params=_params("arbitrary"),
        name="branch_merge",
    )(x, y_hy, y_fn, h_rg, u, u, u, u, mod, w_a, w_b, w_c, w_o)


def _ffn_kernel(x_ref, mod_ref, g_ref, wg_ref, wu_ref, wd_ref, fg_ref, o_ref, h_scr, *, tm, final_norm):
    i = pl.program_id(0)
    f = pl.program_id(1)
    sub = tm // CHUNK

    def gated_mlp(h, wg, wu, wd):
        gt = jnp.dot(h, wg, preferred_element_type=F32)
        up = jnp.dot(h, wu, preferred_element_type=F32)
        act = (gt * _sigmoid(gt)) * up
        return jnp.dot(act.astype(BF16), wd, preferred_element_type=F32)

    @pl.when(f == 0)
    def _():
        wg, wu, wd = (w_ref[...].astype(BF16) for w_ref in (wg_ref, wu_ref, wd_ref))
        for s in range(sub):
            rows = slice(s * CHUNK, (s + 1) * CHUNK)
            h = _modulated_norm(x_ref[rows, :], g_ref[...], mod_ref, i * sub + s, 3)
            h_scr[rows, :] = h
            o_ref[rows, :] = gated_mlp(h, wg, wu, wd)

    last = pl.num_programs(1) - 1

    @pl.when((f != 0) & (f != last))
    def _():
        o_ref[...] += gated_mlp(h_scr[...], *(w_ref[...].astype(BF16) for w_ref in (wg_ref, wu_ref, wd_ref)))

    @pl.when(f == last)
    def _():
        wg, wu, wd = (w_ref[...].astype(BF16) for w_ref in (wg_ref, wu_ref, wd_ref))
        for s in range(sub):
            r = _mod_row(i * sub + s)
            gate = mod_ref[pl.ds(r, 1), 5 * D_MODEL:6 * D_MODEL]
            rows = slice(s * CHUNK, (s + 1) * CHUNK)
            out = x_ref[rows, :] + gate * (o_ref[rows, :] + gated_mlp(h_scr[rows, :], wg, wu, wd))
            if final_norm:
                out = _rmsnorm(out, fg_ref[...])
            o_ref[rows, :] = out


def _ffn(x, mod, norm_g, w_gu, w_down, final_g, l, final_norm, tm=2048, tf=256):
    nf = D_FF // tf
    return pl.pallas_call(
        functools.partial(_ffn_kernel, tm=tm, final_norm=final_norm),
        grid=(N_TOK // tm, nf),
        in_specs=[
            pl.BlockSpec((tm, D_MODEL), lambda i, f: (i, 0)),
            pl.BlockSpec((None, MOD_ROWS, 6 * D_MODEL), lambda i, f: (l, 0, 0)),
            pl.BlockSpec((None, 1, D_MODEL), lambda i, f: (l, 0, 0)),
            pl.BlockSpec((None, D_MODEL, tf), lambda i, f: (l, 0, f)),
            pl.BlockSpec((None, D_MODEL, tf), lambda i, f: (l, 0, nf + f)),
            pl.BlockSpec((None, tf, D_MODEL), lambda i, f: (l, f, 0)),
            pl.BlockSpec((1, D_MODEL), lambda i, f: (0, 0)),
        ],
        out_specs=pl.BlockSpec((tm, D_MODEL), lambda i, f: (i, 0)),
        out_shape=jax.ShapeDtypeStruct((N_TOK, D_MODEL), F32),
        scratch_shapes=[pltpu.VMEM((tm, D_MODEL), BF16)],
        compiler_params=_params("arbitrary", "arbitrary"),
        name="swiglu_ffn",
    )(x, mod, norm_g, w_gu, w_gu, w_down, final_g)


def kernel(x_prompt, x_sample, c, state_rglru, c_ctx, norm1_g, norm2_g, w_ada, b_ada, w_in,
           hy_conv_w, hy_conv_b, hy_f_w1, hy_f_b1, hy_f_w2, hy_f_b2, hy_f_w3, hy_f_freq, hy_bias,
           w_a, w_b, rg_conv_w, rg_conv_b, rg_wr, rg_br, rg_wi, rg_bi, rg_lam, w_c, w_o,
           w_gu, w_down, final_g):
    x = jnp.concatenate([_to_time_permuted(x_prompt, SEQ), _to_time_permuted(x_sample, DEC_SEQ)])
    cvec = jnp.concatenate([c_ctx[None, :], c, jnp.zeros((MOD_ROWS - 1 - DEC_BATCH, D_MODEL), F32)])
    mod = _modulation(cvec, w_ada, b_ada)

    dft_short = _as_bf16(*_hyena_dft_tables(SEQ))
    dft_long = _as_bf16(*_hyena_dft_tables(DEC_SEQ))
    fnet_short, fnet_chan = _as_bf16(*_fnet_tables(SEQ))
    fnet_long, = _as_bf16(_fnet_tables(DEC_SEQ)[0])
    filt = (hy_f_w1, hy_f_b1, hy_f_w2, hy_f_b2, hy_f_w3, hy_f_freq)
    p_short = _hyena_filters(SEQ, dft_short[0], *filt)
    p_long = _hyena_filters(DEC_SEQ, dft_long[0], *filt)

    w_gates = 0.5 * jnp.concatenate([rg_wr[:, 0], rg_wi[:, 0], rg_wr[:, 1], rg_wi[:, 1]], axis=-1)
    lat_h0 = jnp.pad(state_rglru.astype(F32).transpose(1, 0, 2, 3)[:, :, None],
                     ((0, 0), (0, 0), (0, CTX_PER_CHUNK - 1), (0, 0), (0, 0)))
    h0_all = jnp.concatenate([jnp.zeros((DEPTH, NQ_CTX, CTX_PER_CHUNK, 2, D_RG), F32), lat_h0], axis=1)
    norm1 = norm1_g.reshape(DEPTH, 1, D_MODEL)
    norm2 = norm2_g.reshape(DEPTH, 1, D_MODEL)
    final = final_g.reshape(1, D_MODEL)

    states = []
    for l in range(DEPTH):
        u = _input_projection(x, mod, norm1, w_in, l)
        y_hy, y_fn, h_rg, st = _mixers(
            u, l, hy_conv_w, hy_conv_b, hy_bias, p_short, p_long, dft_short, dft_long,
            fnet_chan, fnet_short, fnet_long, h0_all[l], rg_conv_w, rg_conv_b, w_gates,
            rg_br, rg_bi, rg_lam)
        states.append(st[:NQ_CTX].reshape(BATCH, 2, D_RG))
        x = _merge(x, y_hy, y_fn, h_rg, u, mod, w_a, w_b, w_c, w_o, l)
        x = _ffn(x, mod, norm2, w_gu, w_down, final, l, final_norm=(l == DEPTH - 1))

    y_prompt = _from_time_permuted(x[:N_CTX_TOK], BATCH, SEQ)
    y_sample = _from_time_permuted(x[N_CTX_TOK:], DEC_BATCH, DEC_SEQ)
    new_state = jnp.stack(states, axis=1).astype(x_prompt.dtype)
    return (y_prompt, y_sample, new_state)
```

```python
import functools
import math

import numpy as np
import jax
import jax.numpy as jnp
from jax import lax
from jax.experimental import pallas as pl
from jax.experimental.pallas import tpu as pltpu

F32 = jnp.float32
BF16 = jnp.bfloat16
HIGHEST = lax.Precision.HIGHEST

D_MODEL = 1024
BATCH = 16
SEQ = 256
DEPTH = 4
DEC_BATCH = 4
DEC_SEQ = 1024
EPS = 1e-6
TINY_F32 = float(np.finfo(np.float32).tiny)
D_HY = 512
N_BANDS = 8
FILT_EMB = 1 + 2 * N_BANDS
FILT_HID = 64
DECAY_SLOW = -math.log(1e-2) / 1.5
DECAY_FAST = -math.log(1e-2) / 0.3
D_FN = 512
FN_GROUP = 128
N_FN_GROUPS = D_FN // FN_GROUP
D_RG = 1024
RG_BLOCK = 128
RG_C = 8.0
D_FF = -(-8 * D_MODEL // (3 * 256)) * 256
D_IN = 3 * D_HY + D_FN + 2 * D_RG + 3 * D_MODEL
COL_FN = 3 * D_HY
COL_RX = COL_FN + D_FN
COL_RY = COL_RX + D_RG
COL_G = COL_RY + D_RG

CHUNK = 1024
N_CTX_TOK = BATCH * SEQ
N_LAT_TOK = DEC_BATCH * DEC_SEQ
N_TOK = N_CTX_TOK + N_LAT_TOK
NQ_CTX = N_CTX_TOK // CHUNK
NQ = N_TOK // CHUNK
CTX_PER_CHUNK = CHUNK // SEQ
MOD_ROWS = 8
FEAT_PAD = 128
HY_TC = 256
RG_TC = 512
K_PIECE = 1024
GATE_ROWS = 256
SPLIT_DOT_ROWS = 1024
VMEM_LIMIT = 60 * 1024 * 1024

assert DEC_SEQ == CHUNK and CHUNK % SEQ == 0 and N_CTX_TOK % CHUNK == 0
assert SEQ % 16 == 0 and DEC_SEQ % 16 == 0
assert 1 + DEC_BATCH <= MOD_ROWS


def _params(*sem):
    return pltpu.CompilerParams(dimension_semantics=sem, vmem_limit_bytes=VMEM_LIMIT)


def _mod_row(q):
    return jnp.maximum(q - (NQ_CTX - 1), 0)


def _rmsnorm(x, g):
    return x * lax.rsqrt(jnp.mean(x * x, axis=-1, keepdims=True) + EPS) * g


def _sigmoid(x):
    return 0.5 * jnp.tanh(0.5 * x) + 0.5


def _dot_split3(a, w):
    rows = a.shape[0]
    a_hi = a.astype(BF16)
    a_lo = (a - a_hi.astype(F32)).astype(BF16)
    w_hi = w.astype(BF16)
    w_lo = (w - w_hi.astype(F32)).astype(BF16)
    heads = jnp.dot(jnp.concatenate([a_hi, a_lo], axis=0), w_hi, preferred_element_type=F32)
    return (heads[:rows] + heads[rows:]) + jnp.dot(a_hi, w_lo, preferred_element_type=F32)


def _gelu_tanh(x):
    inner = x * (math.sqrt(2.0 / math.pi) * 0.044715 * (x * x) + math.sqrt(2.0 / math.pi))
    half = 0.5 * x
    return half * jnp.tanh(inner) + half


def _time_of_row(L):
    p = np.arange(L)
    return (p % 8) * (L // 8) + p // 8


def _to_time_permuted(x, L):
    b = x.shape[0]
    return x.reshape(b, 8, L // 8, x.shape[-1]).transpose(0, 2, 1, 3).reshape(b * L, x.shape[-1])


def _from_time_permuted(x, b, L):
    return x.reshape(b, L // 8, 8, x.shape[-1]).transpose(0, 2, 1, 3).reshape(b, L, x.shape[-1])


def _shift_time(x, k, L):
    sub = lax.broadcasted_iota(jnp.int32, (8, 1), 0)
    out = []
    for s0 in range(0, x.shape[0], L):
        xs = x[s0:s0 + L]
        if k > 0:
            wrap = [jnp.where(sub == 0, 0.0, pltpu.roll(xs[L - 8 * (k - i):L - 8 * (k - i - 1)], 1, axis=0))
                    for i in range(k)]
            out += wrap + [xs[:L - 8 * k]]
        else:
            wrap = [jnp.where(sub == 7, 0.0, pltpu.roll(xs[8 * i:8 * (i + 1)], 7, axis=0))
                    for i in range(-k)]
            out += [xs[-8 * k:]] + wrap
    return jnp.concatenate(out, axis=0)


def _as_bf16(*tables):
    return tuple(jnp.asarray(t, dtype=F32).astype(BF16) for t in tables)


def _angle_table(n_rows, n_cols, period):
    prod = np.outer(np.arange(n_rows, dtype=np.int64), np.arange(n_cols, dtype=np.int64)) % period
    return 2.0 * np.pi * prod.astype(np.float64) / period


@functools.lru_cache(maxsize=None)
def _hyena_dft_tables(L):
    n = 2 * L
    ang = _angle_table(L, L, n)
    alt = np.where(np.arange(L) % 2 == 0, 1.0, -1.0)
    cos_f, sin_f = np.cos(ang), np.sin(ang)
    sin_f[0, :] = alt
    fwd = np.concatenate([cos_f, sin_f], axis=0)
    cos_i, sin_i = 2.0 / n * np.cos(ang), 2.0 / n * np.sin(ang)
    cos_i[:, 0] = 1.0 / n
    sin_i[:, 0] = alt / n
    inv = np.concatenate([cos_i, sin_i], axis=1)
    perm = _time_of_row(L)
    return fwd[:, perm].astype(np.float32), inv[perm, :].astype(np.float32)


@functools.lru_cache(maxsize=None)
def _fnet_tables(L):
    perm = _time_of_row(L)
    ang = _angle_table(L, L, L)[perm][:, perm]
    seq = np.concatenate([np.cos(ang), -np.sin(ang)], axis=1)
    ang_c = _angle_table(FN_GROUP, FN_GROUP, FN_GROUP)
    chan = np.concatenate([np.cos(ang_c), np.sin(ang_c)], axis=1)
    return seq.astype(np.float32), chan.astype(np.float32)


@functools.lru_cache(maxsize=None)
def _filter_tables(L):
    t = np.arange(L, dtype=np.float32) / np.float32(L)
    ang = 2.0 * np.pi * t[:, None].astype(np.float64) * np.arange(1, N_BANDS + 1, dtype=np.float64)
    feats = np.zeros((L, FEAT_PAD), np.float64)
    feats[:, 0] = t
    feats[:, 1:1 + N_BANDS] = np.sin(ang)
    feats[:, 1 + N_BANDS:FILT_EMB] = np.cos(ang)
    deltas = np.linspace(DECAY_SLOW, DECAY_FAST, D_HY, dtype=np.float32).astype(np.float64)
    decay = np.exp(-t[:, None].astype(np.float64) * deltas)
    perm = _time_of_row(L)
    return jnp.asarray(feats[perm], dtype=F32), jnp.asarray(decay[perm], dtype=F32)


def _mod_kernel(ca_ref, cb_ref, wa_ref, wb_ref, b_ref, o_ref):
    silu = lambda c_ref: c_ref[...] * _sigmoid(c_ref[...])
    part = _dot_split3(silu(ca_ref), wa_ref[...]) + _dot_split3(silu(cb_ref), wb_ref[...])

    @pl.when(pl.program_id(1) == 0)
    def _():
        o_ref[...] = part + b_ref[...]

    @pl.when(pl.program_id(1) != 0)
    def _():
        o_ref[...] += part


def _modulation(cvec, w_ada, b_ada):
    tk = 256
    nk = D_MODEL // tk
    c_slabs = cvec.reshape(MOD_ROWS, nk, tk).transpose(1, 0, 2)
    c_spec = lambda half: pl.BlockSpec((None, MOD_ROWS, tk), lambda l, k: (2 * k + half, 0, 0))
    w_spec = lambda half: pl.BlockSpec((None, tk, 6 * D_MODEL), lambda l, k: (l, 2 * k + half, 0))
    return pl.pallas_call(
        _mod_kernel,
        grid=(DEPTH, nk // 2),
        in_specs=[
            c_spec(0), c_spec(1), w_spec(0), w_spec(1),
            pl.BlockSpec((None, 1, 6 * D_MODEL), lambda l, k: (l, 0, 0)),
        ],
        out_specs=pl.BlockSpec((None, MOD_ROWS, 6 * D_MODEL), lambda l, k: (l, 0, 0)),
        out_shape=jax.ShapeDtypeStruct((DEPTH, MOD_ROWS, 6 * D_MODEL), F32),
        compiler_params=_params("arbitrary", "arbitrary"),
        name="adaln_modulation",
    )(c_slabs, c_slabs, w_ada, w_ada, b_ada.reshape(DEPTH, 1, 6 * D_MODEL))


def _filter_kernel(feats_ref, decay_ref, wf_ref, w1_ref, b1_ref, w2_ref, b2_ref, w3_ref, fq_ref,
                   p_ref, nyq_ref, hid_scr, *, L):
    @pl.when(pl.program_id(1) == 0)
    def _():
        freq = fq_ref[...]
        h = jnp.sin(freq * (jnp.dot(feats_ref[...], w1_ref[...], precision=HIGHEST,
                                    preferred_element_type=F32) + b1_ref[...]))
        hid_scr[...] = jnp.sin(freq * (jnp.dot(h, w2_ref[...], precision=HIGHEST,
                                               preferred_element_type=F32) + b2_ref[...]))

    h = _dot_split3(hid_scr[...], w3_ref[...])
    decay = decay_ref[...]
    row = lax.broadcasted_iota(jnp.int32, (L, 1), 0)
    h_fwd = h[:, :D_HY] * decay
    h_bwd = jnp.where(row == 0, 0.0, h[:, D_HY:] * decay)
    ssq = jnp.sum(h_fwd * h_fwd, axis=0, keepdims=True) + jnp.sum(h_bwd * h_bwd, axis=0, keepdims=True)
    scale = lax.rsqrt(ssq + EPS)
    even = (h_fwd + h_bwd) * scale
    odd = (h_bwd - h_fwd) * scale
    k_re = jnp.dot(wf_ref[0:L, :], even.astype(BF16), preferred_element_type=F32)
    k_im = jnp.dot(wf_ref[L:2 * L, :], odd.astype(BF16), preferred_element_type=F32)
    alt = jnp.where(((row >> 3) & 1) == 0, 1.0, -1.0)
    k_nyq = jnp.sum(even * alt, axis=0, keepdims=True)
    p_ref[0] = k_re
    p_ref[1] = jnp.where(row == 0, 0.0, k_im)
    nyq_ref[...] = k_nyq


def _hyena_filters(L, wf, fw1, fb1, fw2, fb2, fw3, ffreq):
    feats, decay = _filter_tables(L)
    w1 = jnp.pad(fw1, ((0, 0), (0, FEAT_PAD - FILT_EMB), (0, 0)))
    const = lambda shape: pl.BlockSpec(shape, lambda l, o: (0,) * len(shape))
    per_layer = lambda *shape: pl.BlockSpec((None,) + shape, lambda l, o: (l,) + (0,) * len(shape))
    return pl.pallas_call(
        functools.partial(_filter_kernel, L=L),
        grid=(DEPTH, 2),
        in_specs=[
            const((L, FEAT_PAD)), const((L, D_HY)), const((2 * L, L)),
            per_layer(FEAT_PAD, FILT_HID), per_layer(1, FILT_HID),
            per_layer(FILT_HID, FILT_HID), per_layer(1, FILT_HID),
            pl.BlockSpec((None, FILT_HID, 2 * D_HY), lambda l, o: (l, 0, o)),
            per_layer(1, FILT_HID),
        ],
        out_specs=[pl.BlockSpec((None, None, 2, L, D_HY), lambda l, o: (l, o, 0, 0, 0)),
                   pl.BlockSpec((None, None, 1, D_HY), lambda l, o: (l, o, 0, 0))],
        out_shape=[jax.ShapeDtypeStruct((DEPTH, 2, 2, L, D_HY), F32),
                   jax.ShapeDtypeStruct((DEPTH, 2, 1, D_HY), F32)],
        scratch_shapes=[pltpu.VMEM((L, FILT_HID), F32)],
        compiler_params=_params("arbitrary", "arbitrary"),
        name=f"hyena_filters_{L}",
    )(feats, decay, wf, w1, fb1.reshape(DEPTH, 1, FILT_HID), fw2, fb2.reshape(DEPTH, 1, FILT_HID),
      fw3, ffreq.reshape(DEPTH, 1, FILT_HID))


def _modulated_norm(x, g, mod_ref, q, col):
    r = _mod_row(q)
    shift = mod_ref[pl.ds(r, 1), col * D_MODEL:(col + 1) * D_MODEL]
    scale = mod_ref[pl.ds(r, 1), (col + 1) * D_MODEL:(col + 2) * D_MODEL]
    return (_rmsnorm(x, g) * (1.0 + scale) + shift).astype(BF16)


def _inproj_kernel(x_ref, mod_ref, g_ref, w_ref, u_ref, h_scr, *, tm):
    i = pl.program_id(0)
    project = lambda h, w: jnp.dot(h, w, preferred_element_type=F32).astype(BF16)

    @pl.when(pl.program_id(1) == 0)
    def _():
        w = w_ref[...].astype(BF16)
        for s in range(tm // CHUNK):
            rows = slice(s * CHUNK, (s + 1) * CHUNK)
            h = _modulated_norm(x_ref[rows, :], g_ref[...], mod_ref, i * (tm // CHUNK) + s, 0)
            h_scr[rows, :] = h
            u_ref[rows, :] = project(h, w)

    @pl.when(pl.program_id(1) != 0)
    def _():
        u_ref[...] = project(h_scr[...], w_ref[...].astype(BF16))


def _input_projection(x, mod, norm_g, w_in, l, tm=2048, tn=1024):
    return pl.pallas_call(
        functools.partial(_inproj_kernel, tm=tm),
        grid=(N_TOK // tm, COL_RY // tn),
        in_specs=[
            pl.BlockSpec((tm, D_MODEL), lambda i, j: (i, 0)),
            pl.BlockSpec((None, MOD_ROWS, 6 * D_MODEL), lambda i, j: (l, 0, 0)),
            pl.BlockSpec((None, 1, D_MODEL), lambda i, j: (l, 0, 0)),
            pl.BlockSpec((None, D_MODEL, tn), lambda i, j: (l, 0, j)),
        ],
        out_specs=pl.BlockSpec((tm, tn), lambda i, j: (i, j)),
        out_shape=jax.ShapeDtypeStruct((N_TOK, COL_RY), BF16),
        scratch_shapes=[pltpu.VMEM((tm, D_MODEL), BF16)],
        compiler_params=_params("arbitrary", "arbitrary"),
        name="input_projection",
    )(x, mod, norm_g, w_in)


def _interleave(*task_lists):
    steps = max(len(tasks) for tasks in task_lists)
    done = [0] * len(task_lists)
    for step in range(1, steps + 1):
        for i, tasks in enumerate(task_lists):
            while done[i] < (step * len(tasks)) // steps:
                tasks[done[i]]()
                done[i] += 1


def _dot_by_k_tiles(lhs_ref, rhs_of, out):
    k_tile = min(K_PIECE, lhs_ref.shape[1])

    def piece(k):
        def run():
            cols = slice(k * k_tile, (k + 1) * k_tile)
            rhs = rhs_of()[cols]
            n_split = 2 if lhs_ref.shape[0] >= SPLIT_DOT_ROWS else 1
            rows = lhs_ref.shape[0] // n_split
            part = jnp.concatenate(
                [jnp.dot(lhs_ref[r * rows:(r + 1) * rows, cols], rhs, preferred_element_type=F32)
                 for r in range(n_split)], axis=0)
            out["acc"] = part if k == 0 else out["acc"] + part
        return run
    return [piece(k) for k in range(lhs_ref.shape[1] // k_tile)]


def _spectrum_product(spec, p_ref, nyq_ref, order, L):
    z_re, z_sn = spec[:L], spec[L:]
    p_re, p_im = p_ref[order, 0], p_ref[order, 1]
    y_re = z_re * p_re + z_sn * p_im
    y_sn = z_sn * p_re - z_re * p_im
    first = lax.broadcasted_iota(jnp.int32, (8, 1), 0) == 0
    y_sn_top = jnp.where(first, z_sn[:8] * nyq_ref[order], y_sn[:8])
    return jnp.concatenate([y_re, y_sn_top, y_sn[8:]], axis=0).astype(BF16)


def _hyena_tasks(L, v_ref, x1_ref, x2_ref, wv_ref, w1_ref, w2_ref, bv_ref, b1_ref, b2_ref, hb_ref,
                 p_ref, nyq_ref, wf_ref, wi_ref, o_ref):
    chunk = {}

    def prepare():
        def short_conv(u_ref, w_ref, b_ref):
            u = u_ref[...].astype(F32)
            w = w_ref[...]
            return (b_ref[...] + _shift_time(u, 1, L) * w[0:1] + u * w[1:2]
                    + _shift_time(u, -1, L) * w[2:3])
        chunk["v"] = short_conv(v_ref, wv_ref, bv_ref)
        chunk["x1"] = short_conv(x1_ref, w1_ref, b1_ref)
        chunk["x2"] = short_conv(x2_ref, w2_ref, b2_ref)
        chunk["bias"] = hb_ref[...]

    tasks = [prepare]
    for s in range(CHUNK // L):
        rows = slice(s * L, (s + 1) * L)
        seq = {}

        def begin(seq=seq, rows=rows):
            seq["in0"] = chunk["v"][rows]
            seq["in0_bf16"] = seq["in0"].astype(BF16)

        tasks.append(begin)
        for order in (0, 1):
            spec, conv = {}, {}
            tasks += _dot_by_k_tiles(wf_ref, lambda seq=seq, order=order: seq[f"in{order}_bf16"], spec)

            def pointwise(seq=seq, spec=spec, order=order):
                seq[f"y{order}"] = _spectrum_product(spec["acc"], p_ref, nyq_ref, order, L)

            tasks.append(pointwise)
            tasks += _dot_by_k_tiles(wi_ref, lambda seq=seq, order=order: seq[f"y{order}"], conv)

            def gate(seq=seq, conv=conv, order=order, rows=rows):
                x = chunk["x1" if order == 0 else "x2"][rows]
                z = seq[f"in{order}"]
                out = x * (conv["acc"] + chunk["bias"][order:order + 1] * z)
                if order == 0:
                    seq["in1"] = out
                    seq["in1_bf16"] = out.astype(BF16)
                else:
                    o_ref[rows, :] = out.astype(BF16)

            tasks.append(gate)
    return tasks


def _fnet_tasks(L, u_ref, cs_ref, f_ref, o_ref):
    chunk = {}

    def prepare():
        x = u_ref[...].astype(BF16)
        parts_c, parts_s = [], []
        for g in range(x.shape[1] // FN_GROUP):
            r = jnp.dot(x[:, g * FN_GROUP:(g + 1) * FN_GROUP], cs_ref[...], preferred_element_type=F32)
            parts_c.append(r[:, :FN_GROUP])
            parts_s.append(r[:, FN_GROUP:])
        chunk["xc"] = jnp.concatenate(parts_c, axis=1).astype(BF16)
        chunk["xs"] = jnp.concatenate(parts_s, axis=1).astype(BF16)

    tasks = [prepare]
    norm = 1.0 / math.sqrt(L * FN_GROUP)
    for s in range(CHUNK // L):
        rows = slice(s * L, (s + 1) * L)
        seq, out = {}, {}

        def stack(seq=seq, rows=rows):
            seq["stacked"] = jnp.concatenate([chunk["xc"][rows], chunk["xs"][rows]], axis=0)

        def finish(out=out, rows=rows):
            o_ref[rows, :] = (out["acc"] * norm).astype(BF16)

        tasks += [stack] + _dot_by_k_tiles(f_ref, lambda seq=seq: seq["stacked"], out) + [finish]
    return tasks


def _scan_sequence(s, L, a_f, b_f, a_b, b_b, h0_ref, y_ref, st_ref):
    nb = L // 8
    tc = a_f.shape[1]
    s0 = s * L

    def local(j, carry):
        hf, pf, hb, pb = carry
        rf = pl.multiple_of(s0 + 8 * j, 8)
        rb = pl.multiple_of(s0 + L - 8 - 8 * j, 8)
        af = a_f[pl.ds(rf, 8), :]
        hf = af * hf + b_f[pl.ds(rf, 8), :]
        pf = af * pf
        b_f[pl.ds(rf, 8), :] = hf
        a_f[pl.ds(rf, 8), :] = pf
        ab = a_b[pl.ds(rb, 8), :]
        hb = ab * hb + b_b[pl.ds(rb, 8), :]
        pb = ab * pb
        b_b[pl.ds(rb, 8), :] = hb
        a_b[pl.ds(rb, 8), :] = pb
        return hf, pf, hb, pb

    zero = jnp.zeros((8, tc), F32)
    one = jnp.ones((8, tc), F32)
    hf, pf, hb, pb = lax.fori_loop(0, nb, local, (zero, one, zero, one), unroll=4)

    carry = h0_ref[s, 0:1, :]
    rows = []
    for k in range(8):
        rows.append(carry)
        carry = hf[k:k + 1] + pf[k:k + 1] * carry
    carry_f = jnp.concatenate(rows, axis=0)
    st_ref[s, 0:1, :] = carry
    carry = h0_ref[s, 1:2, :]
    rows = [None] * 8
    for k in range(7, -1, -1):
        rows[k] = carry
        carry = hb[k:k + 1] + pb[k:k + 1] * carry
    carry_b = jnp.concatenate(rows, axis=0)
    st_ref[s, 1:2, :] = carry

    sl = slice(s0, s0 + L)
    blocked = lambda ref: ref[sl, :].reshape(nb, 8, tc)
    h_sum = ((blocked(b_f) + blocked(a_f) * carry_f[None]) + (blocked(b_b) + blocked(a_b) * carry_b[None]))
    y_ref[sl, :] = h_sum.reshape(L, tc).astype(BF16)


def _rglru_gate_tasks(L, reset, ux_ref, cw_ref, cb_ref, wg_ref, br_ref, bi_ref, lam_ref,
                      a_f, b_f, a_b, b_b):
    chunk = {}

    def prepare():
        u = ux_ref[...].astype(F32)
        w = cw_ref[...]
        xr = (cb_ref[...] + _shift_time(u, 2, L) * w[0:1] + _shift_time(u, 1, L) * w[1:2]
              + u * w[2:3] + _shift_time(u, -1, L) * w[3:4])
        chunk["x"] = xr
        chunk["x_bf16"] = xr.astype(BF16)
        neg_lam = -lam_ref[...]
        softplus = jnp.maximum(neg_lam, 0.0) + jnp.log1p(jnp.exp(-jnp.abs(neg_lam)))
        chunk["rate"] = (-0.25 * RG_C) * softplus
        chunk["half_br"] = 0.5 * br_ref[...]
        chunk["half_bi"] = 0.5 * bi_ref[...]

    tasks = [prepare]
    for n in range(RG_TC // RG_BLOCK):
        cs = slice(n * RG_BLOCK, (n + 1) * RG_BLOCK)

        def project(n=n, cs=cs):
            chunk["gates", n] = jnp.dot(chunk["x_bf16"][:, cs], wg_ref[n].astype(BF16),
                                        preferred_element_type=F32)

        tasks.append(project)
        for d, (a_scr, b_scr) in enumerate(((a_f, b_f), (a_b, b_b))):
            for r0 in range(0, CHUNK, GATE_ROWS):
                def gate(n=n, cs=cs, d=d, a_scr=a_scr, b_scr=b_scr, r0=r0):
                    rows = slice(r0, r0 + GATE_ROWS)
                    gates = chunk["gates", n][rows]
                    c0 = 2 * d * RG_BLOCK
                    t_r = jnp.tanh(gates[:, c0:c0 + RG_BLOCK] + chunk["half_br"][d:d + 1, cs])
                    t_i = jnp.tanh(gates[:, c0 + RG_BLOCK:c0 + 2 * RG_BLOCK] + chunk["half_bi"][d:d + 1, cs])
                    rate = chunk["rate"][d:d + 1, cs]
                    th = jnp.tanh(rate * t_r + rate)
                    recip = 1.0 / (1.0 - th)
                    neg_th = -th
                    root = neg_th * lax.rsqrt(jnp.maximum(neg_th, TINY_F32))
                    half_mult = recip * root
                    if reset:
                        tpos = (r0 + lax.broadcasted_iota(jnp.int32, (GATE_ROWS, 1), 0)) & (L - 1)
                        half_mult = jnp.where(tpos == (0 if d == 0 else L - 1), 0.5, half_mult)
                    a_scr[rows, cs] = (1.0 + th) * recip
                    b_scr[rows, cs] = (half_mult * chunk["x"][rows, cs]) * (t_i + 1.0)

                tasks.append(gate)
    return tasks


def _mixers_kernel(hv_ref, hx1_ref, hx2_ref, fn_ref, ux_ref,
                   hwv_ref, hw1_ref, hw2_ref, hbv_ref, hb1_ref, hb2_ref, hbias_ref,
                   ps_ref, ns_ref, pl_ref, nl_ref, wfs_ref, wis_ref, wfl_ref, wil_ref,
                   cs_ref, fs_ref, fl_ref,
                   h0_ref, cw_ref, cb_ref, wg_ref, br_ref, bi_ref, lam_ref,
                   yh_ref, yf_ref, y_ref, st_ref, a_f, b_f, a_b, b_b):
    q = pl.program_id(1)

    def run(L, reset, p_ref, nyq_ref, wf_ref, wi_ref, f_ref):
        matmul_side = (_hyena_tasks(L, hv_ref, hx1_ref, hx2_ref, hwv_ref, hw1_ref, hw2_ref, hbv_ref,
                                    hb1_ref, hb2_ref, hbias_ref, p_ref, nyq_ref, wf_ref, wi_ref, yh_ref)
                       + _fnet_tasks(L, fn_ref, cs_ref, f_ref, yf_ref))
        vector_side = _rglru_gate_tasks(L, reset, ux_ref, cw_ref, cb_ref, wg_ref, br_ref, bi_ref,
                                        lam_ref, a_f, b_f, a_b, b_b)
        _interleave(matmul_side, vector_side)
        st_ref[...] = jnp.zeros(st_ref.shape, F32)
        for s in range(CHUNK // L):
            _scan_sequence(s, L, a_f, b_f, a_b, b_b, h0_ref, y_ref, st_ref)

    @pl.when(q < NQ_CTX)
    def _():
        run(SEQ, True, ps_ref, ns_ref, wfs_ref, wis_ref, fs_ref)

    @pl.when(q >= NQ_CTX)
    def _():
        run(DEC_SEQ, False, pl_ref, nl_ref, wfl_ref, wil_ref, fl_ref)


def _mixers(u, l, hy_conv_w, hy_conv_b, hy_bias, spectra_short, spectra_long, dft_short, dft_long,
            fnet_chan, fnet_short, fnet_long, h0, rg_conv_w, rg_conv_b, w_gates, b_r, b_i, lam):
    assert D_HY // HY_TC == D_FN // HY_TC == D_RG // RG_TC
    halves = D_HY // HY_TC
    p_short, nyq_short = spectra_short
    p_long, nyq_long = spectra_long
    ucol = lambda width, col0: pl.BlockSpec((CHUNK, width), lambda c, q: (q, col0 // width + c))
    hy_part = lambda rows, k: pl.BlockSpec((None, rows, HY_TC), lambda c, q: (l, 0, k * halves + c))
    hy_tile = lambda rows: pl.BlockSpec((None, rows, HY_TC), lambda c, q: (l, 0, c))
    rg_tile = lambda rows: pl.BlockSpec((None, rows, RG_TC), lambda c, q: (l, 0, c))
    const = lambda shape: pl.BlockSpec(shape, lambda c, q: (0, 0))
    planes = lambda L: pl.BlockSpec((None, 2, 2, L, HY_TC), lambda c, q: (l, 0, 0, 0, c))
    nyquist = pl.BlockSpec((None, 2, 1, HY_TC), lambda c, q: (l, 0, 0, c))
    state_spec = pl.BlockSpec((None, CTX_PER_CHUNK, 2, RG_TC), lambda c, q: (q, 0, 0, c))
    out_tile = lambda width: pl.BlockSpec((CHUNK, width), lambda c, q: (q, c))
    hy_conv_b = hy_conv_b.reshape(DEPTH, 1, 3 * D_HY)
    return pl.pallas_call(
        _mixers_kernel,
        grid=(halves, NQ),
        in_specs=[
            ucol(HY_TC, 0), ucol(HY_TC, D_HY), ucol(HY_TC, 2 * D_HY), ucol(HY_TC, COL_FN),
            ucol(RG_TC, COL_RX),
            hy_part(3, 0), hy_part(3, 1), hy_part(3, 2), hy_part(1, 0), hy_part(1, 1), hy_part(1, 2),
            hy_tile(2),
            planes(SEQ), nyquist, planes(DEC_SEQ), nyquist,
            const((2 * SEQ, SEQ)), const((SEQ, 2 * SEQ)),
            const((2 * DEC_SEQ, DEC_SEQ)), const((DEC_SEQ, 2 * DEC_SEQ)),
            const((FN_GROUP, 2 * FN_GROUP)), const((SEQ, 2 * SEQ)), const((DEC_SEQ, 2 * DEC_SEQ)),
            state_spec, rg_tile(4), rg_tile(1),
            pl.BlockSpec((None, RG_TC // RG_BLOCK, RG_BLOCK, 4 * RG_BLOCK), lambda c, q: (l, c, 0, 0)),
            rg_tile(2), rg_tile(2), rg_tile(2),
        ],
        out_specs=[out_tile(HY_TC), out_tile(HY_TC), out_tile(RG_TC), state_spec],
        out_shape=[jax.ShapeDtypeStruct((N_TOK, D_HY), BF16),
                   jax.ShapeDtypeStruct((N_TOK, D_FN), BF16),
                   jax.ShapeDtypeStruct((N_TOK, D_RG), BF16),
                   jax.ShapeDtypeStruct((NQ, CTX_PER_CHUNK, 2, D_RG), F32)],
        scratch_shapes=[pltpu.VMEM((CHUNK, RG_TC), F32)] * 4,
        compiler_params=_params("arbitrary", "arbitrary"),
        name="sequence_mixers",
    )(u, u, u, u, u, hy_conv_w, hy_conv_w, hy_conv_w, hy_conv_b, hy_conv_b, hy_conv_b, hy_bias,
      p_short, nyq_short, p_long, nyq_long, *dft_short, *dft_long, fnet_chan, fnet_short, fnet_long,
      h0, rg_conv_w, rg_conv_b.reshape(DEPTH, 1, D_RG), w_gates, b_r, b_i, lam)


def _merge_kernel(x_ref, ya_ref, yb_ref, hc_ref, mod_ref, g_ref, wy_ref, wga_ref, wgb_ref, wgc_ref,
                  wa_ref, wb_ref, wc_ref, wo_ref, o_ref, *, tm):
    q = pl.program_id(0) // (CHUNK // tm)
    gate = mod_ref[pl.ds(_mod_row(q), 1), 2 * D_MODEL:3 * D_MODEL]
    project = lambda y, w_ref: jnp.dot(y, w_ref[...].astype(BF16), preferred_element_type=F32)
    h = _modulated_norm(x_ref[...], g_ref[...], mod_ref, q, 0)
    ya = project(ya_ref[...], wa_ref)
    yb = project(yb_ref[...], wb_ref)
    rg = hc_ref[...].astype(F32) * _gelu_tanh(project(h, wy_ref))
    yc = project(rg.astype(BF16), wc_ref)
    mix = (_sigmoid(project(h, wga_ref)) * ya + _sigmoid(project(h, wgb_ref)) * yb
           + _sigmoid(project(h, wgc_ref)) * yc)
    o_ref[...] = x_ref[...] + gate * project(mix.astype(BF16), wo_ref)


def _merge(x, y_hy, y_fn, h_rg, mod, norm_g, w_in, w_a, w_b, w_c, w_o, l, tm=512):
    rows = lambda width: pl.BlockSpec((tm, width), lambda i: (i, 0))
    weight = lambda k: pl.BlockSpec((None, k, D_MODEL), lambda i: (l, 0, 0))
    in_cols = lambda col0: pl.BlockSpec((None, D_MODEL, D_MODEL), lambda i: (l, 0, col0 // D_MODEL))
    return pl.pallas_call(
        functools.partial(_merge_kernel, tm=tm),
        grid=(N_TOK // tm,),
        in_specs=[
            rows(D_MODEL), rows(D_HY), rows(D_FN), rows(D_RG),
            pl.BlockSpec((None, MOD_ROWS, 6 * D_MODEL), lambda i: (l, 0, 0)),
            pl.BlockSpec((None, 1, D_MODEL), lambda i: (l, 0, 0)),
            in_cols(COL_RY), in_cols(COL_G), in_cols(COL_G + D_MODEL), in_cols(COL_G + 2 * D_MODEL),
            weight(D_HY), weight(D_FN), weight(D_RG), weight(D_MODEL),
        ],
        out_specs=rows(D_MODEL),
        out_shape=jax.ShapeDtypeStruct((N_TOK, D_MODEL), F32),
        compiler_params=_params("arbitrary"),
        name="branch_merge",
    )(x, y_hy, y_fn, h_rg, mod, norm_g, w_in, w_in, w_in, w_in, w_a, w_b, w_c, w_o)


def _ffn_kernel(x_ref, mod_ref, g_ref, wg_ref, wu_ref, wd_ref, fg_ref, o_ref, h_scr, *, tm, final_norm):
    i = pl.program_id(0)
    f = pl.program_id(1)
    sub = tm // CHUNK

    def gated_mlp(h, wg, wu, wd):
        gt = jnp.dot(h, wg, preferred_element_type=F32)
        up = jnp.dot(h, wu, preferred_element_type=F32)
        act = (gt * _sigmoid(gt)) * up
        return jnp.dot(act.astype(BF16), wd, preferred_element_type=F32)

    @pl.when(f == 0)
    def _():
        wg, wu, wd = (w_ref[...].astype(BF16) for w_ref in (wg_ref, wu_ref, wd_ref))
        for s in range(sub):
            rows = slice(s * CHUNK, (s + 1) * CHUNK)
            h = _modulated_norm(x_ref[rows, :], g_ref[...], mod_ref, i * sub + s, 3)
            h_scr[rows, :] = h
            o_ref[rows, :] = gated_mlp(h, wg, wu, wd)

    last = pl.num_programs(1) - 1

    @pl.when((f != 0) & (f != last))
    def _():
        o_ref[...] += gated_mlp(h_scr[...], *(w_ref[...].astype(BF16) for w_ref in (wg_ref, wu_ref, wd_ref)))

    @pl.when(f == last)
    def _():
        wg, wu, wd = (w_ref[...].astype(BF16) for w_ref in (wg_ref, wu_ref, wd_ref))
        for s in range(sub):
            r = _mod_row(i * sub + s)
            gate = mod_ref[pl.ds(r, 1), 5 * D_MODEL:6 * D_MODEL]
            rows = slice(s * CHUNK, (s + 1) * CHUNK)
            out = x_ref[rows, :] + gate * (o_ref[rows, :] + gated_mlp(h_scr[rows, :], wg, wu, wd))
            if final_norm:
                out = _rmsnorm(out, fg_ref[...])
            o_ref[rows, :] = out


def _ffn(x, mod, norm_g, w_gu, w_down, final_g, l, final_norm, tm=2048, tf=256):
    nf = D_FF // tf
    return pl.pallas_call(
        functools.partial(_ffn_kernel, tm=tm, final_norm=final_norm),
        grid=(N_TOK // tm, nf),
        in_specs=[
            pl.BlockSpec((tm, D_MODEL), lambda i, f: (i, 0)),
            pl.BlockSpec((None, MOD_ROWS, 6 * D_MODEL), lambda i, f: (l, 0, 0)),
            pl.BlockSpec((None, 1, D_MODEL), lambda i, f: (l, 0, 0)),
            pl.BlockSpec((None, D_MODEL, tf), lambda i, f: (l, 0, f)),
            pl.BlockSpec((None, D_MODEL, tf), lambda i, f: (l, 0, nf + f)),
            pl.BlockSpec((None, tf, D_MODEL), lambda i, f: (l, f, 0)),
            pl.BlockSpec((1, D_MODEL), lambda i, f: (0, 0)),
        ],
        out_specs=pl.BlockSpec((tm, D_MODEL), lambda i, f: (i, 0)),
        out_shape=jax.ShapeDtypeStruct((N_TOK, D_MODEL), F32),
        scratch_shapes=[pltpu.VMEM((tm, D_MODEL), BF16)],
        compiler_params=_params("arbitrary", "arbitrary"),
        name="swiglu_ffn",
    )(x, mod, norm_g, w_gu, w_gu, w_down, final_g)


def kernel(x_prompt, x_sample, c, state_rglru, c_ctx, norm1_g, norm2_g, w_ada, b_ada, w_in,
           hy_conv_w, hy_conv_b, hy_f_w1, hy_f_b1, hy_f_w2, hy_f_b2, hy_f_w3, hy_f_freq, hy_bias,
           w_a, w_b, rg_conv_w, rg_conv_b, rg_wr, rg_br, rg_wi, rg_bi, rg_lam, w_c, w_o,
           w_gu, w_down, final_g):
    x = jnp.concatenate([_to_time_permuted(x_prompt, SEQ), _to_time_permuted(x_sample, DEC_SEQ)])
    cvec = jnp.concatenate([c_ctx[None, :], c, jnp.zeros((MOD_ROWS - 1 - DEC_BATCH, D_MODEL), F32)])
    mod = _modulation(cvec, w_ada, b_ada)

    dft_short = _as_bf16(*_hyena_dft_tables(SEQ))
    dft_long = _as_bf16(*_hyena_dft_tables(DEC_SEQ))
    fnet_short, fnet_chan = _as_bf16(*_fnet_tables(SEQ))
    fnet_long, = _as_bf16(_fnet_tables(DEC_SEQ)[0])
    filt = (hy_f_w1, hy_f_b1, hy_f_w2, hy_f_b2, hy_f_w3, hy_f_freq)
    p_short = _hyena_filters(SEQ, dft_short[0], *filt)
    p_long = _hyena_filters(DEC_SEQ, dft_long[0], *filt)

    w_gates = 0.5 * jnp.concatenate([rg_wr[:, 0], rg_wi[:, 0], rg_wr[:, 1], rg_wi[:, 1]], axis=-1)
    lat_h0 = jnp.pad(state_rglru.astype(F32).transpose(1, 0, 2, 3)[:, :, None],
                     ((0, 0), (0, 0), (0, CTX_PER_CHUNK - 1), (0, 0), (0, 0)))
    h0_all = jnp.concatenate([jnp.zeros((DEPTH, NQ_CTX, CTX_PER_CHUNK, 2, D_RG), F32), lat_h0], axis=1)
    norm1 = norm1_g.reshape(DEPTH, 1, D_MODEL)
    norm2 = norm2_g.reshape(DEPTH, 1, D_MODEL)
    final = final_g.reshape(1, D_MODEL)

    states = []
    for l in range(DEPTH):
        u = _input_projection(x, mod, norm1, w_in, l)
        y_hy, y_fn, h_rg, st = _mixers(
            u, l, hy_conv_w, hy_conv_b, hy_bias, p_short, p_long, dft_short, dft_long,
            fnet_chan, fnet_short, fnet_long, h0_all[l], rg_conv_w, rg_conv_b, w_gates,
            rg_br, rg_bi, rg_lam)
        states.append(st[:NQ_CTX].reshape(BATCH, 2, D_RG))
        x = _merge(x, y_hy, y_fn, h_rg, mod, norm1, w_in, w_a, w_b, w_c, w_o, l)
        x = _ffn(x, mod, norm2, w_gu, w_down, final, l, final_norm=(l == DEPTH - 1))

    y_prompt = _from_time_permuted(x[:N_CTX_TOK], BATCH, SEQ)
    y_sample = _from_time_permuted(x[N_CTX_TOK:], DEC_BATCH, DEC_SEQ)
    new_state = jnp.stack(states, axis=1).astype(x_prompt.dtype)
    return (y_prompt, y_sample, new_state)
```

```python
import functools
import math

import numpy as np
import jax
import jax.numpy as jnp
from jax import lax
from jax.experimental import pallas as pl
from jax.experimental.pallas import tpu as pltpu

F32 = jnp.float32
BF16 = jnp.bfloat16
HIGHEST = lax.Precision.HIGHEST

D_MODEL = 1024
BATCH = 16
SEQ = 256
DEPTH = 4
DEC_BATCH = 4
DEC_SEQ = 1024
EPS = 1e-6
TINY_F32 = float(np.finfo(np.float32).tiny)
D_HY = 512
N_BANDS = 8
FILT_EMB = 1 + 2 * N_BANDS
FILT_HID = 64
DECAY_SLOW = -math.log(1e-2) / 1.5
DECAY_FAST = -math.log(1e-2) / 0.3
D_FN = 512
FN_GROUP = 128
N_FN_GROUPS = D_FN // FN_GROUP
D_RG = 1024
RG_BLOCK = 128
RG_C = 8.0
D_FF = -(-8 * D_MODEL // (3 * 256)) * 256
D_IN = 3 * D_HY + D_FN + 2 * D_RG + 3 * D_MODEL
COL_FN = 3 * D_HY
COL_RX = COL_FN + D_FN
COL_RY = COL_RX + D_RG
COL_G = COL_RY + D_RG

CHUNK = 1024
N_CTX_TOK = BATCH * SEQ
N_LAT_TOK = DEC_BATCH * DEC_SEQ
N_TOK = N_CTX_TOK + N_LAT_TOK
NQ_CTX = N_CTX_TOK // CHUNK
NQ = N_TOK // CHUNK
CTX_PER_CHUNK = CHUNK // SEQ
MOD_ROWS = 8
FEAT_PAD = 128
HY_TC = 256
RG_TC = 512
K_PIECE = 1024
GATE_ROWS = 256
SPLIT_DOT_ROWS = 1024
VMEM_LIMIT = 60 * 1024 * 1024

assert DEC_SEQ == CHUNK and CHUNK % SEQ == 0 and N_CTX_TOK % CHUNK == 0
assert SEQ % 16 == 0 and DEC_SEQ % 16 == 0
assert 1 + DEC_BATCH <= MOD_ROWS


def _params(*sem):
    return pltpu.CompilerParams(dimension_semantics=sem, vmem_limit_bytes=VMEM_LIMIT)


def _mod_row(q):
    return jnp.maximum(q - (NQ_CTX - 1), 0)


def _rmsnorm(x, g):
    return x * lax.rsqrt(jnp.mean(x * x, axis=-1, keepdims=True) + EPS) * g


def _sigmoid(x):
    return 0.5 * jnp.tanh(0.5 * x) + 0.5


def _dot_split3(a, w):
    rows = a.shape[0]
    a_hi = a.astype(BF16)
    a_lo = (a - a_hi.astype(F32)).astype(BF16)
    w_hi = w.astype(BF16)
    w_lo = (w - w_hi.astype(F32)).astype(BF16)
    heads = jnp.dot(jnp.concatenate([a_hi, a_lo], axis=0), w_hi, preferred_element_type=F32)
    return (heads[:rows] + heads[rows:]) + jnp.dot(a_hi, w_lo, preferred_element_type=F32)


def _gelu_tanh(x):
    inner = x * (math.sqrt(2.0 / math.pi) * 0.044715 * (x * x) + math.sqrt(2.0 / math.pi))
    half = 0.5 * x
    return half * jnp.tanh(inner) + half


def _time_of_row(L):
    p = np.arange(L)
    return (p % 8) * (L // 8) + p // 8


def _to_time_permuted(x, L):
    b = x.shape[0]
    return x.reshape(b, 8, L // 8, x.shape[-1]).transpose(0, 2, 1, 3).reshape(b * L, x.shape[-1])


def _from_time_permuted(x, b, L):
    return x.reshape(b, L // 8, 8, x.shape[-1]).transpose(0, 2, 1, 3).reshape(b, L, x.shape[-1])


def _shift_time(x, k, L):
    sub = lax.broadcasted_iota(jnp.int32, (8, 1), 0)
    out = []
    for s0 in range(0, x.shape[0], L):
        xs = x[s0:s0 + L]
        if k > 0:
            wrap = [jnp.where(sub == 0, 0.0, pltpu.roll(xs[L - 8 * (k - i):L - 8 * (k - i - 1)], 1, axis=0))
                    for i in range(k)]
            out += wrap + [xs[:L - 8 * k]]
        else:
            wrap = [jnp.where(sub == 7, 0.0, pltpu.roll(xs[8 * i:8 * (i + 1)], 7, axis=0))
                    for i in range(-k)]
            out += [xs[-8 * k:]] + wrap
    return jnp.concatenate(out, axis=0)


def _as_bf16(*tables):
    return tuple(jnp.asarray(t, dtype=F32).astype(BF16) for t in tables)


def _angle_table(n_rows, n_cols, period):
    prod = np.outer(np.arange(n_rows, dtype=np.int64), np.arange(n_cols, dtype=np.int64)) % period
    return 2.0 * np.pi * prod.astype(np.float64) / period


@functools.lru_cache(maxsize=None)
def _hyena_dft_tables(L):
    n = 2 * L
    ang = _angle_table(L, L, n)
    alt = np.where(np.arange(L) % 2 == 0, 1.0, -1.0)
    cos_f, sin_f = np.cos(ang), np.sin(ang)
    sin_f[0, :] = alt
    fwd = np.concatenate([cos_f, sin_f], axis=0)
    cos_i, sin_i = 2.0 / n * np.cos(ang), 2.0 / n * np.sin(ang)
    cos_i[:, 0] = 1.0 / n
    sin_i[:, 0] = alt / n
    inv = np.concatenate([cos_i, sin_i], axis=1)
    perm = _time_of_row(L)
    return fwd[:, perm].astype(np.float32), inv[perm, :].astype(np.float32)


@functools.lru_cache(maxsize=None)
def _fnet_tables(L):
    perm = _time_of_row(L)
    ang = _angle_table(L, L, L)[perm][:, perm]
    seq = np.concatenate([np.cos(ang), -np.sin(ang)], axis=1)
    ang_c = _angle_table(FN_GROUP, FN_GROUP, FN_GROUP)
    chan = np.concatenate([np.cos(ang_c), np.sin(ang_c)], axis=1)
    return seq.astype(np.float32), chan.astype(np.float32)


@functools.lru_cache(maxsize=None)
def _filter_tables(L):
    t = np.arange(L, dtype=np.float32) / np.float32(L)
    ang = 2.0 * np.pi * t[:, None].astype(np.float64) * np.arange(1, N_BANDS + 1, dtype=np.float64)
    feats = np.zeros((L, FEAT_PAD), np.float64)
    feats[:, 0] = t
    feats[:, 1:1 + N_BANDS] = np.sin(ang)
    feats[:, 1 + N_BANDS:FILT_EMB] = np.cos(ang)
    deltas = np.linspace(DECAY_SLOW, DECAY_FAST, D_HY, dtype=np.float32).astype(np.float64)
    decay = np.exp(-t[:, None].astype(np.float64) * deltas)
    perm = _time_of_row(L)
    return jnp.asarray(feats[perm], dtype=F32), jnp.asarray(decay[perm], dtype=F32)


def _mod_kernel(ca_ref, cb_ref, wa_ref, wb_ref, b_ref, o_ref):
    silu = lambda c_ref: c_ref[...] * _sigmoid(c_ref[...])
    part = _dot_split3(silu(ca_ref), wa_ref[...]) + _dot_split3(silu(cb_ref), wb_ref[...])

    @pl.when(pl.program_id(1) == 0)
    def _():
        o_ref[...] = part + b_ref[...]

    @pl.when(pl.program_id(1) != 0)
    def _():
        o_ref[...] += part


def _modulation(cvec, w_ada, b_ada):
    tk = 256
    nk = D_MODEL // tk
    c_slabs = cvec.reshape(MOD_ROWS, nk, tk).transpose(1, 0, 2)
    c_spec = lambda half: pl.BlockSpec((None, MOD_ROWS, tk), lambda l, k: (2 * k + half, 0, 0))
    w_spec = lambda half: pl.BlockSpec((None, tk, 6 * D_MODEL), lambda l, k: (l, 2 * k + half, 0))
    return pl.pallas_call(
        _mod_kernel,
        grid=(DEPTH, nk // 2),
        in_specs=[
            c_spec(0), c_spec(1), w_spec(0), w_spec(1),
            pl.BlockSpec((None, 1, 6 * D_MODEL), lambda l, k: (l, 0, 0)),
        ],
        out_specs=pl.BlockSpec((None, MOD_ROWS, 6 * D_MODEL), lambda l, k: (l, 0, 0)),
        out_shape=jax.ShapeDtypeStruct((DEPTH, MOD_ROWS, 6 * D_MODEL), F32),
        compiler_params=_params("arbitrary", "arbitrary"),
        name="adaln_modulation",
    )(c_slabs, c_slabs, w_ada, w_ada, b_ada.reshape(DEPTH, 1, 6 * D_MODEL))


def _filter_kernel(feats_ref, decay_ref, wf_ref, w1_ref, b1_ref, w2_ref, b2_ref, w3_ref, fq_ref,
                   p_ref, nyq_ref, hid_scr, *, L):
    @pl.when(pl.program_id(1) == 0)
    def _():
        freq = fq_ref[...]
        h = jnp.sin(freq * (jnp.dot(feats_ref[...], w1_ref[...], precision=HIGHEST,
                                    preferred_element_type=F32) + b1_ref[...]))
        hid_scr[...] = jnp.sin(freq * (jnp.dot(h, w2_ref[...], precision=HIGHEST,
                                               preferred_element_type=F32) + b2_ref[...]))

    h = _dot_split3(hid_scr[...], w3_ref[...])
    decay = decay_ref[...]
    row = lax.broadcasted_iota(jnp.int32, (L, 1), 0)
    h_fwd = h[:, :D_HY] * decay
    h_bwd = jnp.where(row == 0, 0.0, h[:, D_HY:] * decay)
    ssq = jnp.sum(h_fwd * h_fwd, axis=0, keepdims=True) + jnp.sum(h_bwd * h_bwd, axis=0, keepdims=True)
    scale = lax.rsqrt(ssq + EPS)
    even = (h_fwd + h_bwd) * scale
    odd = (h_bwd - h_fwd) * scale
    k_re = jnp.dot(wf_ref[0:L, :], even.astype(BF16), preferred_element_type=F32)
    k_im = jnp.dot(wf_ref[L:2 * L, :], odd.astype(BF16), preferred_element_type=F32)
    alt = jnp.where(((row >> 3) & 1) == 0, 1.0, -1.0)
    k_nyq = jnp.sum(even * alt, axis=0, keepdims=True)
    p_ref[0] = k_re
    p_ref[1] = jnp.where(row == 0, 0.0, k_im)
    nyq_ref[...] = k_nyq


def _hyena_filters(L, wf, fw1, fb1, fw2, fb2, fw3, ffreq):
    feats, decay = _filter_tables(L)
    w1 = jnp.pad(fw1, ((0, 0), (0, FEAT_PAD - FILT_EMB), (0, 0)))
    const = lambda shape: pl.BlockSpec(shape, lambda l, o: (0,) * len(shape))
    per_layer = lambda *shape: pl.BlockSpec((None,) + shape, lambda l, o: (l,) + (0,) * len(shape))
    return pl.pallas_call(
        functools.partial(_filter_kernel, L=L),
        grid=(DEPTH, 2),
        in_specs=[
            const((L, FEAT_PAD)), const((L, D_HY)), const((2 * L, L)),
            per_layer(FEAT_PAD, FILT_HID), per_layer(1, FILT_HID),
            per_layer(FILT_HID, FILT_HID), per_layer(1, FILT_HID),
            pl.BlockSpec((None, FILT_HID, 2 * D_HY), lambda l, o: (l, 0, o)),
            per_layer(1, FILT_HID),
        ],
        out_specs=[pl.BlockSpec((None, None, 2, L, D_HY), lambda l, o: (l, o, 0, 0, 0)),
                   pl.BlockSpec((None, None, 1, D_HY), lambda l, o: (l, o, 0, 0))],
        out_shape=[jax.ShapeDtypeStruct((DEPTH, 2, 2, L, D_HY), F32),
                   jax.ShapeDtypeStruct((DEPTH, 2, 1, D_HY), F32)],
        scratch_shapes=[pltpu.VMEM((L, FILT_HID), F32)],
        compiler_params=_params("arbitrary", "arbitrary"),
        name=f"hyena_filters_{L}",
    )(feats, decay, wf, w1, fb1.reshape(DEPTH, 1, FILT_HID), fw2, fb2.reshape(DEPTH, 1, FILT_HID),
      fw3, ffreq.reshape(DEPTH, 1, FILT_HID))


def _modulated_norm(x, g, mod_ref, q, col):
    r = _mod_row(q)
    shift = mod_ref[pl.ds(r, 1), col * D_MODEL:(col + 1) * D_MODEL]
    scale = mod_ref[pl.ds(r, 1), (col + 1) * D_MODEL:(col + 2) * D_MODEL]
    return (_rmsnorm(x, g) * (1.0 + scale) + shift).astype(BF16)


def _inproj_kernel(x_ref, mod_ref, g_ref, w_ref, u_ref, *, tm, part_rows):
    i = pl.program_id(0)
    w = w_ref[...].astype(BF16)
    for s in range(tm // part_rows):
        rows = slice(s * part_rows, (s + 1) * part_rows)
        h = _modulated_norm(x_ref[rows, :], g_ref[...], mod_ref, (i * tm + s * part_rows) // CHUNK, 0)
        u_ref[rows, :] = jnp.dot(h, w, preferred_element_type=F32).astype(BF16)


def _input_projection(x, mod, norm_g, w_in, l, tm=1024, part_rows=512):
    assert CHUNK % part_rows == 0 and tm % part_rows == 0
    return pl.pallas_call(
        functools.partial(_inproj_kernel, tm=tm, part_rows=part_rows),
        grid=(N_TOK // tm,),
        in_specs=[
            pl.BlockSpec((tm, D_MODEL), lambda i: (i, 0)),
            pl.BlockSpec((None, MOD_ROWS, 6 * D_MODEL), lambda i: (l, 0, 0)),
            pl.BlockSpec((None, 1, D_MODEL), lambda i: (l, 0, 0)),
            pl.BlockSpec((None, D_MODEL, COL_RY), lambda i: (l, 0, 0)),
        ],
        out_specs=pl.BlockSpec((tm, COL_RY), lambda i: (i, 0)),
        out_shape=jax.ShapeDtypeStruct((N_TOK, COL_RY), BF16),
        compiler_params=_params("arbitrary"),
        name="input_projection",
    )(x, mod, norm_g, w_in)


def _interleave(*task_lists):
    steps = max(len(tasks) for tasks in task_lists)
    done = [0] * len(task_lists)
    for step in range(1, steps + 1):
        for i, tasks in enumerate(task_lists):
            while done[i] < (step * len(tasks)) // steps:
                tasks[done[i]]()
                done[i] += 1


def _dot_by_k_tiles(lhs_ref, rhs_of, out):
    k_tile = min(K_PIECE, lhs_ref.shape[1])

    def piece(k):
        def run():
            cols = slice(k * k_tile, (k + 1) * k_tile)
            rhs = rhs_of()[cols]
            n_split = 2 if lhs_ref.shape[0] >= SPLIT_DOT_ROWS else 1
            rows = lhs_ref.shape[0] // n_split
            part = jnp.concatenate(
                [jnp.dot(lhs_ref[r * rows:(r + 1) * rows, cols], rhs, preferred_element_type=F32)
                 for r in range(n_split)], axis=0)
            out["acc"] = part if k == 0 else out["acc"] + part
        return run
    return [piece(k) for k in range(lhs_ref.shape[1] // k_tile)]


def _spectrum_product(spec, p_ref, nyq_ref, order, L):
    z_re, z_sn = spec[:L], spec[L:]
    p_re, p_im = p_ref[order, 0], p_ref[order, 1]
    y_re = z_re * p_re + z_sn * p_im
    y_sn = z_sn * p_re - z_re * p_im
    first = lax.broadcasted_iota(jnp.int32, (8, 1), 0) == 0
    y_sn_top = jnp.where(first, z_sn[:8] * nyq_ref[order], y_sn[:8])
    return jnp.concatenate([y_re, y_sn_top, y_sn[8:]], axis=0).astype(BF16)


def _hyena_tasks(L, v_ref, x1_ref, x2_ref, wv_ref, w1_ref, w2_ref, bv_ref, b1_ref, b2_ref, hb_ref,
                 p_ref, nyq_ref, wf_ref, wi_ref, o_ref):
    chunk = {}

    def prepare():
        def short_conv(u_ref, w_ref, b_ref):
            u = u_ref[...].astype(F32)
            w = w_ref[...]
            return (b_ref[...] + _shift_time(u, 1, L) * w[0:1] + u * w[1:2]
                    + _shift_time(u, -1, L) * w[2:3])
        chunk["v"] = short_conv(v_ref, wv_ref, bv_ref)
        chunk["x1"] = short_conv(x1_ref, w1_ref, b1_ref)
        chunk["x2"] = short_conv(x2_ref, w2_ref, b2_ref)
        chunk["bias"] = hb_ref[...]

    tasks = [prepare]
    for s in range(CHUNK // L):
        rows = slice(s * L, (s + 1) * L)
        seq = {}

        def begin(seq=seq, rows=rows):
            seq["in0"] = chunk["v"][rows]
            seq["in0_bf16"] = seq["in0"].astype(BF16)

        tasks.append(begin)
        for order in (0, 1):
            spec, conv = {}, {}
            tasks += _dot_by_k_tiles(wf_ref, lambda seq=seq, order=order: seq[f"in{order}_bf16"], spec)

            def pointwise(seq=seq, spec=spec, order=order):
                seq[f"y{order}"] = _spectrum_product(spec["acc"], p_ref, nyq_ref, order, L)

            tasks.append(pointwise)
            tasks += _dot_by_k_tiles(wi_ref, lambda seq=seq, order=order: seq[f"y{order}"], conv)

            def gate(seq=seq, conv=conv, order=order, rows=rows):
                x = chunk["x1" if order == 0 else "x2"][rows]
                z = seq[f"in{order}"]
                out = x * (conv["acc"] + chunk["bias"][order:order + 1] * z)
                if order == 0:
                    seq["in1"] = out
                    seq["in1_bf16"] = out.astype(BF16)
                else:
                    o_ref[rows, :] = out.astype(BF16)

            tasks.append(gate)
    return tasks


def _fnet_tasks(L, u_ref, cs_ref, f_ref, o_ref):
    chunk = {}

    def prepare():
        x = u_ref[...].astype(BF16)
        parts_c, parts_s = [], []
        for g in range(x.shape[1] // FN_GROUP):
            r = jnp.dot(x[:, g * FN_GROUP:(g + 1) * FN_GROUP], cs_ref[...], preferred_element_type=F32)
            parts_c.append(r[:, :FN_GROUP])
            parts_s.append(r[:, FN_GROUP:])
        chunk["xc"] = jnp.concatenate(parts_c, axis=1).astype(BF16)
        chunk["xs"] = jnp.concatenate(parts_s, axis=1).astype(BF16)

    tasks = [prepare]
    norm = 1.0 / math.sqrt(L * FN_GROUP)
    for s in range(CHUNK // L):
        rows = slice(s * L, (s + 1) * L)
        seq, out = {}, {}

        def stack(seq=seq, rows=rows):
            seq["stacked"] = jnp.concatenate([chunk["xc"][rows], chunk["xs"][rows]], axis=0)

        def finish(out=out, rows=rows):
            o_ref[rows, :] = (out["acc"] * norm).astype(BF16)

        tasks += [stack] + _dot_by_k_tiles(f_ref, lambda seq=seq: seq["stacked"], out) + [finish]
    return tasks


def _scan_sequence(s, L, a_f, b_f, a_b, b_b, h0_ref, y_ref, st_ref):
    nb = L // 8
    tc = a_f.shape[1]
    s0 = s * L

    def local(j, carry):
        hf, pf, hb, pb = carry
        rf = pl.multiple_of(s0 + 8 * j, 8)
        rb = pl.multiple_of(s0 + L - 8 - 8 * j, 8)
        af = a_f[pl.ds(rf, 8), :]
        hf = af * hf + b_f[pl.ds(rf, 8), :]
        pf = af * pf
        b_f[pl.ds(rf, 8), :] = hf
        a_f[pl.ds(rf, 8), :] = pf
        ab = a_b[pl.ds(rb, 8), :]
        hb = ab * hb + b_b[pl.ds(rb, 8), :]
        pb = ab * pb
        b_b[pl.ds(rb, 8), :] = hb
        a_b[pl.ds(rb, 8), :] = pb
        return hf, pf, hb, pb

    zero = jnp.zeros((8, tc), F32)
    one = jnp.ones((8, tc), F32)
    hf, pf, hb, pb = lax.fori_loop(0, nb, local, (zero, one, zero, one), unroll=4)

    carry = h0_ref[s, 0:1, :]
    rows = []
    for k in range(8):
        rows.append(carry)
        carry = hf[k:k + 1] + pf[k:k + 1] * carry
    carry_f = jnp.concatenate(rows, axis=0)
    st_ref[s, 0:1, :] = carry
    carry = h0_ref[s, 1:2, :]
    rows = [None] * 8
    for k in range(7, -1, -1):
        rows[k] = carry
        carry = hb[k:k + 1] + pb[k:k + 1] * carry
    carry_b = jnp.concatenate(rows, axis=0)
    st_ref[s, 1:2, :] = carry

    sl = slice(s0, s0 + L)
    blocked = lambda ref: ref[sl, :].reshape(nb, 8, tc)
    h_sum = ((blocked(b_f) + blocked(a_f) * carry_f[None]) + (blocked(b_b) + blocked(a_b) * carry_b[None]))
    y_ref[sl, :] = h_sum.reshape(L, tc).astype(BF16)


def _rglru_gate_tasks(L, reset, ux_ref, cw_ref, cb_ref, wg_ref, br_ref, bi_ref, lam_ref,
                      a_f, b_f, a_b, b_b):
    chunk = {}

    def prepare():
        u = ux_ref[...].astype(F32)
        w = cw_ref[...]
        xr = (cb_ref[...] + _shift_time(u, 2, L) * w[0:1] + _shift_time(u, 1, L) * w[1:2]
              + u * w[2:3] + _shift_time(u, -1, L) * w[3:4])
        chunk["x"] = xr
        chunk["x_bf16"] = xr.astype(BF16)
        neg_lam = -lam_ref[...]
        softplus = jnp.maximum(neg_lam, 0.0) + jnp.log1p(jnp.exp(-jnp.abs(neg_lam)))
        chunk["rate"] = (-0.25 * RG_C) * softplus
        chunk["half_br"] = 0.5 * br_ref[...]
        chunk["half_bi"] = 0.5 * bi_ref[...]

    tasks = [prepare]
    for n in range(RG_TC // RG_BLOCK):
        cs = slice(n * RG_BLOCK, (n + 1) * RG_BLOCK)

        def project(n=n, cs=cs):
            chunk["gates", n] = jnp.dot(chunk["x_bf16"][:, cs], wg_ref[n].astype(BF16),
                                        preferred_element_type=F32)

        tasks.append(project)
        for d, (a_scr, b_scr) in enumerate(((a_f, b_f), (a_b, b_b))):
            for r0 in range(0, CHUNK, GATE_ROWS):
                def gate(n=n, cs=cs, d=d, a_scr=a_scr, b_scr=b_scr, r0=r0):
                    rows = slice(r0, r0 + GATE_ROWS)
                    gates = chunk["gates", n][rows]
                    c0 = 2 * d * RG_BLOCK
                    t_r = jnp.tanh(gates[:, c0:c0 + RG_BLOCK] + chunk["half_br"][d:d + 1, cs])
                    t_i = jnp.tanh(gates[:, c0 + RG_BLOCK:c0 + 2 * RG_BLOCK] + chunk["half_bi"][d:d + 1, cs])
                    rate = chunk["rate"][d:d + 1, cs]
                    th = jnp.tanh(rate * t_r + rate)
                    recip = 1.0 / (1.0 - th)
                    neg_th = -th
                    root = neg_th * lax.rsqrt(jnp.maximum(neg_th, TINY_F32))
                    half_mult = recip * root
                    if reset:
                        tpos = (r0 + lax.broadcasted_iota(jnp.int32, (GATE_ROWS, 1), 0)) & (L - 1)
                        half_mult = jnp.where(tpos == (0 if d == 0 else L - 1), 0.5, half_mult)
                    a_scr[rows, cs] = (1.0 + th) * recip
                    b_scr[rows, cs] = (half_mult * chunk["x"][rows, cs]) * (t_i + 1.0)

                tasks.append(gate)
    return tasks


def _mixers_kernel(hv_ref, hx1_ref, hx2_ref, fn_ref, ux_ref,
                   hwv_ref, hw1_ref, hw2_ref, hbv_ref, hb1_ref, hb2_ref, hbias_ref,
                   ps_ref, ns_ref, pl_ref, nl_ref, wfs_ref, wis_ref, wfl_ref, wil_ref,
                   cs_ref, fs_ref, fl_ref,
                   h0_ref, cw_ref, cb_ref, wg_ref, br_ref, bi_ref, lam_ref,
                   yh_ref, yf_ref, y_ref, st_ref, a_f, b_f, a_b, b_b):
    q = pl.program_id(1)

    def run(L, reset, p_ref, nyq_ref, wf_ref, wi_ref, f_ref):
        matmul_side = (_hyena_tasks(L, hv_ref, hx1_ref, hx2_ref, hwv_ref, hw1_ref, hw2_ref, hbv_ref,
                                    hb1_ref, hb2_ref, hbias_ref, p_ref, nyq_ref, wf_ref, wi_ref, yh_ref)
                       + _fnet_tasks(L, fn_ref, cs_ref, f_ref, yf_ref))
        vector_side = _rglru_gate_tasks(L, reset, ux_ref, cw_ref, cb_ref, wg_ref, br_ref, bi_ref,
                                        lam_ref, a_f, b_f, a_b, b_b)
        _interleave(matmul_side, vector_side)
        st_ref[...] = jnp.zeros(st_ref.shape, F32)
        for s in range(CHUNK // L):
            _scan_sequence(s, L, a_f, b_f, a_b, b_b, h0_ref, y_ref, st_ref)

    @pl.when(q < NQ_CTX)
    def _():
        run(SEQ, True, ps_ref, ns_ref, wfs_ref, wis_ref, fs_ref)

    @pl.when(q >= NQ_CTX)
    def _():
        run(DEC_SEQ, False, pl_ref, nl_ref, wfl_ref, wil_ref, fl_ref)


def _mixers(u, l, hy_conv_w, hy_conv_b, hy_bias, spectra_short, spectra_long, dft_short, dft_long,
            fnet_chan, fnet_short, fnet_long, h0, rg_conv_w, rg_conv_b, w_gates, b_r, b_i, lam):
    assert D_HY // HY_TC == D_FN // HY_TC == D_RG // RG_TC
    halves = D_HY // HY_TC
    p_short, nyq_short = spectra_short
    p_long, nyq_long = spectra_long
    ucol = lambda width, col0: pl.BlockSpec((CHUNK, width), lambda c, q: (q, col0 // width + c))
    hy_part = lambda rows, k: pl.BlockSpec((None, rows, HY_TC), lambda c, q: (l, 0, k * halves + c))
    hy_tile = lambda rows: pl.BlockSpec((None, rows, HY_TC), lambda c, q: (l, 0, c))
    rg_tile = lambda rows: pl.BlockSpec((None, rows, RG_TC), lambda c, q: (l, 0, c))
    const = lambda shape: pl.BlockSpec(shape, lambda c, q: (0, 0))
    planes = lambda L: pl.BlockSpec((None, 2, 2, L, HY_TC), lambda c, q: (l, 0, 0, 0, c))
    nyquist = pl.BlockSpec((None, 2, 1, HY_TC), lambda c, q: (l, 0, 0, c))
    state_spec = pl.BlockSpec((None, CTX_PER_CHUNK, 2, RG_TC), lambda c, q: (q, 0, 0, c))
    out_tile = lambda width: pl.BlockSpec((CHUNK, width), lambda c, q: (q, c))
    hy_conv_b = hy_conv_b.reshape(DEPTH, 1, 3 * D_HY)
    return pl.pallas_call(
        _mixers_kernel,
        grid=(halves, NQ),
        in_specs=[
            ucol(HY_TC, 0), ucol(HY_TC, D_HY), ucol(HY_TC, 2 * D_HY), ucol(HY_TC, COL_FN),
            ucol(RG_TC, COL_RX),
            hy_part(3, 0), hy_part(3, 1), hy_part(3, 2), hy_part(1, 0), hy_part(1, 1), hy_part(1, 2),
            hy_tile(2),
            planes(SEQ), nyquist, planes(DEC_SEQ), nyquist,
            const((2 * SEQ, SEQ)), const((SEQ, 2 * SEQ)),
            const((2 * DEC_SEQ, DEC_SEQ)), const((DEC_SEQ, 2 * DEC_SEQ)),
            const((FN_GROUP, 2 * FN_GROUP)), const((SEQ, 2 * SEQ)), const((DEC_SEQ, 2 * DEC_SEQ)),
            state_spec, rg_tile(4), rg_tile(1),
            pl.BlockSpec((None, RG_TC // RG_BLOCK, RG_BLOCK, 4 * RG_BLOCK), lambda c, q: (l, c, 0, 0)),
            rg_tile(2), rg_tile(2), rg_tile(2),
        ],
        out_specs=[out_tile(HY_TC), out_tile(HY_TC), out_tile(RG_TC), state_spec],
        out_shape=[jax.ShapeDtypeStruct((N_TOK, D_HY), BF16),
                   jax.ShapeDtypeStruct((N_TOK, D_FN), BF16),
                   jax.ShapeDtypeStruct((N_TOK, D_RG), BF16),
                   jax.ShapeDtypeStruct((NQ, CTX_PER_CHUNK, 2, D_RG), F32)],
        scratch_shapes=[pltpu.VMEM((CHUNK, RG_TC), F32)] * 4,
        compiler_params=_params("arbitrary", "arbitrary"),
        name="sequence_mixers",
    )(u, u, u, u, u, hy_conv_w, hy_conv_w, hy_conv_w, hy_conv_b, hy_conv_b, hy_conv_b, hy_bias,
      p_short, nyq_short, p_long, nyq_long, *dft_short, *dft_long, fnet_chan, fnet_short, fnet_long,
      h0, rg_conv_w, rg_conv_b.reshape(DEPTH, 1, D_RG), w_gates, b_r, b_i, lam)


def _merge_kernel(x_ref, ya_ref, yb_ref, hc_ref, mod_ref, g_ref, wy_ref, wga_ref, wgb_ref, wgc_ref,
                  wa_ref, wb_ref, wc_ref, wo_ref, o_ref, *, tm):
    q = pl.program_id(0) // (CHUNK // tm)
    gate = mod_ref[pl.ds(_mod_row(q), 1), 2 * D_MODEL:3 * D_MODEL]
    project = lambda y, w_ref: jnp.dot(y, w_ref[...].astype(BF16), preferred_element_type=F32)
    h = _modulated_norm(x_ref[...], g_ref[...], mod_ref, q, 0)
    ya = project(ya_ref[...], wa_ref)
    yb = project(yb_ref[...], wb_ref)
    rg = hc_ref[...].astype(F32) * _gelu_tanh(project(h, wy_ref))
    yc = project(rg.astype(BF16), wc_ref)
    mix = (_sigmoid(project(h, wga_ref)) * ya + _sigmoid(project(h, wgb_ref)) * yb
           + _sigmoid(project(h, wgc_ref)) * yc)
    o_ref[...] = x_ref[...] + gate * project(mix.astype(BF16), wo_ref)


def _merge(x, y_hy, y_fn, h_rg, mod, norm_g, w_in, w_a, w_b, w_c, w_o, l, tm=512):
    rows = lambda width: pl.BlockSpec((tm, width), lambda i: (i, 0))
    weight = lambda k: pl.BlockSpec((None, k, D_MODEL), lambda i: (l, 0, 0))
    in_cols = lambda col0: pl.BlockSpec((None, D_MODEL, D_MODEL), lambda i: (l, 0, col0 // D_MODEL))
    return pl.pallas_call(
        functools.partial(_merge_kernel, tm=tm),
        grid=(N_TOK // tm,),
        in_specs=[
            rows(D_MODEL), rows(D_HY), rows(D_FN), rows(D_RG),
            pl.BlockSpec((None, MOD_ROWS, 6 * D_MODEL), lambda i: (l, 0, 0)),
            pl.BlockSpec((None, 1, D_MODEL), lambda i: (l, 0, 0)),
            in_cols(COL_RY), in_cols(COL_G), in_cols(COL_G + D_MODEL), in_cols(COL_G + 2 * D_MODEL),
            weight(D_HY), weight(D_FN), weight(D_RG), weight(D_MODEL),
        ],
        out_specs=rows(D_MODEL),
        out_shape=jax.ShapeDtypeStruct((N_TOK, D_MODEL), F32),
        compiler_params=_params("arbitrary"),
        name="branch_merge",
    )(x, y_hy, y_fn, h_rg, mod, norm_g, w_in, w_in, w_in, w_in, w_a, w_b, w_c, w_o)


def _ffn_kernel(x_ref, mod_ref, g_ref, wg_ref, wu_ref, wd_ref, fg_ref, o_ref, h_scr, *, tm, final_norm):
    i = pl.program_id(0)
    f = pl.program_id(1)
    sub = tm // CHUNK

    def gated_mlp(h, wg, wu, wd):
        gt = jnp.dot(h, wg, preferred_element_type=F32)
        up = jnp.dot(h, wu, preferred_element_type=F32)
        act = (gt * _sigmoid(gt)) * up
        return jnp.dot(act.astype(BF16), wd, preferred_element_type=F32)

    @pl.when(f == 0)
    def _():
        wg, wu, wd = (w_ref[...].astype(BF16) for w_ref in (wg_ref, wu_ref, wd_ref))
        for s in range(sub):
            rows = slice(s * CHUNK, (s + 1) * CHUNK)
            h = _modulated_norm(x_ref[rows, :], g_ref[...], mod_ref, i * sub + s, 3)
            h_scr[rows, :] = h
            o_ref[rows, :] = gated_mlp(h, wg, wu, wd)

    last = pl.num_programs(1) - 1

    @pl.when((f != 0) & (f != last))
    def _():
        o_ref[...] += gated_mlp(h_scr[...], *(w_ref[...].astype(BF16) for w_ref in (wg_ref, wu_ref, wd_ref)))

    @pl.when(f == last)
    def _():
        wg, wu, wd = (w_ref[...].astype(BF16) for w_ref in (wg_ref, wu_ref, wd_ref))
        for s in range(sub):
            r = _mod_row(i * sub + s)
            gate = mod_ref[pl.ds(r, 1), 5 * D_MODEL:6 * D_MODEL]
            rows = slice(s * CHUNK, (s + 1) * CHUNK)
            out = x_ref[rows, :] + gate * (o_ref[rows, :] + gated_mlp(h_scr[rows, :], wg, wu, wd))
            if final_norm:
                out = _rmsnorm(out, fg_ref[...])
            o_ref[rows, :] = out


def _ffn(x, mod, norm_g, w_gu, w_down, final_g, l, final_norm, tm=2048, tf=256):
    nf = D_FF // tf
    return pl.pallas_call(
        functools.partial(_ffn_kernel, tm=tm, final_norm=final_norm),
        grid=(N_TOK // tm, nf),
        in_specs=[
            pl.BlockSpec((tm, D_MODEL), lambda i, f: (i, 0)),
            pl.BlockSpec((None, MOD_ROWS, 6 * D_MODEL), lambda i, f: (l, 0, 0)),
            pl.BlockSpec((None, 1, D_MODEL), lambda i, f: (l, 0, 0)),
            pl.BlockSpec((None, D_MODEL, tf), lambda i, f: (l, 0, f)),
            pl.BlockSpec((None, D_MODEL, tf), lambda i, f: (l, 0, nf + f)),
            pl.BlockSpec((None, tf, D_MODEL), lambda i, f: (l, f, 0)),
            pl.BlockSpec((1, D_MODEL), lambda i, f: (0, 0)),
        ],
        out_specs=pl.BlockSpec((tm, D_MODEL), lambda i, f: (i, 0)),
        out_shape=jax.ShapeDtypeStruct((N_TOK, D_MODEL), F32),
        scratch_shapes=[pltpu.VMEM((tm, D_MODEL), BF16)],
        compiler_params=_params("arbitrary", "arbitrary"),
        name="swiglu_ffn",
    )(x, mod, norm_g, w_gu, w_gu, w_down, final_g)


def kernel(x_prompt, x_sample, c, state_rglru, c_ctx, norm1_g, norm2_g, w_ada, b_ada, w_in,
           hy_conv_w, hy_conv_b, hy_f_w1, hy_f_b1, hy_f_w2, hy_f_b2, hy_f_w3, hy_f_freq, hy_bias,
           w_a, w_b, rg_conv_w, rg_conv_b, rg_wr, rg_br, rg_wi, rg_bi, rg_lam, w_c, w_o,
           w_gu, w_down, final_g):
    x = jnp.concatenate([_to_time_permuted(x_prompt, SEQ), _to_time_permuted(x_sample, DEC_SEQ)])
    cvec = jnp.concatenate([c_ctx[None, :], c, jnp.zeros((MOD_ROWS - 1 - DEC_BATCH, D_MODEL), F32)])
    mod = _modulation(cvec, w_ada, b_ada)

    dft_short = _as_bf16(*_hyena_dft_tables(SEQ))
    dft_long = _as_bf16(*_hyena_dft_tables(DEC_SEQ))
    fnet_short, fnet_chan = _as_bf16(*_fnet_tables(SEQ))
    fnet_long, = _as_bf16(_fnet_tables(DEC_SEQ)[0])
    filt = (hy_f_w1, hy_f_b1, hy_f_w2, hy_f_b2, hy_f_w3, hy_f_freq)
    p_short = _hyena_filters(SEQ, dft_short[0], *filt)
    p_long = _hyena_filters(DEC_SEQ, dft_long[0], *filt)

    w_gates = 0.5 * jnp.concatenate([rg_wr[:, 0], rg_wi[:, 0], rg_wr[:, 1], rg_wi[:, 1]], axis=-1)
    lat_h0 = jnp.pad(state_rglru.astype(F32).transpose(1, 0, 2, 3)[:, :, None],
                     ((0, 0), (0, 0), (0, CTX_PER_CHUNK - 1), (0, 0), (0, 0)))
    h0_all = jnp.concatenate([jnp.zeros((DEPTH, NQ_CTX, CTX_PER_CHUNK, 2, D_RG), F32), lat_h0], axis=1)
    norm1 = norm1_g.reshape(DEPTH, 1, D_MODEL)
    norm2 = norm2_g.reshape(DEPTH, 1, D_MODEL)
    final = final_g.reshape(1, D_MODEL)

    states = []
    for l in range(DEPTH):
        u = _input_projection(x, mod, norm1, w_in, l)
        y_hy, y_fn, h_rg, st = _mixers(
            u, l, hy_conv_w, hy_conv_b, hy_bias, p_short, p_long, dft_short, dft_long,
            fnet_chan, fnet_short, fnet_long, h0_all[l], rg_conv_w, rg_conv_b, w_gates,
            rg_br, rg_bi, rg_lam)
        states.append(st[:NQ_CTX].reshape(BATCH, 2, D_RG))
        x = _merge(x, y_hy, y_fn, h_rg, mod, norm1, w_in, w_a, w_b, w_c, w_o, l)
        x = _ffn(x, mod, norm2, w_gu, w_down, final, l, final_norm=(l == DEPTH - 1))

    y_prompt = _from_time_permuted(x[:N_CTX_TOK], BATCH, SEQ)
    y_sample = _from_time_permuted(x[N_CTX_TOK:], DEC_BATCH, DEC_SEQ)
    new_state = jnp.stack(states, axis=1).astype(x_prompt.dtype)
    return (y_prompt, y_sample, new_state)
```

```python
import functools
import math

import numpy as np
import jax
import jax.numpy as jnp
from jax import lax
from jax.experimental import pallas as pl
from jax.experimental.pallas import tpu as pltpu

F32 = jnp.float32
BF16 = jnp.bfloat16

D_MODEL = 1024
BATCH = 16
SEQ = 256
DEPTH = 4
DEC_BATCH = 4
DEC_SEQ = 1024
EPS = 1e-6
TINY_F32 = float(np.finfo(np.float32).tiny)
D_HY = 512
N_BANDS = 8
FILT_EMB = 1 + 2 * N_BANDS
FILT_HID = 64
DECAY_SLOW = -math.log(1e-2) / 1.5
DECAY_FAST = -math.log(1e-2) / 0.3
D_FN = 512
FN_GROUP = 128
N_FN_GROUPS = D_FN // FN_GROUP
D_RG = 1024
RG_BLOCK = 128
RG_C = 8.0
D_FF = -(-8 * D_MODEL // (3 * 256)) * 256
D_IN = 3 * D_HY + D_FN + 2 * D_RG + 3 * D_MODEL
COL_FN = 3 * D_HY
COL_RX = COL_FN + D_FN
COL_RY = COL_RX + D_RG
COL_G = COL_RY + D_RG

CHUNK = 1024
N_CTX_TOK = BATCH * SEQ
N_LAT_TOK = DEC_BATCH * DEC_SEQ
N_TOK = N_CTX_TOK + N_LAT_TOK
NQ_CTX = N_CTX_TOK // CHUNK
NQ = N_TOK // CHUNK
CTX_PER_CHUNK = CHUNK // SEQ
MOD_ROWS = 8
FEAT_PAD = 128
HY_TC = 256
RG_TC = 512
K_PIECE = 1024
GATE_ROWS = 256
SPLIT_DOT_ROWS = 1024
VMEM_LIMIT = 60 * 1024 * 1024

assert DEC_SEQ == CHUNK and CHUNK % SEQ == 0 and N_CTX_TOK % CHUNK == 0
assert SEQ % 16 == 0 and DEC_SEQ % 16 == 0
assert 1 + DEC_BATCH <= MOD_ROWS


def _params(*sem):
    return pltpu.CompilerParams(dimension_semantics=sem, vmem_limit_bytes=VMEM_LIMIT)


def _mod_row(q):
    return jnp.maximum(q - (NQ_CTX - 1), 0)


def _rmsnorm(x, g):
    return x * lax.rsqrt(jnp.mean(x * x, axis=-1, keepdims=True) + EPS) * g


def _sigmoid(x):
    return 0.5 * jnp.tanh(0.5 * x) + 0.5


def _dot_split3(a, w):
    rows = a.shape[0]
    a_hi = a.astype(BF16)
    a_lo = (a - a_hi.astype(F32)).astype(BF16)
    w_hi = w.astype(BF16)
    w_lo = (w - w_hi.astype(F32)).astype(BF16)
    heads = jnp.dot(jnp.concatenate([a_hi, a_lo], axis=0), w_hi, preferred_element_type=F32)
    return (heads[:rows] + heads[rows:]) + jnp.dot(a_hi, w_lo, preferred_element_type=F32)


def _gelu_tanh(x):
    inner = x * (math.sqrt(2.0 / math.pi) * 0.044715 * (x * x) + math.sqrt(2.0 / math.pi))
    half = 0.5 * x
    return half * jnp.tanh(inner) + half


def _time_of_row(L):
    p = np.arange(L)
    return (p % 8) * (L // 8) + p // 8


def _to_time_permuted(x, L):
    b = x.shape[0]
    return x.reshape(b, 8, L // 8, x.shape[-1]).transpose(0, 2, 1, 3).reshape(b * L, x.shape[-1])


def _from_time_permuted(x, b, L):
    return x.reshape(b, L // 8, 8, x.shape[-1]).transpose(0, 2, 1, 3).reshape(b, L, x.shape[-1])


def _shift_time(x, k, L):
    sub = lax.broadcasted_iota(jnp.int32, (8, 1), 0)
    out = []
    for s0 in range(0, x.shape[0], L):
        xs = x[s0:s0 + L]
        if k > 0:
            wrap = [jnp.where(sub == 0, 0.0, pltpu.roll(xs[L - 8 * (k - i):L - 8 * (k - i - 1)], 1, axis=0))
                    for i in range(k)]
            out += wrap + [xs[:L - 8 * k]]
        else:
            wrap = [jnp.where(sub == 7, 0.0, pltpu.roll(xs[8 * i:8 * (i + 1)], 7, axis=0))
                    for i in range(-k)]
            out += [xs[-8 * k:]] + wrap
    return jnp.concatenate(out, axis=0)


def _as_bf16(*tables):
    return tuple(jnp.asarray(t, dtype=F32).astype(BF16) for t in tables)


def _angle_table(n_rows, n_cols, period):
    prod = np.outer(np.arange(n_rows, dtype=np.int64), np.arange(n_cols, dtype=np.int64)) % period
    return 2.0 * np.pi * prod.astype(np.float64) / period


@functools.lru_cache(maxsize=None)
def _hyena_dft_tables(L):
    n = 2 * L
    ang = _angle_table(L, L, n)
    alt = np.where(np.arange(L) % 2 == 0, 1.0, -1.0)
    cos_f, sin_f = np.cos(ang), np.sin(ang)
    sin_f[0, :] = alt
    fwd = np.concatenate([cos_f, sin_f], axis=0)
    cos_i, sin_i = 2.0 / n * np.cos(ang), 2.0 / n * np.sin(ang)
    cos_i[:, 0] = 1.0 / n
    sin_i[:, 0] = alt / n
    inv = np.concatenate([cos_i, sin_i], axis=1)
    perm = _time_of_row(L)
    return fwd[:, perm].astype(np.float32), inv[perm, :].astype(np.float32)


@functools.lru_cache(maxsize=None)
def _fnet_tables(L):
    perm = _time_of_row(L)
    ang = _angle_table(L, L, L)[perm][:, perm]
    seq = np.concatenate([np.cos(ang), -np.sin(ang)], axis=1)
    ang_c = _angle_table(FN_GROUP, FN_GROUP, FN_GROUP)
    chan = np.concatenate([np.cos(ang_c), np.sin(ang_c)], axis=1)
    return seq.astype(np.float32), chan.astype(np.float32)


@functools.lru_cache(maxsize=None)
def _filter_tables(L):
    t = np.arange(L, dtype=np.float32) / np.float32(L)
    ang = 2.0 * np.pi * t[:, None].astype(np.float64) * np.arange(1, N_BANDS + 1, dtype=np.float64)
    feats = np.zeros((L, FEAT_PAD), np.float64)
    feats[:, 0] = t
    feats[:, 1:1 + N_BANDS] = np.sin(ang)
    feats[:, 1 + N_BANDS:FILT_EMB] = np.cos(ang)
    deltas = np.linspace(DECAY_SLOW, DECAY_FAST, D_HY, dtype=np.float32).astype(np.float64)
    decay = np.exp(-t[:, None].astype(np.float64) * deltas)
    perm = _time_of_row(L)
    return jnp.asarray(feats[perm], dtype=F32), jnp.asarray(decay[perm], dtype=F32)


def _mod_kernel(ca_ref, cb_ref, wa_ref, wb_ref, b_ref, o_ref):
    silu = lambda c_ref: c_ref[...] * _sigmoid(c_ref[...])
    part = _dot_split3(silu(ca_ref), wa_ref[...]) + _dot_split3(silu(cb_ref), wb_ref[...])

    @pl.when(pl.program_id(1) == 0)
    def _():
        o_ref[...] = part + b_ref[...]

    @pl.when(pl.program_id(1) != 0)
    def _():
        o_ref[...] += part


def _modulation(cvec, w_ada, b_ada):
    tk = 256
    nk = D_MODEL // tk
    c_slabs = cvec.reshape(MOD_ROWS, nk, tk).transpose(1, 0, 2)
    c_spec = lambda half: pl.BlockSpec((None, MOD_ROWS, tk), lambda l, k: (2 * k + half, 0, 0))
    w_spec = lambda half: pl.BlockSpec((None, tk, 6 * D_MODEL), lambda l, k: (l, 2 * k + half, 0))
    return pl.pallas_call(
        _mod_kernel,
        grid=(DEPTH, nk // 2),
        in_specs=[
            c_spec(0), c_spec(1), w_spec(0), w_spec(1),
            pl.BlockSpec((None, 1, 6 * D_MODEL), lambda l, k: (l, 0, 0)),
        ],
        out_specs=pl.BlockSpec((None, MOD_ROWS, 6 * D_MODEL), lambda l, k: (l, 0, 0)),
        out_shape=jax.ShapeDtypeStruct((DEPTH, MOD_ROWS, 6 * D_MODEL), F32),
        compiler_params=_params("arbitrary", "arbitrary"),
        name="adaln_modulation",
    )(c_slabs, c_slabs, w_ada, w_ada, b_ada.reshape(DEPTH, 1, 6 * D_MODEL))


def _filter_kernel(feats_ref, decay_ref, wf_ref, w1_ref, b1_ref, w2_ref, b2_ref, w3_ref, fq_ref,
                   p_ref, nyq_ref, hid_scr, *, L):
    @pl.when(pl.program_id(1) == 0)
    def _():
        freq = fq_ref[...]
        h = jnp.sin(freq * (_dot_split3(feats_ref[...], w1_ref[...]) + b1_ref[...]))
        hid_scr[...] = jnp.sin(freq * (_dot_split3(h, w2_ref[...]) + b2_ref[...]))

    h = _dot_split3(hid_scr[...], w3_ref[...])
    decay = decay_ref[...]
    row = lax.broadcasted_iota(jnp.int32, (L, 1), 0)
    h_fwd = h[:, :D_HY] * decay
    h_bwd = jnp.where(row == 0, 0.0, h[:, D_HY:] * decay)
    ssq = jnp.sum(h_fwd * h_fwd, axis=0, keepdims=True) + jnp.sum(h_bwd * h_bwd, axis=0, keepdims=True)
    scale = lax.rsqrt(ssq + EPS)
    even = (h_fwd + h_bwd) * scale
    odd = (h_bwd - h_fwd) * scale
    k_re = jnp.dot(wf_ref[0:L, :], even.astype(BF16), preferred_element_type=F32)
    k_im = jnp.dot(wf_ref[L:2 * L, :], odd.astype(BF16), preferred_element_type=F32)
    alt = jnp.where(((row >> 3) & 1) == 0, 1.0, -1.0)
    k_nyq = jnp.sum(even * alt, axis=0, keepdims=True)
    p_ref[0] = k_re
    p_ref[1] = jnp.where(row == 0, 0.0, k_im)
    nyq_ref[...] = k_nyq


def _hyena_filters(L, wf, fw1, fb1, fw2, fb2, fw3, ffreq):
    feats, decay = _filter_tables(L)
    w1 = jnp.pad(fw1, ((0, 0), (0, FEAT_PAD - FILT_EMB), (0, 0)))
    const = lambda shape: pl.BlockSpec(shape, lambda l, o: (0,) * len(shape))
    per_layer = lambda *shape: pl.BlockSpec((None,) + shape, lambda l, o: (l,) + (0,) * len(shape))
    return pl.pallas_call(
        functools.partial(_filter_kernel, L=L),
        grid=(DEPTH, 2),
        in_specs=[
            const((L, FEAT_PAD)), const((L, D_HY)), const((2 * L, L)),
            per_layer(FEAT_PAD, FILT_HID), per_layer(1, FILT_HID),
            per_layer(FILT_HID, FILT_HID), per_layer(1, FILT_HID),
            pl.BlockSpec((None, FILT_HID, 2 * D_HY), lambda l, o: (l, 0, o)),
            per_layer(1, FILT_HID),
        ],
        out_specs=[pl.BlockSpec((None, None, 2, L, D_HY), lambda l, o: (l, o, 0, 0, 0)),
                   pl.BlockSpec((None, None, 1, D_HY), lambda l, o: (l, o, 0, 0))],
        out_shape=[jax.ShapeDtypeStruct((DEPTH, 2, 2, L, D_HY), F32),
                   jax.ShapeDtypeStruct((DEPTH, 2, 1, D_HY), F32)],
        scratch_shapes=[pltpu.VMEM((L, FILT_HID), F32)],
        compiler_params=_params("arbitrary", "arbitrary"),
        name=f"hyena_filters_{L}",
    )(feats, decay, wf, w1, fb1.reshape(DEPTH, 1, FILT_HID), fw2, fb2.reshape(DEPTH, 1, FILT_HID),
      fw3, ffreq.reshape(DEPTH, 1, FILT_HID))


def _modulated_norm(x, g, mod_ref, q, col):
    r = _mod_row(q)
    shift = mod_ref[pl.ds(r, 1), col * D_MODEL:(col + 1) * D_MODEL]
    scale = mod_ref[pl.ds(r, 1), (col + 1) * D_MODEL:(col + 2) * D_MODEL]
    return (_rmsnorm(x, g) * (1.0 + scale) + shift).astype(BF16)


def _inproj_kernel(x_ref, mod_ref, g_ref, w_ref, u_ref, *, tm, part_rows):
    i = pl.program_id(0)
    w = w_ref[...].astype(BF16)
    for s in range(tm // part_rows):
        rows = slice(s * part_rows, (s + 1) * part_rows)
        h = _modulated_norm(x_ref[rows, :], g_ref[...], mod_ref, (i * tm + s * part_rows) // CHUNK, 0)
        u_ref[rows, :] = jnp.dot(h, w, preferred_element_type=F32).astype(BF16)


def _input_projection(x, mod, norm_g, w_in, l, tm=1024, part_rows=512):
    assert CHUNK % part_rows == 0 and tm % part_rows == 0
    return pl.pallas_call(
        functools.partial(_inproj_kernel, tm=tm, part_rows=part_rows),
        grid=(N_TOK // tm,),
        in_specs=[
            pl.BlockSpec((tm, D_MODEL), lambda i: (i, 0)),
            pl.BlockSpec((None, MOD_ROWS, 6 * D_MODEL), lambda i: (l, 0, 0)),
            pl.BlockSpec((None, 1, D_MODEL), lambda i: (l, 0, 0)),
            pl.BlockSpec((None, D_MODEL, COL_RY), lambda i: (l, 0, 0)),
        ],
        out_specs=pl.BlockSpec((tm, COL_RY), lambda i: (i, 0)),
        out_shape=jax.ShapeDtypeStruct((N_TOK, COL_RY), BF16),
        compiler_params=_params("arbitrary"),
        name="input_projection",
    )(x, mod, norm_g, w_in)


def _interleave(*task_lists):
    steps = max(len(tasks) for tasks in task_lists)
    done = [0] * len(task_lists)
    for step in range(1, steps + 1):
        for i, tasks in enumerate(task_lists):
            while done[i] < (step * len(tasks)) // steps:
                tasks[done[i]]()
                done[i] += 1


def _dot_by_k_tiles(lhs_ref, rhs_of, out):
    k_tile = min(K_PIECE, lhs_ref.shape[1])

    def piece(k):
        def run():
            cols = slice(k * k_tile, (k + 1) * k_tile)
            rhs = rhs_of()[cols]
            n_split = 2 if lhs_ref.shape[0] >= SPLIT_DOT_ROWS else 1
            rows = lhs_ref.shape[0] // n_split
            part = jnp.concatenate(
                [jnp.dot(lhs_ref[r * rows:(r + 1) * rows, cols], rhs, preferred_element_type=F32)
                 for r in range(n_split)], axis=0)
            out["acc"] = part if k == 0 else out["acc"] + part
        return run
    return [piece(k) for k in range(lhs_ref.shape[1] // k_tile)]


def _spectrum_product(spec, p_ref, nyq_ref, order, L):
    z_re, z_sn = spec[:L], spec[L:]
    p_re, p_im = p_ref[order, 0], p_ref[order, 1]
    y_re = z_re * p_re + z_sn * p_im
    y_sn = z_sn * p_re - z_re * p_im
    first = lax.broadcasted_iota(jnp.int32, (8, 1), 0) == 0
    y_sn_top = jnp.where(first, z_sn[:8] * nyq_ref[order], y_sn[:8])
    return jnp.concatenate([y_re, y_sn_top, y_sn[8:]], axis=0).astype(BF16)


def _hyena_tasks(L, v_ref, x1_ref, x2_ref, wv_ref, w1_ref, w2_ref, bv_ref, b1_ref, b2_ref, hb_ref,
                 p_ref, nyq_ref, wf_ref, wi_ref, o_ref):
    chunk = {}

    def prepare():
        def short_conv(u_ref, w_ref, b_ref):
            u = u_ref[...].astype(F32)
            w = w_ref[...]
            return (b_ref[...] + _shift_time(u, 1, L) * w[0:1] + u * w[1:2]
                    + _shift_time(u, -1, L) * w[2:3])
        chunk["v"] = short_conv(v_ref, wv_ref, bv_ref)
        chunk["x1"] = short_conv(x1_ref, w1_ref, b1_ref)
        chunk["x2"] = short_conv(x2_ref, w2_ref, b2_ref)
        chunk["bias"] = hb_ref[...]

    tasks = [prepare]
    for s in range(CHUNK // L):
        rows = slice(s * L, (s + 1) * L)
        seq = {}

        def begin(seq=seq, rows=rows):
            seq["in0"] = chunk["v"][rows]
            seq["in0_bf16"] = seq["in0"].astype(BF16)

        tasks.append(begin)
        for order in (0, 1):
            spec, conv = {}, {}
            tasks += _dot_by_k_tiles(wf_ref, lambda seq=seq, order=order: seq[f"in{order}_bf16"], spec)

            def pointwise(seq=seq, spec=spec, order=order):
                seq[f"y{order}"] = _spectrum_product(spec["acc"], p_ref, nyq_ref, order, L)

            tasks.append(pointwise)
            tasks += _dot_by_k_tiles(wi_ref, lambda seq=seq, order=order: seq[f"y{order}"], conv)

            def gate(seq=seq, conv=conv, order=order, rows=rows):
                x = chunk["x1" if order == 0 else "x2"][rows]
                z = seq[f"in{order}"]
                out = x * (conv["acc"] + chunk["bias"][order:order + 1] * z)
                if order == 0:
                    seq["in1"] = out
                    seq["in1_bf16"] = out.astype(BF16)
                else:
                    o_ref[rows, :] = out.astype(BF16)

            tasks.append(gate)
    return tasks


def _fnet_tasks(L, u_ref, cs_ref, f_ref, o_ref):
    chunk = {}

    def prepare():
        x = u_ref[...].astype(BF16)
        parts_c, parts_s = [], []
        for g in range(x.shape[1] // FN_GROUP):
            r = jnp.dot(x[:, g * FN_GROUP:(g + 1) * FN_GROUP], cs_ref[...], preferred_element_type=F32)
            parts_c.append(r[:, :FN_GROUP])
            parts_s.append(r[:, FN_GROUP:])
        chunk["xc"] = jnp.concatenate(parts_c, axis=1).astype(BF16)
        chunk["xs"] = jnp.concatenate(parts_s, axis=1).astype(BF16)

    tasks = [prepare]
    norm = 1.0 / math.sqrt(L * FN_GROUP)
    for s in range(CHUNK // L):
        rows = slice(s * L, (s + 1) * L)
        seq, out = {}, {}

        def stack(seq=seq, rows=rows):
            seq["stacked"] = jnp.concatenate([chunk["xc"][rows], chunk["xs"][rows]], axis=0)

        def finish(out=out, rows=rows):
            o_ref[rows, :] = (out["acc"] * norm).astype(BF16)

        tasks += [stack] + _dot_by_k_tiles(f_ref, lambda seq=seq: seq["stacked"], out) + [finish]
    return tasks


def _scan_sequence(s, L, a_f, b_f, a_b, b_b, h0_ref, y_ref, st_ref):
    nb = L // 8
    tc = a_f.shape[1]
    s0 = s * L

    def local(j, carry):
        hf, pf, hb, pb = carry
        rf = pl.multiple_of(s0 + 8 * j, 8)
        rb = pl.multiple_of(s0 + L - 8 - 8 * j, 8)
        af = a_f[pl.ds(rf, 8), :]
        hf = af * hf + b_f[pl.ds(rf, 8), :]
        pf = af * pf
        b_f[pl.ds(rf, 8), :] = hf
        a_f[pl.ds(rf, 8), :] = pf
        ab = a_b[pl.ds(rb, 8), :]
        hb = ab * hb + b_b[pl.ds(rb, 8), :]
        pb = ab * pb
        b_b[pl.ds(rb, 8), :] = hb
        a_b[pl.ds(rb, 8), :] = pb
        return hf, pf, hb, pb

    zero = jnp.zeros((8, tc), F32)
    one = jnp.ones((8, tc), F32)
    hf, pf, hb, pb = lax.fori_loop(0, nb, local, (zero, one, zero, one), unroll=4)

    carry = h0_ref[s, 0:1, :]
    rows = []
    for k in range(8):
        rows.append(carry)
        carry = hf[k:k + 1] + pf[k:k + 1] * carry
    carry_f = jnp.concatenate(rows, axis=0)
    st_ref[s, 0:1, :] = carry
    carry = h0_ref[s, 1:2, :]
    rows = [None] * 8
    for k in range(7, -1, -1):
        rows[k] = carry
        carry = hb[k:k + 1] + pb[k:k + 1] * carry
    carry_b = jnp.concatenate(rows, axis=0)
    st_ref[s, 1:2, :] = carry

    sl = slice(s0, s0 + L)
    blocked = lambda ref: ref[sl, :].reshape(nb, 8, tc)
    h_sum = ((blocked(b_f) + blocked(a_f) * carry_f[None]) + (blocked(b_b) + blocked(a_b) * carry_b[None]))
    y_ref[sl, :] = h_sum.reshape(L, tc).astype(BF16)


def _rglru_gate_tasks(L, reset, ux_ref, cw_ref, cb_ref, wg_ref, br_ref, bi_ref, lam_ref,
                      a_f, b_f, a_b, b_b):
    chunk = {}

    def prepare():
        u = ux_ref[...].astype(F32)
        w = cw_ref[...]
        xr = (cb_ref[...] + _shift_time(u, 2, L) * w[0:1] + _shift_time(u, 1, L) * w[1:2]
              + u * w[2:3] + _shift_time(u, -1, L) * w[3:4])
        chunk["x"] = xr
        chunk["x_bf16"] = xr.astype(BF16)
        neg_lam = -lam_ref[...]
        softplus = jnp.maximum(neg_lam, 0.0) + jnp.log1p(jnp.exp(-jnp.abs(neg_lam)))
        chunk["rate"] = (-0.25 * RG_C) * softplus
        chunk["half_br"] = 0.5 * br_ref[...]
        chunk["half_bi"] = 0.5 * bi_ref[...]

    tasks = [prepare]
    for n in range(RG_TC // RG_BLOCK):
        cs = slice(n * RG_BLOCK, (n + 1) * RG_BLOCK)

        def project(n=n, cs=cs):
            chunk["gates", n] = jnp.dot(chunk["x_bf16"][:, cs], wg_ref[n].astype(BF16),
                                        preferred_element_type=F32)

        tasks.append(project)
        for d, (a_scr, b_scr) in enumerate(((a_f, b_f), (a_b, b_b))):
            for r0 in range(0, CHUNK, GATE_ROWS):
                def gate(n=n, cs=cs, d=d, a_scr=a_scr, b_scr=b_scr, r0=r0):
                    rows = slice(r0, r0 + GATE_ROWS)
                    gates = chunk["gates", n][rows]
                    c0 = 2 * d * RG_BLOCK
                    t_r = jnp.tanh(gates[:, c0:c0 + RG_BLOCK] + chunk["half_br"][d:d + 1, cs])
                    t_i = jnp.tanh(gates[:, c0 + RG_BLOCK:c0 + 2 * RG_BLOCK] + chunk["half_bi"][d:d + 1, cs])
                    rate = chunk["rate"][d:d + 1, cs]
                    th = jnp.tanh(rate * t_r + rate)
                    recip = 1.0 / (1.0 - th)
                    neg_th = -th
                    root = neg_th * lax.rsqrt(jnp.maximum(neg_th, TINY_F32))
                    half_mult = recip * root
                    if reset:
                        tpos = (r0 + lax.broadcasted_iota(jnp.int32, (GATE_ROWS, 1), 0)) & (L - 1)
                        half_mult = jnp.where(tpos == (0 if d == 0 else L - 1), 0.5, half_mult)
                    a_scr[rows, cs] = (1.0 + th) * recip
                    b_scr[rows, cs] = (half_mult * chunk["x"][rows, cs]) * (t_i + 1.0)

                tasks.append(gate)
    return tasks


def _mixers_kernel(hv_ref, hx1_ref, hx2_ref, fn_ref, ux_ref,
                   hwv_ref, hw1_ref, hw2_ref, hbv_ref, hb1_ref, hb2_ref, hbias_ref,
                   ps_ref, ns_ref, pl_ref, nl_ref, wfs_ref, wis_ref, wfl_ref, wil_ref,
                   cs_ref, fs_ref, fl_ref,
                   h0_ref, cw_ref, cb_ref, wg_ref, br_ref, bi_ref, lam_ref,
                   yh_ref, yf_ref, y_ref, st_ref, a_f, b_f, a_b, b_b):
    q = pl.program_id(1)

    def run(L, reset, p_ref, nyq_ref, wf_ref, wi_ref, f_ref):
        matmul_side = (_hyena_tasks(L, hv_ref, hx1_ref, hx2_ref, hwv_ref, hw1_ref, hw2_ref, hbv_ref,
                                    hb1_ref, hb2_ref, hbias_ref, p_ref, nyq_ref, wf_ref, wi_ref, yh_ref)
                       + _fnet_tasks(L, fn_ref, cs_ref, f_ref, yf_ref))
        vector_side = _rglru_gate_tasks(L, reset, ux_ref, cw_ref, cb_ref, wg_ref, br_ref, bi_ref,
                                        lam_ref, a_f, b_f, a_b, b_b)
        _interleave(matmul_side, vector_side)
        st_ref[...] = jnp.zeros(st_ref.shape, F32)
        for s in range(CHUNK // L):
            _scan_sequence(s, L, a_f, b_f, a_b, b_b, h0_ref, y_ref, st_ref)

    @pl.when(q < NQ_CTX)
    def _():
        run(SEQ, True, ps_ref, ns_ref, wfs_ref, wis_ref, fs_ref)

    @pl.when(q >= NQ_CTX)
    def _():
        run(DEC_SEQ, False, pl_ref, nl_ref, wfl_ref, wil_ref, fl_ref)


def _mixers(u, l, hy_conv_w, hy_conv_b, hy_bias, spectra_short, spectra_long, dft_short, dft_long,
            fnet_chan, fnet_short, fnet_long, h0, rg_conv_w, rg_conv_b, w_gates, b_r, b_i, lam):
    assert D_HY // HY_TC == D_FN // HY_TC == D_RG // RG_TC
    halves = D_HY // HY_TC
    p_short, nyq_short = spectra_short
    p_long, nyq_long = spectra_long
    ucol = lambda width, col0: pl.BlockSpec((CHUNK, width), lambda c, q: (q, col0 // width + c))
    hy_part = lambda rows, k: pl.BlockSpec((None, rows, HY_TC), lambda c, q: (l, 0, k * halves + c))
    hy_tile = lambda rows: pl.BlockSpec((None, rows, HY_TC), lambda c, q: (l, 0, c))
    rg_tile = lambda rows: pl.BlockSpec((None, rows, RG_TC), lambda c, q: (l, 0, c))
    const = lambda shape: pl.BlockSpec(shape, lambda c, q: (0, 0))
    planes = lambda L: pl.BlockSpec((None, 2, 2, L, HY_TC), lambda c, q: (l, 0, 0, 0, c))
    nyquist = pl.BlockSpec((None, 2, 1, HY_TC), lambda c, q: (l, 0, 0, c))
    state_spec = pl.BlockSpec((None, CTX_PER_CHUNK, 2, RG_TC), lambda c, q: (q, 0, 0, c))
    out_tile = lambda width: pl.BlockSpec((CHUNK, width), lambda c, q: (q, c))
    hy_conv_b = hy_conv_b.reshape(DEPTH, 1, 3 * D_HY)
    return pl.pallas_call(
        _mixers_kernel,
        grid=(halves, NQ),
        in_specs=[
            ucol(HY_TC, 0), ucol(HY_TC, D_HY), ucol(HY_TC, 2 * D_HY), ucol(HY_TC, COL_FN),
            ucol(RG_TC, COL_RX),
            hy_part(3, 0), hy_part(3, 1), hy_part(3, 2), hy_part(1, 0), hy_part(1, 1), hy_part(1, 2),
            hy_tile(2),
            planes(SEQ), nyquist, planes(DEC_SEQ), nyquist,
            const((2 * SEQ, SEQ)), const((SEQ, 2 * SEQ)),
            const((2 * DEC_SEQ, DEC_SEQ)), const((DEC_SEQ, 2 * DEC_SEQ)),
            const((FN_GROUP, 2 * FN_GROUP)), const((SEQ, 2 * SEQ)), const((DEC_SEQ, 2 * DEC_SEQ)),
            state_spec, rg_tile(4), rg_tile(1),
            pl.BlockSpec((None, RG_TC // RG_BLOCK, RG_BLOCK, 4 * RG_BLOCK), lambda c, q: (l, c, 0, 0)),
            rg_tile(2), rg_tile(2), rg_tile(2),
        ],
        out_specs=[out_tile(HY_TC), out_tile(HY_TC), out_tile(RG_TC), state_spec],
        out_shape=[jax.ShapeDtypeStruct((N_TOK, D_HY), BF16),
                   jax.ShapeDtypeStruct((N_TOK, D_FN), BF16),
                   jax.ShapeDtypeStruct((N_TOK, D_RG), BF16),
                   jax.ShapeDtypeStruct((NQ, CTX_PER_CHUNK, 2, D_RG), F32)],
        scratch_shapes=[pltpu.VMEM((CHUNK, RG_TC), F32)] * 4,
        compiler_params=_params("arbitrary", "arbitrary"),
        name="sequence_mixers",
    )(u, u, u, u, u, hy_conv_w, hy_conv_w, hy_conv_w, hy_conv_b, hy_conv_b, hy_conv_b, hy_bias,
      p_short, nyq_short, p_long, nyq_long, *dft_short, *dft_long, fnet_chan, fnet_short, fnet_long,
      h0, rg_conv_w, rg_conv_b.reshape(DEPTH, 1, D_RG), w_gates, b_r, b_i, lam)


def _merge_kernel(x_ref, ya_ref, yb_ref, hc_ref, mod_ref, g_ref, wy_ref, wga_ref, wgb_ref, wgc_ref,
                  wa_ref, wb_ref, wc_ref, wo_ref, o_ref, *, tm):
    q = pl.program_id(0) // (CHUNK // tm)
    gate = mod_ref[pl.ds(_mod_row(q), 1), 2 * D_MODEL:3 * D_MODEL]
    project = lambda y, w_ref: jnp.dot(y, w_ref[...].astype(BF16), preferred_element_type=F32)
    h = _modulated_norm(x_ref[...], g_ref[...], mod_ref, q, 0)
    ya = project(ya_ref[...], wa_ref)
    yb = project(yb_ref[...], wb_ref)
    rg = hc_ref[...].astype(F32) * _gelu_tanh(project(h, wy_ref))
    yc = project(rg.astype(BF16), wc_ref)
    mix = (_sigmoid(project(h, wga_ref)) * ya + _sigmoid(project(h, wgb_ref)) * yb
           + _sigmoid(project(h, wgc_ref)) * yc)
    o_ref[...] = x_ref[...] + gate * project(mix.astype(BF16), wo_ref)


def _merge(x, y_hy, y_fn, h_rg, mod, norm_g, w_in, w_a, w_b, w_c, w_o, l, tm=512):
    rows = lambda width: pl.BlockSpec((tm, width), lambda i: (i, 0))
    weight = lambda k: pl.BlockSpec((None, k, D_MODEL), lambda i: (l, 0, 0))
    in_cols = lambda col0: pl.BlockSpec((None, D_MODEL, D_MODEL), lambda i: (l, 0, col0 // D_MODEL))
    return pl.pallas_call(
        functools.partial(_merge_kernel, tm=tm),
        grid=(N_TOK // tm,),
        in_specs=[
            rows(D_MODEL), rows(D_HY), rows(D_FN), rows(D_RG),
            pl.BlockSpec((None, MOD_ROWS, 6 * D_MODEL), lambda i: (l, 0, 0)),
            pl.BlockSpec((None, 1, D_MODEL), lambda i: (l, 0, 0)),
            in_cols(COL_RY), in_cols(COL_G), in_cols(COL_G + D_MODEL), in_cols(COL_G + 2 * D_MODEL),
            weight(D_HY), weight(D_FN), weight(D_RG), weight(D_MODEL),
        ],
        out_specs=rows(D_MODEL),
        out_shape=jax.ShapeDtypeStruct((N_TOK, D_MODEL), F32),
        compiler_params=_params("arbitrary"),
        name="branch_merge",
    )(x, y_hy, y_fn, h_rg, mod, norm_g, w_in, w_in, w_in, w_in, w_a, w_b, w_c, w_o)


def _ffn_kernel(x_ref, mod_ref, g_ref, wg_ref, wu_ref, wd_ref, fg_ref, o_ref, h_scr, *, tm, final_norm):
    i = pl.program_id(0)
    f = pl.program_id(1)
    sub = tm // CHUNK

    def gated_mlp(h, wg, wu, wd):
        gt = jnp.dot(h, wg, preferred_element_type=F32)
        up = jnp.dot(h, wu, preferred_element_type=F32)
        act = (gt * _sigmoid(gt)) * up
        return jnp.dot(act.astype(BF16), wd, preferred_element_type=F32)

    @pl.when(f == 0)
    def _():
        wg, wu, wd = (w_ref[...].astype(BF16) for w_ref in (wg_ref, wu_ref, wd_ref))
        for s in range(sub):
            rows = slice(s * CHUNK, (s + 1) * CHUNK)
            h = _modulated_norm(x_ref[rows, :], g_ref[...], mod_ref, i * sub + s, 3)
            h_scr[rows, :] = h
            o_ref[rows, :] = gated_mlp(h, wg, wu, wd)

    last = pl.num_programs(1) - 1

    @pl.when((f != 0) & (f != last))
    def _():
        o_ref[...] += gated_mlp(h_scr[...], *(w_ref[...].astype(BF16) for w_ref in (wg_ref, wu_ref, wd_ref)))

    @pl.when(f == last)
    def _():
        wg, wu, wd = (w_ref[...].astype(BF16) for w_ref in (wg_ref, wu_ref, wd_ref))
        for s in range(sub):
            r = _mod_row(i * sub + s)
            gate = mod_ref[pl.ds(r, 1), 5 * D_MODEL:6 * D_MODEL]
            rows = slice(s * CHUNK, (s + 1) * CHUNK)
            out = x_ref[rows, :] + gate * (o_ref[rows, :] + gated_mlp(h_scr[rows, :], wg, wu, wd))
            if final_norm:
                out = _rmsnorm(out, fg_ref[...])
            o_ref[rows, :] = out


def _ffn(x, mod, norm_g, w_gu, w_down, final_g, l, final_norm, tm=2048, tf=256):
    nf = D_FF // tf
    return pl.pallas_call(
        functools.partial(_ffn_kernel, tm=tm, final_norm=final_norm),
        grid=(N_TOK // tm, nf),
        in_specs=[
            pl.BlockSpec((tm, D_MODEL), lambda i, f: (i, 0)),
            pl.BlockSpec((None, MOD_ROWS, 6 * D_MODEL), lambda i, f: (l, 0, 0)),
            pl.BlockSpec((None, 1, D_MODEL), lambda i, f: (l, 0, 0)),
            pl.BlockSpec((None, D_MODEL, tf), lambda i, f: (l, 0, f)),
            pl.BlockSpec((None, D_MODEL, tf), lambda i, f: (l, 0, nf + f)),
            pl.BlockSpec((None, tf, D_MODEL), lambda i, f: (l, f, 0)),
            pl.BlockSpec((1, D_MODEL), lambda i, f: (0, 0)),
        ],
        out_specs=pl.BlockSpec((tm, D_MODEL), lambda i, f: (i, 0)),
        out_shape=jax.ShapeDtypeStruct((N_TOK, D_MODEL), F32),
        scratch_shapes=[pltpu.VMEM((tm, D_MODEL), BF16)],
        compiler_params=_params("arbitrary", "arbitrary"),
        name="swiglu_ffn",
    )(x, mod, norm_g, w_gu, w_gu, w_down, final_g)


def kernel(x_prompt, x_sample, c, state_rglru, c_ctx, norm1_g, norm2_g, w_ada, b_ada, w_in,
           hy_conv_w, hy_conv_b, hy_f_w1, hy_f_b1, hy_f_w2, hy_f_b2, hy_f_w3, hy_f_freq, hy_bias,
           w_a, w_b, rg_conv_w, rg_conv_b, rg_wr, rg_br, rg_wi, rg_bi, rg_lam, w_c, w_o,
           w_gu, w_down, final_g):
    x = jnp.concatenate([_to_time_permuted(x_prompt, SEQ), _to_time_permuted(x_sample, DEC_SEQ)])
    cvec = jnp.concatenate([c_ctx[None, :], c, jnp.zeros((MOD_ROWS - 1 - DEC_BATCH, D_MODEL), F32)])
    mod = _modulation(cvec, w_ada, b_ada)

    dft_short = _as_bf16(*_hyena_dft_tables(SEQ))
    dft_long = _as_bf16(*_hyena_dft_tables(DEC_SEQ))
    fnet_short, fnet_chan = _as_bf16(*_fnet_tables(SEQ))
    fnet_long, = _as_bf16(_fnet_tables(DEC_SEQ)[0])
    filt = (hy_f_w1, hy_f_b1, hy_f_w2, hy_f_b2, hy_f_w3, hy_f_freq)
    p_short = _hyena_filters(SEQ, dft_short[0], *filt)
    p_long = _hyena_filters(DEC_SEQ, dft_long[0], *filt)

    w_gates = 0.5 * jnp.concatenate([rg_wr[:, 0], rg_wi[:, 0], rg_wr[:, 1], rg_wi[:, 1]], axis=-1)
    lat_h0 = jnp.pad(state_rglru.astype(F32).transpose(1, 0, 2, 3)[:, :, None],
                     ((0, 0), (0, 0), (0, CTX_PER_CHUNK - 1), (0, 0), (0, 0)))
    h0_all = jnp.concatenate([jnp.zeros((DEPTH, NQ_CTX, CTX_PER_CHUNK, 2, D_RG), F32), lat_h0], axis=1)
    norm1 = norm1_g.reshape(DEPTH, 1, D_MODEL)
    norm2 = norm2_g.reshape(DEPTH, 1, D_MODEL)
    final = final_g.reshape(1, D_MODEL)

    states = []
    for l in range(DEPTH):
        u = _input_projection(x, mod, norm1, w_in, l)
        y_hy, y_fn, h_rg, st = _mixers(
            u, l, hy_conv_w, hy_conv_b, hy_bias, p_short, p_long, dft_short, dft_long,
            fnet_chan, fnet_short, fnet_long, h0_all[l], rg_conv_w, rg_conv_b, w_gates,
            rg_br, rg_bi, rg_lam)
        states.append(st[:NQ_CTX].reshape(BATCH, 2, D_RG))
        x = _merge(x, y_hy, y_fn, h_rg, mod, norm1, w_in, w_a, w_b, w_c, w_o, l)
        x = _ffn(x, mod, norm2, w_gu, w_down, final, l, final_norm=(l == DEPTH - 1))

    y_prompt = _from_time_permuted(x[:N_CTX_TOK], BATCH, SEQ)
    y_sample = _from_time_permuted(x[N_CTX_TOK:], DEC_BATCH, DEC_SEQ)
    new_state = jnp.stack(states, axis=1).astype(x_prompt.dtype)
    return (y_prompt, y_sample, new_state)
```
